```python
import jax
import jax.numpy as jnp
from jax import lax
import numpy as np

D_MODEL = 4096
BATCH = 1
SEQ = 8192
DEPTH = 1

ATT_PATTERNS = ((128, 1), (512, 4), (2048, 16))
N_ATT_GROUPS = 3
ATT_HEADS = 8
ATT_HEAD_DIM = 128
ATT_BLOCK = 64
ROPE_THETA = 10000.0
ATT_GROUP_W = ATT_HEADS * ATT_HEAD_DIM
ATT_W = N_ATT_GROUPS * ATT_GROUP_W

ML_HEADS = 8
ML_QK_DIM = 256
ML_V_DIM = 512
ML_CHUNK = 64
GATE_SOFTCAP = 15.0
ML_QK_W = ML_HEADS * ML_QK_DIM
ML_V_W = ML_HEADS * ML_V_DIM
N_ML_GATES = 4 * ML_HEADS

N_GROUPS = 8
EXPERTS_PER_GROUP = 8
N_EXPERTS = N_GROUPS * EXPERTS_PER_GROUP
TOP_K = 2
D_EXPERT = 768
MOE_BLOCK = 128

IN_SPLIT = (ATT_W, ATT_W, ATT_W, ML_QK_W, ML_QK_W, ML_V_W, ML_V_W, N_ML_GATES, D_MODEL, D_MODEL)
D_IN = sum(IN_SPLIT)
EPS = 1e-6
NEG = -1e30

kernel_name = 'hybrid_dilated_attn_mlstm_hier_moe'


def rms_norm(x, gain):
    xf = x.astype(jnp.float32)
    y = xf * lax.rsqrt(jnp.mean(xf * xf, axis=-1, keepdims=True) + EPS)
    return (y * gain.astype(jnp.float32)).astype(x.dtype)


def rotary(x, pos):
    half = x.shape[-1] // 2
    inv_freq = ROPE_THETA ** (-jnp.arange(half, dtype=jnp.float32) / half)
    ang = pos.astype(jnp.float32)[:, None] * inv_freq[None, :]
    shape = (1, x.shape[1]) + (1,) * (x.ndim - 3) + (half,)
    cos = jnp.cos(ang).reshape(shape)
    sin = jnp.sin(ang).reshape(shape)
    xf = x.astype(jnp.float32)
    x1, x2 = xf[..., :half], xf[..., half:]
    return jnp.concatenate([x1 * cos - x2 * sin, x2 * cos + x1 * sin], axis=-1).astype(x.dtype)


def dilated_window_attention(q, k, v, window, dilation):
    B, S, H, Dh = q.shape
    radius = window // (2 * dilation)
    blk = ATT_BLOCK
    assert radius <= blk
    L = S // dilation
    nb = -(-L // blk)
    Lp = nb * blk

    def to_residue(t):
        t = t.reshape(B, L, dilation, H, Dh).transpose(0, 2, 3, 1, 4)
        return jnp.pad(t, ((0, 0), (0, 0), (0, 0), (0, Lp - L), (0, 0)))

    def windows(t):
        tp = jnp.pad(t, ((0, 0), (0, 0), (0, 0), (blk, blk), (0, 0)))
        tp = tp.reshape(B, dilation, H, nb + 2, blk, Dh)
        return jnp.concatenate([tp[:, :, :, 0:nb], tp[:, :, :, 1:nb + 1], tp[:, :, :, 2:nb + 2]], axis=4)

    qb = to_residue(q).reshape(B, dilation, H, nb, blk, Dh).astype(jnp.float32)
    kw = windows(to_residue(k)).astype(jnp.float32)
    vw = windows(to_residue(v)).astype(jnp.float32)
    scores = jnp.einsum('brhnqd,brhnkd->brhnqk', qb, kw) * (Dh ** -0.5)
    uq = jnp.arange(nb)[:, None, None] * blk + jnp.arange(blk)[None, :, None]
    uk = (jnp.arange(nb)[:, None, None] - 1) * blk + jnp.arange(3 * blk)[None, None, :]
    valid = (jnp.abs(uk - uq) <= radius) & (uk >= 0) & (uk < L)
    scores = jnp.where(valid, scores, NEG)
    lse = jax.nn.logsumexp(scores, axis=-1)
    p = jnp.exp(scores - lse[..., None])
    o = jnp.einsum('brhnqk,brhnkd->brhnqd', p, vw)
    o = o.reshape(B, dilation, H, Lp, Dh)[:, :, :, :L].transpose(0, 3, 1, 2, 4).reshape(B, S, H, Dh)
    lse = lse.reshape(B, dilation, H, Lp)[..., :L].transpose(0, 3, 1, 2).reshape(B, S, H)
    return o, lse


def dilated_attention_mixer(aq, ak, av, q_gain, k_gain):
    B, S, _ = aq.shape
    shp = (B, S, N_ATT_GROUPS, ATT_HEADS, ATT_HEAD_DIM)
    pos = jnp.arange(S)
    q = rotary(rms_norm(aq.reshape(shp), q_gain[:, None, :]), pos)
    k = rotary(rms_norm(ak.reshape(shp), k_gain[:, None, :]), pos)
    v = av.reshape(shp)
    outs, lses = [], []
    for g, (window, dilation) in enumerate(ATT_PATTERNS):
        o, lse = dilated_window_attention(q[:, :, g], k[:, :, g], v[:, :, g], window, dilation)
        outs.append(o)
        lses.append(lse)
    o = jnp.stack(outs)
    w = jax.nn.softmax(jnp.stack(lses), axis=0)
    o = jnp.sum(w[..., None] * o, axis=0)
    return o.reshape(B, S, ATT_GROUP_W).astype(aq.dtype)


def mlstm_chunk_step(carry, xs):
    C, n, m = carry
    q, k, v, ig, lf = xs
    Lc = q.shape[2]
    b = jnp.cumsum(lf, axis=-1)
    log_d = b[..., :, None] - b[..., None, :] + ig[..., None, :]
    tri = jnp.tril(jnp.ones((Lc, Lc), dtype=bool))
    log_d = jnp.where(tri, log_d, NEG)
    m_inter = b + m[..., None]
    m_t = jnp.maximum(m_inter, jnp.max(log_d, axis=-1))
    decay_q = jnp.exp(m_inter - m_t)
    s = jnp.einsum('nhtd,nhsd->nhts', q, k) * jnp.exp(log_d - m_t[..., None])
    num = jnp.einsum('nhts,nhsv->nhtv', s, v) + decay_q[..., None] * jnp.einsum('nhtd,nhdv->nhtv', q, C)
    den = jnp.sum(s, axis=-1) + decay_q * jnp.einsum('nhtd,nhd->nht', q, n)
    h = num / jnp.maximum(jnp.abs(den), jnp.exp(-m_t))[..., None]
    m_last = m_t[..., -1]
    w_s = jnp.exp(b[..., -1:] - b + ig - m_last[..., None])
    decay_c = jnp.exp(b[..., -1] + m - m_last)
    C = decay_c[..., None, None] * C + jnp.einsum('nhs,nhsd,nhsv->nhdv', w_s, k, v)
    n = decay_c[..., None] * n + jnp.einsum('nhs,nhsd->nhd', w_s, k)
    return (C, n, m_last), h


def mlstm_mixer(mq, mk, mv, mo, mg, gate_bias, head_gain):
    B, S, _ = mq.shape
    f32 = jnp.float32
    q = mq.astype(f32).reshape(B, S, ML_HEADS, ML_QK_DIM)
    k = mk.astype(f32).reshape(B, S, ML_HEADS, ML_QK_DIM) * (ML_QK_DIM ** -0.5)
    v = mv.astype(f32).reshape(B, S, ML_HEADS, ML_V_DIM)
    pre = mg.astype(f32).reshape(B, S, 4, ML_HEADS) + gate_bias.astype(f32)
    pre = GATE_SOFTCAP * jnp.tanh(pre / GATE_SOFTCAP)
    i_fw, f_fw, i_bw, f_bw = pre[:, :, 0], pre[:, :, 1], pre[:, :, 2], pre[:, :, 3]

    def rev(t):
        return jnp.flip(t, axis=1)

    qd = jnp.concatenate([q, rev(q)], axis=0)
    kd = jnp.concatenate([k, rev(k)], axis=0)
    vd = jnp.concatenate([v, rev(v)], axis=0)
    igd = jnp.concatenate([i_fw, rev(i_bw)], axis=0)
    lfd = jnp.concatenate([jax.nn.log_sigmoid(f_fw), rev(jax.nn.log_sigmoid(f_bw))], axis=0)
    N2 = 2 * B
    nc = S // ML_CHUNK

    def chunk4(t):
        return t.reshape(N2, nc, ML_CHUNK, ML_HEADS, t.shape[-1]).transpose(1, 0, 3, 2, 4)

    def chunk3(t):
        return t.reshape(N2, nc, ML_CHUNK, ML_HEADS).transpose(1, 0, 3, 2)

    carry0 = (jnp.zeros((N2, ML_HEADS, ML_QK_DIM, ML_V_DIM), f32),
              jnp.zeros((N2, ML_HEADS, ML_QK_DIM), f32),
              jnp.zeros((N2, ML_HEADS), f32))
    _, hs = lax.scan(mlstm_chunk_step, carry0, (chunk4(qd), chunk4(kd), chunk4(vd), chunk3(igd), chunk3(lfd)))
    hs = hs.transpose(1, 0, 3, 2, 4).reshape(N2, S, ML_HEADS, ML_V_DIM)
    h = hs[:B] + rev(hs[B:])
    h = rms_norm(h, head_gain) * jax.nn.sigmoid(mo.astype(f32)).reshape(B, S, ML_HEADS, ML_V_DIM)
    return h.reshape(B, S, ML_V_W).astype(mq.dtype)


def hier_moe(h, w_rg, b_rg, w_re, b_re, w_gate, w_up, w_down):
    N, D = h.shape
    glog = (h @ w_rg + b_rg).astype(jnp.float32)
    gprob = jax.nn.softmax(glog, axis=-1)
    g_sel = jnp.argmax(glog, axis=-1)
    p_g = jnp.take_along_axis(gprob, g_sel[:, None], axis=1)[:, 0]
    elog = (h @ w_re + b_re).astype(jnp.float32).reshape(N, N_GROUPS, EXPERTS_PER_GROUP)
    elog = jnp.take_along_axis(elog, g_sel[:, None, None], axis=1)[:, 0]
    top_v, top_i = lax.top_k(elog, TOP_K)
    weights = p_g[:, None] * jax.nn.softmax(top_v, axis=-1)
    expert_ids = (g_sel[:, None] * EXPERTS_PER_GROUP + top_i).astype(jnp.int32)

    A = N * TOP_K
    blk = MOE_BLOCK
    flat_e = expert_ids.reshape(A)
    flat_tok = jnp.repeat(jnp.arange(N, dtype=jnp.int32), TOP_K)
    flat_w = weights.reshape(A)
    order = jnp.argsort(flat_e)
    se = flat_e[order]
    counts = jnp.zeros((N_EXPERTS,), jnp.int32).at[flat_e].add(1)
    padded = ((counts + blk - 1) // blk) * blk
    pad_end = jnp.cumsum(padded)
    pad_start = pad_end - padded
    start = jnp.cumsum(counts) - counts
    dest = pad_start[se] + jnp.arange(A, dtype=jnp.int32) - start[se]
    P = A + N_EXPERTS * blk
    nblk = P // blk
    slot_tok = jnp.full((P,), N, jnp.int32).at[dest].set(flat_tok[order])
    slot_w = jnp.zeros((P,), jnp.float32).at[dest].set(flat_w[order])
    blk_expert = jnp.minimum(jnp.searchsorted(pad_end, jnp.arange(nblk, dtype=jnp.int32) * blk, side='right'),
                             N_EXPERTS - 1)
    h_pad = jnp.concatenate([h, jnp.zeros((1, D), h.dtype)], axis=0)
    xs = h_pad[slot_tok].reshape(nblk, blk, D)

    def expert_block(args):
        xb, e = args
        return (jax.nn.silu(xb @ w_gate[e]) * (xb @ w_up[e])) @ w_down[e]

    ys = lax.map(expert_block, (xs, blk_expert)).reshape(P, D)
    ys = ys * slot_w[:, None].astype(ys.dtype)
    return jnp.zeros((N + 1, D), ys.dtype).at[slot_tok].add(ys)[:N]


def setup_inputs(seed: int = 0) -> dict:
    key = jax.random.key(seed)
    ks = jax.random.split(key, 20)
    f32 = jnp.float32

    def nrm(k, shape, scale):
        return jax.random.normal(k, shape, f32) * scale

    i_bias = -1.0 + 0.1 * jax.random.normal(ks[5], (DEPTH, 2, ML_HEADS), f32)
    f_bias = 3.0 + 3.0 * jax.random.uniform(ks[6], (DEPTH, 2, ML_HEADS), f32)
    mlstm_gate_bias = jnp.stack([i_bias[:, 0], f_bias[:, 0], i_bias[:, 1], f_bias[:, 1]], axis=1)
    return {
        'x': nrm(ks[0], (BATCH, SEQ, D_MODEL), 1.0),
        'norm1_gain': 1.0 + nrm(ks[1], (DEPTH, D_MODEL), 0.01),
        'w_in': nrm(ks[2], (DEPTH, D_MODEL, D_IN), D_MODEL ** -0.5),
        'attn_q_norm_gain': 1.0 + nrm(ks[3], (DEPTH, N_ATT_GROUPS, ATT_HEAD_DIM), 0.01),
        'attn_k_norm_gain': 1.0 + nrm(ks[4], (DEPTH, N_ATT_GROUPS, ATT_HEAD_DIM), 0.01),
        'mlstm_gate_bias': mlstm_gate_bias,
        'mlstm_head_norm_gain': 1.0 + nrm(ks[7], (DEPTH, ML_HEADS, ML_V_DIM), 0.01),
        'branch_gate_bias': nrm(ks[8], (DEPTH, 2, D_MODEL), 0.1),
        'w_attn_branch': nrm(ks[9], (DEPTH, ATT_GROUP_W, D_MODEL), ATT_GROUP_W ** -0.5),
        'w_mlstm_branch': nrm(ks[10], (DEPTH, ML_V_W, D_MODEL), ML_V_W ** -0.5),
        'w_out': nrm(ks[11], (DEPTH, D_MODEL, D_MODEL), D_MODEL ** -0.5),
        'norm2_gain': 1.0 + nrm(ks[12], (DEPTH, D_MODEL), 0.01),
        'w_router_group': nrm(ks[13], (DEPTH, D_MODEL, N_GROUPS), D_MODEL ** -0.5),
        'b_router_group': nrm(ks[14], (DEPTH, N_GROUPS), 0.01),
        'w_router_expert': nrm(ks[15], (DEPTH, D_MODEL, N_EXPERTS), D_MODEL ** -0.5),
        'b_router_expert': nrm(ks[16], (DEPTH, N_EXPERTS), 0.01),
        'w_expert_gate': nrm(ks[17], (DEPTH, N_EXPERTS, D_MODEL, D_EXPERT), D_MODEL ** -0.5),
        'w_expert_up': nrm(ks[18], (DEPTH, N_EXPERTS, D_MODEL, D_EXPERT), D_MODEL ** -0.5),
        'w_expert_down': nrm(ks[19], (DEPTH, N_EXPERTS, D_EXPERT, D_MODEL), D_EXPERT ** -0.5),
    }


def reference(x, norm1_gain, w_in, attn_q_norm_gain, attn_k_norm_gain, mlstm_gate_bias,
              mlstm_head_norm_gain, branch_gate_bias, w_attn_branch, w_mlstm_branch, w_out,
              norm2_gain, w_router_group, b_router_group, w_router_expert, b_router_expert,
              w_expert_gate, w_expert_up, w_expert_down):
    B, S, D = x.shape
    offs = []
    acc = 0
    for width in IN_SPLIT[:-1]:
        acc += width
        offs.append(acc)
    for l in range(DEPTH):
        h = rms_norm(x, norm1_gain[l])
        proj = h @ w_in[l]
        aq, ak, av, mq, mk, mv, mo, mg, gate_a, gate_m = jnp.split(proj, offs, axis=-1)
        att = dilated_attention_mixer(aq, ak, av, attn_q_norm_gain[l], attn_k_norm_gain[l])
        mem = mlstm_mixer(mq, mk, mv, mo, mg, mlstm_gate_bias[l], mlstm_head_norm_gain[l])
        g_a = jax.nn.sigmoid(gate_a + branch_gate_bias[l, 0])
        g_m = jax.nn.sigmoid(gate_m + branch_gate_bias[l, 1])
        merged = g_a * (att @ w_attn_branch[l]) + g_m * (mem @ w_mlstm_branch[l])
        x = x + merged @ w_out[l]
        h2 = rms_norm(x, norm2_gain[l]).reshape(B * S, D)
        moe = hier_moe(h2, w_router_group[l], b_router_group[l], w_router_expert[l], b_router_expert[l],
                       w_expert_gate[l], w_expert_up[l], w_expert_down[l])
        x = x + moe.reshape(B, S, D)
    return x
```

```python
import functools

import jax
import jax.numpy as jnp
from jax import lax
from jax.experimental import pallas as pl
from jax.experimental.pallas import tpu as pltpu

F32 = jnp.float32
BF16 = jnp.bfloat16

EPS = 1e-6
NEG = -1e30
LANES = 128

ATT_PATTERNS = ((128, 1), (512, 4), (2048, 16))
N_ATT_GROUPS = 3
ATT_HEADS = 8
ATT_HEAD_DIM = 128
ATT_GROUP_W = ATT_HEADS * ATT_HEAD_DIM
ATT_W = N_ATT_GROUPS * ATT_GROUP_W
ROPE_THETA = 10000.0
ATT_Q_TILE = 128

ML_HEADS = 8
ML_QK_DIM = 256
ML_V_DIM = 512
ML_QK_W = ML_HEADS * ML_QK_DIM
ML_V_W = ML_HEADS * ML_V_DIM
N_ML_GATES = 4 * ML_HEADS
GATE_SOFTCAP = 15.0
ML_TILE = 256

N_GROUPS = 8
EXPERTS_PER_GROUP = 8
N_EXPERTS = N_GROUPS * EXPERTS_PER_GROUP
TOP_K = 2
MOE_ROWS = 512
MOE_FF_TILE = 128
COMBINE_ROWS = 256

OFF_AQ = 0
OFF_AK = OFF_AQ + ATT_W
OFF_AV = OFF_AK + ATT_W
OFF_MQ = OFF_AV + ATT_W
OFF_MK = OFF_MQ + ML_QK_W
OFF_MV = OFF_MK + ML_QK_W
OFF_MO = OFF_MV + ML_V_W
OFF_MG = OFF_MO + ML_V_W
OFF_GA = OFF_MG

VMEM_LIMIT = 56 * 1024 * 1024


def _cparams(sem, vmem=VMEM_LIMIT):
    return pltpu.CompilerParams(dimension_semantics=sem, vmem_limit_bytes=vmem)


def _rmsnorm_kernel(x_ref, g_ref, o_ref):
    x = x_ref[...]
    ms = jnp.mean(x * x, axis=-1, keepdims=True)
    o_ref[...] = (x * lax.rsqrt(ms + EPS) * g_ref[...]).astype(o_ref.dtype)


def _rmsnorm(x, gain, tm=256):
    m, d = x.shape
    return pl.pallas_call(
        _rmsnorm_kernel,
        grid=(m // tm,),
        in_specs=[pl.BlockSpec((tm, d), lambda i: (i, 0)),
                  pl.BlockSpec((1, d), lambda i: (0, 0))],
        out_specs=pl.BlockSpec((tm, d), lambda i: (i, 0)),
        out_shape=jax.ShapeDtypeStruct((m, d), BF16),
        compiler_params=_cparams(("parallel",)),
        name="rmsnorm",
    )(x, gain.reshape(1, d))


def _mm_kernel(a_ref, w_ref, o_ref):
    o_ref[...] = jnp.dot(a_ref[...], w_ref[...], preferred_element_type=F32).astype(o_ref.dtype)


def _matmul(a, w, out_dtype, tm, tn, name):
    m, k = a.shape
    n = w.shape[1]
    return pl.pallas_call(
        _mm_kernel,
        grid=(m // tm, n // tn),
        in_specs=[pl.BlockSpec((tm, k), lambda i, j: (i, 0)),
                  pl.BlockSpec((k, tn), lambda i, j: (0, j))],
        out_specs=pl.BlockSpec((tm, tn), lambda i, j: (i, j)),
        out_shape=jax.ShapeDtypeStruct((m, n), out_dtype),
        compiler_params=_cparams(("parallel", "arbitrary")),
        name=name,
    )(a, w)


def _mm_res_kernel(a_ref, w_ref, r_ref, o_ref):
    o_ref[...] = r_ref[...] + jnp.dot(a_ref[...], w_ref[...], preferred_element_type=F32)


def _matmul_residual(a, w, res, tm, tn):
    m, k = a.shape
    n = w.shape[1]
    return pl.pallas_call(
        _mm_res_kernel,
        grid=(m // tm, n // tn),
        in_specs=[pl.BlockSpec((tm, k), lambda i, j: (i, 0)),
                  pl.BlockSpec((k, tn), lambda i, j: (0, j)),
                  pl.BlockSpec((tm, tn), lambda i, j: (i, j))],
        out_specs=pl.BlockSpec((tm, tn), lambda i, j: (i, j)),
        out_shape=jax.ShapeDtypeStruct((m, n), F32),
        compiler_params=_cparams(("parallel", "arbitrary")),
        name="out_proj",
    )(a, w, res)


def _attn_kernel(q_ref, k_ref, v_ref, cos_ref, sin_ref, qg_ref, kg_ref, o_ref, lse_ref,
                 qn_ref, kn_ref, *, sub_len, radius):
    tq = ATT_Q_TILE
    win = tq + 2 * radius
    scale = ATT_HEAD_DIM ** -0.5

    def prep(i, carry):
        rows = pl.ds(pl.multiple_of(i * tq, tq), tq)
        c = cos_ref[rows, :]
        s = sin_ref[rows, :]
        for src, g_ref, dst, mul in ((q_ref, qg_ref, qn_ref, scale), (k_ref, kg_ref, kn_ref, 1.0)):
            xf = src[rows, :].astype(F32)
            y = xf * lax.rsqrt(jnp.mean(xf * xf, axis=-1, keepdims=True) + EPS) * g_ref[...]
            out = y * c + pltpu.roll(y, ATT_HEAD_DIM // 2, axis=1) * s
            dst[rows, :] = (out * mul).astype(BF16)
        return carry

    lax.fori_loop(0, sub_len // tq, prep, 0)

    def tile(i, carry):
        q0 = pl.multiple_of(i * tq, tq)
        k0 = pl.multiple_of(jnp.clip(q0 - radius, 0, sub_len - win), radius)
        q = qn_ref[pl.ds(q0, tq), :]
        k = kn_ref[pl.ds(k0, win), :]
        v = v_ref[pl.ds(k0, win), :]
        s = lax.dot_general(q, k, (((1,), (1,)), ((), ())), preferred_element_type=F32)
        uq = q0 + lax.broadcasted_iota(jnp.int32, (tq, win), 0)
        uk = k0 + lax.broadcasted_iota(jnp.int32, (tq, win), 1)
        s = jnp.where(jnp.abs(uk - uq) <= radius, s, NEG)
        m = jnp.max(s, axis=-1, keepdims=True)
        p = jnp.exp(s - m)
        den = jnp.sum(p, axis=-1, keepdims=True)
        o = jnp.dot(p.astype(BF16), v, preferred_element_type=F32) / den
        o_ref[pl.ds(q0, tq), :] = o
        lse_ref[pl.ds(q0, tq), :] = jnp.broadcast_to(m + jnp.log(den), (tq, ATT_HEAD_DIM))
        return carry

    lax.fori_loop(0, sub_len // tq, tile, 0)


def _attention_group(proj, cos_t, sin_t, q_gain, k_gain, group, window, dilation):
    s_len, n_proj = proj.shape
    radius = window // (2 * dilation)
    sub_len = s_len // dilation
    cb = n_proj // LANES
    qb = OFF_AQ // LANES + group * ATT_HEADS
    kb = OFF_AK // LANES + group * ATT_HEADS
    vb = OFF_AV // LANES + group * ATT_HEADS
    proj_r = proj.reshape(sub_len, dilation * n_proj)
    cos_r = cos_t.reshape(sub_len, dilation * ATT_HEAD_DIM)
    sin_r = sin_t.reshape(sub_len, dilation * ATT_HEAD_DIM)
    blk = (sub_len, ATT_HEAD_DIM)
    kern = functools.partial(_attn_kernel, sub_len=sub_len, radius=radius)
    o, lse = pl.pallas_call(
        kern,
        grid=(dilation, ATT_HEADS),
        in_specs=[pl.BlockSpec(blk, lambda r, h: (0, r * cb + qb + h)),
                  pl.BlockSpec(blk, lambda r, h: (0, r * cb + kb + h)),
                  pl.BlockSpec(blk, lambda r, h: (0, r * cb + vb + h)),
                  pl.BlockSpec(blk, lambda r, h: (0, r)),
                  pl.BlockSpec(blk, lambda r, h: (0, r)),
                  pl.BlockSpec((1, ATT_HEAD_DIM), lambda r, h: (0, 0)),
                  pl.BlockSpec((1, ATT_HEAD_DIM), lambda r, h: (0, 0))],
        out_specs=[pl.BlockSpec(blk, lambda r, h: (0, r * ATT_HEADS + h)),
                   pl.BlockSpec(blk, lambda r, h: (0, r * ATT_HEADS + h))],
        out_shape=[jax.ShapeDtypeStruct((sub_len, dilation * ATT_GROUP_W), F32),
                   jax.ShapeDtypeStruct((sub_len, dilation * ATT_GROUP_W), F32)],
        scratch_shapes=[pltpu.VMEM(blk, BF16), pltpu.VMEM(blk, BF16)],
        compiler_params=_cparams(("parallel", "parallel")),
        name=f"dilated_attn_g{group}",
    )(proj_r, proj_r, proj_r, cos_r, sin_r, q_gain[group:group + 1], k_gain[group:group + 1])
    return o.reshape(s_len, ATT_GROUP_W), lse.reshape(s_len, ATT_GROUP_W)


def _attn_merge_kernel(o0, o1, o2, l0, l1, l2, out_ref):
    a, b, c = l0[...], l1[...], l2[...]
    m = jnp.maximum(jnp.maximum(a, b), c)
    ea, eb, ec = jnp.exp(a - m), jnp.exp(b - m), jnp.exp(c - m)
    tot = ea + eb + ec
    out_ref[...] = ((ea * o0[...] + eb * o1[...] + ec * o2[...]) / tot).astype(out_ref.dtype)


def _attn_merge(outs, lses, tm=512):
    m, n = outs[0].shape
    spec = pl.BlockSpec((tm, n), lambda i: (i, 0))
    return pl.pallas_call(
        _attn_merge_kernel,
        grid=(m // tm,),
        in_specs=[spec] * 6,
        out_specs=spec,
        out_shape=jax.ShapeDtypeStruct((m, n), BF16),
        compiler_params=_cparams(("parallel",)),
        name="attn_merge",
    )(*outs, *lses)


def _mlstm_kernel(q_ref, k_ref, v_ref, g_ref, b_ref, o_ref, c_ref, n_ref, m_ref):
    lc = ML_TILE
    dh = pl.program_id(0)
    step = pl.program_id(1)
    is_bw = dh >= ML_HEADS
    sgn = jnp.where(is_bw, -1, 1)

    @pl.when(step == 0)
    def _():
        c_ref[...] = jnp.zeros_like(c_ref)
        n_ref[...] = jnp.zeros_like(n_ref)
        m_ref[...] = jnp.zeros_like(m_ref)

    g = g_ref[0, 0] + b_ref[0]
    g = GATE_SOFTCAP * jnp.tanh(g / GATE_SOFTCAP)
    ig_row = g[0:1, :]
    f = g[1:2, :]
    lf_row = jnp.minimum(f, 0.0) - jnp.log1p(jnp.exp(-jnp.abs(f)))

    t_idx = lax.broadcasted_iota(jnp.int32, (lc, lc), 0)
    s_idx = lax.broadcasted_iota(jnp.int32, (lc, lc), 1)
    valid = (s_idx - t_idx) * sgn <= 0
    eye = s_idx == t_idx
    b_col = jnp.sum(jnp.where(valid, lf_row, 0.0), axis=1, keepdims=True)
    b_row = jnp.sum(jnp.where(eye, b_col, 0.0), axis=0, keepdims=True)
    ig_col = jnp.sum(jnp.where(eye, ig_row, 0.0), axis=1, keepdims=True)

    log_d = jnp.where(valid, b_col - b_row + ig_row, NEG)
    m_prev = m_ref[...]
    m_inter = b_col + m_prev
    m_t = jnp.maximum(m_inter, jnp.max(log_d, axis=1, keepdims=True))
    decay_q = jnp.exp(m_inter - m_t)
    d_mat = jnp.exp(log_d - m_t)

    q = q_ref[...]
    k = k_ref[...] * (ML_QK_DIM ** -0.5)
    v = v_ref[...]
    s = lax.dot_general(q, k, (((1,), (1,)), ((), ())), preferred_element_type=F32) * d_mat
    c_old = c_ref[...]
    num = jnp.dot(s.astype(BF16), v, preferred_element_type=F32)
    num = num + decay_q * jnp.dot(q, c_old.astype(BF16), preferred_element_type=F32)
    qn = jnp.sum(q.astype(F32) * n_ref[...], axis=1, keepdims=True)
    den = jnp.sum(s, axis=1, keepdims=True) + decay_q * qn
    o_ref[0] = num / jnp.maximum(jnp.abs(den), jnp.exp(-m_t))

    last = jnp.where(is_bw, 0, lc - 1)
    sel = lax.broadcasted_iota(jnp.int32, (lc, 1), 0) == last
    m_last = jnp.max(jnp.where(sel, m_t, -jnp.inf), axis=0, keepdims=True)
    b_last = jnp.sum(jnp.where(sel, b_col, 0.0), axis=0, keepdims=True)
    w_col = jnp.exp(b_last - b_col + ig_col - m_last)
    decay_c = jnp.exp(b_last + m_prev - m_last)
    kw = k.astype(F32) * w_col
    c_ref[...] = decay_c * c_old + lax.dot_general(
        kw.astype(BF16), v, (((0,), (0,)), ((), ())), preferred_element_type=F32)
    n_ref[...] = decay_c * n_ref[...] + jnp.sum(kw, axis=0, keepdims=True)
    m_ref[...] = m_last


def _mlstm(proj, gates, gate_bias):
    s_len = proj.shape[0]
    lc = ML_TILE
    nt = s_len // lc
    gt = gates.reshape(s_len, 2, 2, ML_HEADS).transpose(1, 3, 2, 0)
    gt = gt.reshape(2 * ML_HEADS, 2, nt, lc).transpose(0, 2, 1, 3)
    bias = gate_bias.reshape(2, 2, ML_HEADS).transpose(0, 2, 1).reshape(2 * ML_HEADS, 2, 1)

    def tile_of(dh, t):
        return jnp.where(dh >= ML_HEADS, nt - 1 - t, t)

    qb, kb = OFF_MQ // ML_QK_DIM, OFF_MK // ML_QK_DIM
    vb = OFF_MV // ML_V_DIM
    return pl.pallas_call(
        _mlstm_kernel,
        grid=(2 * ML_HEADS, nt),
        in_specs=[pl.BlockSpec((lc, ML_QK_DIM), lambda dh, t: (tile_of(dh, t), qb + dh % ML_HEADS)),
                  pl.BlockSpec((lc, ML_QK_DIM), lambda dh, t: (tile_of(dh, t), kb + dh % ML_HEADS)),
                  pl.BlockSpec((lc, ML_V_DIM), lambda dh, t: (tile_of(dh, t), vb + dh % ML_HEADS)),
                  pl.BlockSpec((1, 1, 2, lc), lambda dh, t: (dh, tile_of(dh, t), 0, 0)),
                  pl.BlockSpec((1, 2, 1), lambda dh, t: (dh, 0, 0))],
        out_specs=pl.BlockSpec((1, lc, ML_V_DIM),
                               lambda dh, t: (dh // ML_HEADS, tile_of(dh, t), dh % ML_HEADS)),
        out_shape=jax.ShapeDtypeStruct((2, s_len, ML_V_W), F32),
        scratch_shapes=[pltpu.VMEM((ML_QK_DIM, ML_V_DIM), F32),
                        pltpu.VMEM((1, ML_QK_DIM), F32),
                        pltpu.VMEM((1, 1), F32)],
        compiler_params=_cparams(("parallel", "arbitrary")),
        name="mlstm_scan",
    )(proj, proj, proj, gt, bias)


def _mlstm_out_kernel(h_ref, mo_ref, g_ref, o_ref):
    h = h_ref[0] + h_ref[1]
    y = h * lax.rsqrt(jnp.mean(h * h, axis=-1, keepdims=True) + EPS) * g_ref[0]
    o_ref[...] = (y * jax.nn.sigmoid(mo_ref[...].astype(F32))).astype(o_ref.dtype)


def _mlstm_out(hs, proj, head_gain, tm=512):
    s_len = proj.shape[0]
    ob = OFF_MO // ML_V_DIM
    return pl.pallas_call(
        _mlstm_out_kernel,
        grid=(s_len // tm, ML_HEADS),
        in_specs=[pl.BlockSpec((2, tm, ML_V_DIM), lambda i, h: (0, i, h)),
                  pl.BlockSpec((tm, ML_V_DIM), lambda i, h: (i, ob + h)),
                  pl.BlockSpec((1, 1, ML_V_DIM), lambda i, h: (h, 0, 0))],
        out_specs=pl.BlockSpec((tm, ML_V_DIM), lambda i, h: (i, h)),
        out_shape=jax.ShapeDtypeStruct((s_len, ML_V_W), BF16),
        compiler_params=_cparams(("parallel", "parallel")),
        name="mlstm_out",
    )(hs, proj, head_gain.reshape(ML_HEADS, 1, ML_V_DIM))


def _branch_kernel(att_ref, mem_ref, wa_ref, wm_ref, ga_ref, gm_ref, ba_ref, bm_ref, o_ref):
    ya = jnp.dot(att_ref[...], wa_ref[...], preferred_element_type=F32)
    ym = jnp.dot(mem_ref[...], wm_ref[...], preferred_element_type=F32)
    g_a = jax.nn.sigmoid(ga_ref[...].astype(F32) + ba_ref[...])
    g_m = jax.nn.sigmoid(gm_ref[...].astype(F32) + bm_ref[...])
    o_ref[...] = (g_a * ya + g_m * ym).astype(o_ref.dtype)


def _branch_mix(att, mem, w_a, w_m, proj, gate_bias, tm=512, tn=512):
    m, d = att.shape[0], w_a.shape[1]
    gab = OFF_GA // tn
    gmb = (OFF_GA + d) // tn
    return pl.pallas_call(
        _branch_kernel,
        grid=(m // tm, d // tn),
        in_specs=[pl.BlockSpec((tm, att.shape[1]), lambda i, j: (i, 0)),
                  pl.BlockSpec((tm, mem.shape[1]), lambda i, j: (i, 0)),
                  pl.BlockSpec((w_a.shape[0], tn), lambda i, j: (0, j)),
                  pl.BlockSpec((w_m.shape[0], tn), lambda i, j: (0, j)),
                  pl.BlockSpec((tm, tn), lambda i, j: (i, gab + j)),
                  pl.BlockSpec((tm, tn), lambda i, j: (i, gmb + j)),
                  pl.BlockSpec((1, tn), lambda i, j: (0, j)),
                  pl.BlockSpec((1, tn), lambda i, j: (0, j))],
        out_specs=pl.BlockSpec((tm, tn), lambda i, j: (i, j)),
        out_shape=jax.ShapeDtypeStruct((m, d), BF16),
        compiler_params=_cparams(("parallel", "arbitrary")),
        name="branch_mix",
    )(att, mem, w_a, w_m, proj, proj, gate_bias[0:1], gate_bias[1:2])


def _norm_router_kernel(x_ref, g_ref, wr_ref, br_ref, h_ref, lg_ref):
    x = x_ref[...]
    h = x * lax.rsqrt(jnp.mean(x * x, axis=-1, keepdims=True) + EPS) * g_ref[...]
    h_ref[...] = h
    lg_ref[...] = jnp.dot(h, wr_ref[...], preferred_element_type=F32,
                          precision=lax.Precision.HIGHEST) + br_ref[...]


def _norm_router(x, gain, w_r, b_r, tm=256):
    m, d = x.shape
    n = w_r.shape[1]
    return pl.pallas_call(
        _norm_router_kernel,
        grid=(m // tm,),
        in_specs=[pl.BlockSpec((tm, d), lambda i: (i, 0)),
                  pl.BlockSpec((1, d), lambda i: (0, 0)),
                  pl.BlockSpec((d, n), lambda i: (0, 0)),
                  pl.BlockSpec((1, n), lambda i: (0, 0))],
        out_specs=[pl.BlockSpec((tm, d), lambda i: (i, 0)),
                   pl.BlockSpec((tm, n), lambda i: (i, 0))],
        out_shape=[jax.ShapeDtypeStruct((m, d), F32),
                   jax.ShapeDtypeStruct((m, n), F32)],
        compiler_params=_cparams(("parallel",)),
        name="norm_router",
    )(x, gain.reshape(1, d), w_r, b_r)


def _moe_kernel(blk_e_ref, nused_ref, tok_ref, h_hbm, sw_ref, wg_ref, wu_ref, wd_ref, y_ref,
                xg_ref, xb_ref, sem, *, n_ff_tiles):
    b = pl.program_id(0)
    j = pl.program_id(1)
    rows = MOE_ROWS
    used = b < nused_ref[0]

    def row_copy(i, tok):
        return pltpu.make_async_copy(h_hbm.at[pl.ds(tok, 1)], xg_ref.at[pl.ds(i, 1)], sem)

    @pl.when(jnp.logical_and(used, j == 0))
    def _():
        def issue(i, carry):
            row_copy(i, tok_ref[0, 0, i]).start()
            return carry

        def drain(i, carry):
            row_copy(i, 0).wait()
            return carry

        lax.fori_loop(0, rows, issue, 0)
        lax.fori_loop(0, rows, drain, 0)
        xb_ref[...] = xg_ref[...].astype(BF16)

    @pl.when(used)
    def _():
        x = xb_ref[...]
        g = jnp.dot(x, wg_ref[0].astype(BF16), preferred_element_type=F32)
        u = jnp.dot(x, wu_ref[0].astype(BF16), preferred_element_type=F32)
        a = (g * jax.nn.sigmoid(g) * u).astype(BF16)
        part = jnp.dot(a, wd_ref[0].astype(BF16), preferred_element_type=F32)

        @pl.when(j == 0)
        def _():
            y_ref[...] = part

        @pl.when(j > 0)
        def _():
            y_ref[...] += part

        @pl.when(j == n_ff_tiles - 1)
        def _():
            y_ref[...] = y_ref[...] * sw_ref[...]

    @pl.when(jnp.logical_and(jnp.logical_not(used), j == 0))
    def _():
        y_ref[...] = jnp.zeros_like(y_ref)


def _moe_experts(h2, blk_expert, n_used, slot_tok, slot_w, w_gate, w_up, w_down):
    n_tok, d = h2.shape
    d_ff = w_gate.shape[2]
    rows = MOE_ROWS
    n_blk = slot_tok.shape[0] // rows
    tj = MOE_FF_TILE
    n_j = d_ff // tj

    def ff_tile(b, j, nu):
        return jnp.where(b < nu[0], j, n_j - 1)

    grid_spec = pltpu.PrefetchScalarGridSpec(
        num_scalar_prefetch=2,
        grid=(n_blk, n_j),
        in_specs=[pl.BlockSpec((1, 1, rows), lambda b, j, be, nu: (b, 0, 0), memory_space=pltpu.SMEM),
                  pl.BlockSpec(memory_space=pl.ANY),
                  pl.BlockSpec((rows, 1), lambda b, j, be, nu: (b, 0)),
                  pl.BlockSpec((1, d, tj), lambda b, j, be, nu: (be[b], 0, ff_tile(b, j, nu))),
                  pl.BlockSpec((1, d, tj), lambda b, j, be, nu: (be[b], 0, ff_tile(b, j, nu))),
                  pl.BlockSpec((1, tj, d), lambda b, j, be, nu: (be[b], ff_tile(b, j, nu), 0))],
        out_specs=pl.BlockSpec((rows, d), lambda b, j, be, nu: (b, 0)),
        scratch_shapes=[pltpu.VMEM((rows, d), F32),
                        pltpu.VMEM((rows, d), BF16),
                        pltpu.SemaphoreType.DMA(())],
    )
    return pl.pallas_call(
        functools.partial(_moe_kernel, n_ff_tiles=n_j),
        grid_spec=grid_spec,
        out_shape=jax.ShapeDtypeStruct((n_blk * rows, d), F32),
        compiler_params=_cparams(("arbitrary", "arbitrary")),
        name="moe_experts",
    )(blk_expert, n_used, slot_tok.reshape(n_blk, 1, rows), h2, slot_w.reshape(-1, 1),
      w_gate, w_up, w_down)


def _combine_kernel(pos_ref, x_ref, y_hbm, o_ref, buf_ref, sem):
    rows = COMBINE_ROWS

    def row_copy(i, src):
        return pltpu.make_async_copy(y_hbm.at[pl.ds(src, 1)], buf_ref.at[pl.ds(i, 1)], sem)

    def issue(i, carry):
        row_copy(i, pos_ref[0, 0, i]).start()
        return carry

    def drain(i, carry):
        row_copy(i, 0).wait()
        return carry

    lax.fori_loop(0, TOP_K * rows, issue, 0)
    lax.fori_loop(0, TOP_K * rows, drain, 0)
    o_ref[...] = x_ref[...] + (buf_ref[pl.ds(0, rows), :] + buf_ref[pl.ds(rows, rows), :])


def _moe_combine(x1, ys, pos):
    n_tok, d = x1.shape
    rows = COMBINE_ROWS
    n_blk = n_tok // rows
    pos_b = pos.reshape(n_blk, rows, TOP_K).transpose(0, 2, 1).reshape(n_blk, 1, TOP_K * rows)
    return pl.pallas_call(
        _combine_kernel,
        grid=(n_blk,),
        in_specs=[pl.BlockSpec((1, 1, TOP_K * rows), lambda i: (i, 0, 0), memory_space=pltpu.SMEM),
                  pl.BlockSpec((rows, d), lambda i: (i, 0)),
                  pl.BlockSpec(memory_space=pl.ANY)],
        out_specs=pl.BlockSpec((rows, d), lambda i: (i, 0)),
        out_shape=jax.ShapeDtypeStruct((n_tok, d), F32),
        scratch_shapes=[pltpu.VMEM((TOP_K * rows, d), F32), pltpu.SemaphoreType.DMA(())],
        compiler_params=_cparams(("arbitrary",)),
        name="moe_combine",
    )(pos_b, x1, ys)


def _route(logits, n_tok):
    glog = logits[:, :N_GROUPS]
    gprob = jax.nn.softmax(glog, axis=-1)
    g_sel = jnp.argmax(glog, axis=-1)
    p_g = jnp.take_along_axis(gprob, g_sel[:, None], axis=1)[:, 0]
    elog = logits[:, N_GROUPS:N_GROUPS + N_EXPERTS].reshape(n_tok, N_GROUPS, EXPERTS_PER_GROUP)
    elog = jnp.take_along_axis(elog, g_sel[:, None, None], axis=1)[:, 0]
    top_v, top_i = lax.top_k(elog, TOP_K)
    weights = p_g[:, None] * jax.nn.softmax(top_v, axis=-1)
    expert_ids = (g_sel[:, None] * EXPERTS_PER_GROUP + top_i).astype(jnp.int32)

    n_assign = n_tok * TOP_K
    rows = MOE_ROWS
    flat_e = expert_ids.reshape(n_assign)
    flat_tok = jnp.repeat(jnp.arange(n_tok, dtype=jnp.int32), TOP_K)
    flat_w = weights.reshape(n_assign)
    order = jnp.argsort(flat_e)
    se = flat_e[order]
    counts = jnp.zeros((N_EXPERTS,), jnp.int32).at[flat_e].add(1)
    blocks = (counts + rows - 1) // rows
    pad_end = jnp.cumsum(blocks * rows)
    pad_start = pad_end - blocks * rows
    start = jnp.cumsum(counts) - counts
    dest = pad_start[se] + jnp.arange(n_assign, dtype=jnp.int32) - start[se]
    n_slots = n_assign + N_EXPERTS * rows
    n_blk = n_slots // rows
    slot_tok = jnp.zeros((n_slots,), jnp.int32).at[dest].set(flat_tok[order])
    slot_w = jnp.zeros((n_slots,), F32).at[dest].set(flat_w[order])
    n_used = jnp.sum(blocks).astype(jnp.int32)
    blk_ids = jnp.arange(n_blk, dtype=jnp.int32)
    blk_expert = jnp.minimum(jnp.searchsorted(pad_end, blk_ids * rows, side='right'),
                             N_EXPERTS - 1).astype(jnp.int32)
    blk_expert = jnp.where(blk_ids < n_used, blk_expert, blk_expert[jnp.maximum(n_used - 1, 0)])
    pos = jnp.zeros((n_assign,), jnp.int32).at[order].set(dest).reshape(n_tok, TOP_K)
    return blk_expert, n_used.reshape(1), slot_tok, slot_w, pos


def _rope_tables(s_len):
    half = ATT_HEAD_DIM // 2
    inv_freq = ROPE_THETA ** (-jnp.arange(half, dtype=F32) / half)
    ang = jnp.arange(s_len).astype(F32)[:, None] * inv_freq[None, :]
    cos, sin = jnp.cos(ang), jnp.sin(ang)
    return jnp.concatenate([cos, cos], axis=1), jnp.concatenate([-sin, sin], axis=1)


def kernel(x, norm1_gain, w_in, attn_q_norm_gain, attn_k_norm_gain, mlstm_gate_bias,
           mlstm_head_norm_gain, branch_gate_bias, w_attn_branch, w_mlstm_branch, w_out,
           norm2_gain, w_router_group, b_router_group, w_router_expert, b_router_expert,
           w_expert_gate, w_expert_up, w_expert_down):
    bsz, s_len, d = x.shape
    assert bsz == 1
    depth = w_in.shape[0]
    xs = x.reshape(s_len, d)
    cos_t, sin_t = _rope_tables(s_len)
    mg_end = OFF_MG + N_ML_GATES
    for l in range(depth):
        w_l = w_in[l]
        w_main = jnp.concatenate([w_l[:, :OFF_MG], w_l[:, mg_end:]], axis=1).astype(BF16)
        w_mg = jnp.pad(w_l[:, OFF_MG:mg_end], ((0, 0), (0, LANES - N_ML_GATES))).astype(BF16)

        h = _rmsnorm(xs, norm1_gain[l])
        proj = _matmul(h, w_main, BF16, 1024, 512, "in_proj")
        gates = _matmul(h, w_mg, F32, 1024, LANES, "gate_proj")[:, :N_ML_GATES]

        outs, lses = [], []
        for g, (window, dilation) in enumerate(ATT_PATTERNS):
            o, lse = _attention_group(proj, cos_t, sin_t, attn_q_norm_gain[l], attn_k_norm_gain[l],
                                      g, window, dilation)
            outs.append(o)
            lses.append(lse)
        att = _attn_merge(outs, lses)

        hs = _mlstm(proj, gates, mlstm_gate_bias[l])
        mem = _mlstm_out(hs, proj, mlstm_head_norm_gain[l])

        merged = _branch_mix(att, mem, w_attn_branch[l].astype(BF16), w_mlstm_branch[l].astype(BF16),
                             proj, branch_gate_bias[l])
        x1 = _matmul_residual(merged, w_out[l].astype(BF16), xs, 1024, 512)

        w_r = jnp.concatenate([w_router_group[l], w_router_expert[l]], axis=1)
        b_r = jnp.concatenate([b_router_group[l], b_router_expert[l]])
        n_r = w_r.shape[1]
        w_r = jnp.pad(w_r, ((0, 0), (0, LANES - n_r)))
        b_r = jnp.pad(b_r, (0, LANES - n_r)).reshape(1, LANES)
        h2, logits = _norm_router(x1, norm2_gain[l], w_r, b_r)

        blk_expert, n_used, slot_tok, slot_w, pos = _route(logits, s_len)
        ys = _moe_experts(h2, blk_expert, n_used, slot_tok, slot_w,
                          w_expert_gate[l], w_expert_up[l], w_expert_down[l])
        xs = _moe_combine(x1, ys, pos)
    return xs.reshape(bsz, s_len, d)
```

```python
import functools

import jax
import jax.numpy as jnp
from jax import lax
from jax.experimental import pallas as pl
from jax.experimental.pallas import tpu as pltpu

F32 = jnp.float32
BF16 = jnp.bfloat16

EPS = 1e-6
NEG = -1e30
LANES = 128

ATT_PATTERNS = ((128, 1), (512, 4), (2048, 16))
N_ATT_GROUPS = 3
ATT_HEADS = 8
ATT_HEAD_DIM = 128
ATT_GROUP_W = ATT_HEADS * ATT_HEAD_DIM
ATT_W = N_ATT_GROUPS * ATT_GROUP_W
ROPE_THETA = 10000.0
ATT_Q_TILE = 128
ATT_TILE_UNROLL = 4

ML_HEADS = 8
ML_QK_DIM = 256
ML_V_DIM = 512
ML_QK_W = ML_HEADS * ML_QK_DIM
ML_V_W = ML_HEADS * ML_V_DIM
N_ML_GATES = 4 * ML_HEADS
GATE_SOFTCAP = 15.0
ML_TILE = 256

N_GROUPS = 8
EXPERTS_PER_GROUP = 8
N_EXPERTS = N_GROUPS * EXPERTS_PER_GROUP
TOP_K = 2
MOE_ROWS = 512
MOE_FF_TILE = 256
MOE_OUT_TILE = 1024
COMBINE_ROWS = 256
DMA_ISSUE_UNROLL = 8

OFF_AQ = 0
OFF_AK = OFF_AQ + ATT_W
OFF_AV = OFF_AK + ATT_W
OFF_MQ = OFF_AV + ATT_W
OFF_MK = OFF_MQ + ML_QK_W
OFF_MV = OFF_MK + ML_QK_W
OFF_MO = OFF_MV + ML_V_W
OFF_MG = OFF_MO + ML_V_W
OFF_GA = OFF_MG

VMEM_LIMIT = 56 * 1024 * 1024


def _cparams(sem, vmem=VMEM_LIMIT):
    return pltpu.CompilerParams(dimension_semantics=sem, vmem_limit_bytes=vmem)


def _rmsnorm_kernel(x_ref, g_ref, o_ref):
    x = x_ref[...]
    ms = jnp.mean(x * x, axis=-1, keepdims=True)
    o_ref[...] = (x * lax.rsqrt(ms + EPS) * g_ref[...]).astype(o_ref.dtype)


def _rmsnorm(x, gain, tm=256):
    m, d = x.shape
    return pl.pallas_call(
        _rmsnorm_kernel,
        grid=(m // tm,),
        in_specs=[pl.BlockSpec((tm, d), lambda i: (i, 0)),
                  pl.BlockSpec((1, d), lambda i: (0, 0))],
        out_specs=pl.BlockSpec((tm, d), lambda i: (i, 0)),
        out_shape=jax.ShapeDtypeStruct((m, d), BF16),
        compiler_params=_cparams(("parallel",)),
        name="rmsnorm",
    )(x, gain.reshape(1, d))


def _mm_kernel(a_ref, w_ref, o_ref):
    o_ref[...] = jnp.dot(a_ref[...], w_ref[...], preferred_element_type=F32).astype(o_ref.dtype)


def _matmul(a, w, out_dtype, tm, tn, name):
    m, k = a.shape
    n = w.shape[1]
    return pl.pallas_call(
        _mm_kernel,
        grid=(m // tm, n // tn),
        in_specs=[pl.BlockSpec((tm, k), lambda i, j: (i, 0)),
                  pl.BlockSpec((k, tn), lambda i, j: (0, j))],
        out_specs=pl.BlockSpec((tm, tn), lambda i, j: (i, j)),
        out_shape=jax.ShapeDtypeStruct((m, n), out_dtype),
        compiler_params=_cparams(("parallel", "arbitrary")),
        name=name,
    )(a, w)


def _mm_res_kernel(a_ref, w_ref, r_ref, o_ref):
    o_ref[...] = r_ref[...] + jnp.dot(a_ref[...], w_ref[...], preferred_element_type=F32)


def _matmul_residual(a, w, res, tm, tn):
    m, k = a.shape
    n = w.shape[1]
    return pl.pallas_call(
        _mm_res_kernel,
        grid=(m // tm, n // tn),
        in_specs=[pl.BlockSpec((tm, k), lambda i, j: (i, 0)),
                  pl.BlockSpec((k, tn), lambda i, j: (0, j)),
                  pl.BlockSpec((tm, tn), lambda i, j: (i, j))],
        out_specs=pl.BlockSpec((tm, tn), lambda i, j: (i, j)),
        out_shape=jax.ShapeDtypeStruct((m, n), F32),
        compiler_params=_cparams(("parallel", "arbitrary")),
        name="out_proj",
    )(a, w, res)


def _attn_kernel(q_ref, k_ref, v_ref, tab_ref, qg_ref, kg_ref, o_ref,
                 qn_ref, kn_ref, vf_ref, acc_ref, lse_ref, *, s_len):
    grp = pl.program_id(1)
    tq = ATT_Q_TILE
    half = ATT_HEAD_DIM // 2
    scale = ATT_HEAD_DIM ** -0.5
    low = lax.broadcasted_iota(jnp.int32, (tq, ATT_HEAD_DIM), 1) < half

    def prep(i, carry):
        rows = pl.ds(pl.multiple_of(i * tq, tq), tq)
        tab = tab_ref[rows, :]
        swapped = pltpu.roll(tab, half, axis=1)
        c = jnp.where(low, tab, swapped)
        s = jnp.where(low, -swapped, tab)
        for src, g_ref, dst, mul in ((q_ref, qg_ref, qn_ref, scale), (k_ref, kg_ref, kn_ref, 1.0)):
            xf = src[rows, :].astype(F32)
            y = xf * lax.rsqrt(jnp.mean(xf * xf, axis=-1, keepdims=True) + EPS) * g_ref[0]
            dst[rows, :] = (y * c + pltpu.roll(y, half, axis=1) * s) * mul
        vf_ref[rows, :] = v_ref[rows, :].astype(F32)
        return carry

    lax.fori_loop(0, s_len // tq, prep, 0, unroll=2)

    def run_group(window, dilation, first):
        radius = window // (2 * dilation)
        sub_len = s_len // dilation
        win = tq + 2 * radius
        tiles_per_class = sub_len // tq
        rel = (lax.broadcasted_iota(jnp.int32, (tq, win), 1)
               - lax.broadcasted_iota(jnp.int32, (tq, win), 0))

        def rows_of(start, size):
            if dilation == 1:
                return pl.ds(start, size)
            return pl.ds(start, size, stride=dilation)

        def tile(idx, carry):
            r = idx // tiles_per_class
            q0 = (idx % tiles_per_class) * tq
            k0 = jnp.clip(q0 - radius, 0, sub_len - win)
            q_rows = rows_of(r + dilation * q0, tq)
            k_rows = rows_of(r + dilation * k0, win)
            q = qn_ref[q_rows, :].astype(BF16)
            k = kn_ref[k_rows, :].astype(BF16)
            v = vf_ref[k_rows, :].astype(BF16)
            s = lax.dot_general(q, k, (((1,), (1,)), ((), ())), preferred_element_type=F32)
            s = jnp.where(jnp.abs(rel + (k0 - q0)) <= radius, s, NEG)
            m = jnp.max(s, axis=-1, keepdims=True)
            p = jnp.exp(s - m)
            den = jnp.sum(p, axis=-1, keepdims=True)
            o = jnp.dot(p.astype(BF16), v, preferred_element_type=F32) / den
            lse = jnp.broadcast_to(m + jnp.log(den), (tq, ATT_HEAD_DIM))
            if first:
                acc_ref[q_rows, :] = o
                lse_ref[q_rows, :] = lse
            else:
                old_l = lse_ref[q_rows, :]
                top = jnp.maximum(old_l, lse)
                a = jnp.exp(old_l - top)
                b = jnp.exp(lse - top)
                tot = a + b
                acc_ref[q_rows, :] = (a * acc_ref[q_rows, :] + b * o) / tot
                lse_ref[q_rows, :] = top + jnp.log(tot)
            return carry

        lax.fori_loop(0, s_len // tq, tile, 0, unroll=ATT_TILE_UNROLL)

    for gi, (window, dilation) in enumerate(ATT_PATTERNS):
        @pl.when(grp == gi)
        def _(window=window, dilation=dilation, gi=gi):
            run_group(window, dilation, gi == 0)

    @pl.when(grp == N_ATT_GROUPS - 1)
    def _():
        o_ref[...] = acc_ref[...].astype(o_ref.dtype)


def _attention(proj, q_gain, k_gain):
    s_len = proj.shape[0]
    half = ATT_HEAD_DIM // 2
    inv_freq = ROPE_THETA ** (-jnp.arange(half, dtype=F32) / half)
    ang = jnp.arange(s_len).astype(F32)[:, None] * inv_freq[None, :]
    table = jnp.concatenate([jnp.cos(ang), jnp.sin(ang)], axis=1)
    qb, kb, vb = OFF_AQ // LANES, OFF_AK // LANES, OFF_AV // LANES
    blk = (s_len, ATT_HEAD_DIM)
    gain_spec = pl.BlockSpec((1, 1, ATT_HEAD_DIM), lambda h, g: (g, 0, 0))
    return pl.pallas_call(
        functools.partial(_attn_kernel, s_len=s_len),
        grid=(ATT_HEADS, N_ATT_GROUPS),
        in_specs=[pl.BlockSpec(blk, lambda h, g: (0, qb + g * ATT_HEADS + h)),
                  pl.BlockSpec(blk, lambda h, g: (0, kb + g * ATT_HEADS + h)),
                  pl.BlockSpec(blk, lambda h, g: (0, vb + g * ATT_HEADS + h)),
                  pl.BlockSpec(blk, lambda h, g: (0, 0)),
                  gain_spec, gain_spec],
        out_specs=pl.BlockSpec(blk, lambda h, g: (0, h)),
        out_shape=jax.ShapeDtypeStruct((s_len, ATT_GROUP_W), BF16),
        scratch_shapes=[pltpu.VMEM(blk, F32)] * 5,
        compiler_params=_cparams(("parallel", "arbitrary")),
        name="dilated_attn",
    )(proj, proj, proj, table,
      q_gain.reshape(N_ATT_GROUPS, 1, ATT_HEAD_DIM), k_gain.reshape(N_ATT_GROUPS, 1, ATT_HEAD_DIM))


def _mlstm_kernel(q_ref, k_ref, v_ref, g_ref, b_ref, o_ref, c_ref, n_ref, m_ref):
    lc = ML_TILE
    dh = pl.program_id(0)
    step = pl.program_id(1)
    is_bw = dh >= ML_HEADS
    sgn = jnp.where(is_bw, -1, 1)

    @pl.when(step == 0)
    def _():
        c_ref[...] = jnp.zeros_like(c_ref)
        n_ref[...] = jnp.zeros_like(n_ref)
        m_ref[...] = jnp.zeros_like(m_ref)

    g = g_ref[0, 0] + b_ref[0]
    g = GATE_SOFTCAP * jnp.tanh(g / GATE_SOFTCAP)
    ig_row = g[0:1, :]
    f = g[1:2, :]
    lf_row = jnp.minimum(f, 0.0) - jnp.log1p(jnp.exp(-jnp.abs(f)))

    t_idx = lax.broadcasted_iota(jnp.int32, (lc, lc), 0)
    s_idx = lax.broadcasted_iota(jnp.int32, (lc, lc), 1)
    valid = (s_idx - t_idx) * sgn <= 0
    eye = s_idx == t_idx
    b_col = jnp.sum(jnp.where(valid, lf_row, 0.0), axis=1, keepdims=True)
    b_row = jnp.sum(jnp.where(eye, b_col, 0.0), axis=0, keepdims=True)
    ig_col = jnp.sum(jnp.where(eye, ig_row, 0.0), axis=1, keepdims=True)

    log_d = jnp.where(valid, b_col - b_row + ig_row, NEG)
    m_prev = m_ref[...]
    m_inter = b_col + m_prev
    m_t = jnp.maximum(m_inter, jnp.max(log_d, axis=1, keepdims=True))
    decay_q = jnp.exp(m_inter - m_t)
    d_mat = jnp.exp(log_d - m_t)

    q = q_ref[...]
    k = k_ref[...] * (ML_QK_DIM ** -0.5)
    v = v_ref[...]
    s = lax.dot_general(q, k, (((1,), (1,)), ((), ())), preferred_element_type=F32) * d_mat
    c_old = c_ref[...]
    num = jnp.dot(s.astype(BF16), v, preferred_element_type=F32)
    num = num + decay_q * jnp.dot(q, c_old.astype(BF16), preferred_element_type=F32)
    qn = jnp.sum(q.astype(F32) * n_ref[...], axis=1, keepdims=True)
    den = jnp.sum(s, axis=1, keepdims=True) + decay_q * qn
    o_ref[0] = num / jnp.maximum(jnp.abs(den), jnp.exp(-m_t))

    last = jnp.where(is_bw, 0, lc - 1)
    sel = lax.broadcasted_iota(jnp.int32, (lc, 1), 0) == last
    m_last = jnp.max(jnp.where(sel, m_t, -jnp.inf), axis=0, keepdims=True)
    b_last = jnp.sum(jnp.where(sel, b_col, 0.0), axis=0, keepdims=True)
    w_col = jnp.exp(b_last - b_col + ig_col - m_last)
    decay_c = jnp.exp(b_last + m_prev - m_last)
    kw = k.astype(F32) * w_col
    c_ref[...] = decay_c * c_old + lax.dot_general(
        kw.astype(BF16), v, (((0,), (0,)), ((), ())), preferred_element_type=F32)
    n_ref[...] = decay_c * n_ref[...] + jnp.sum(kw, axis=0, keepdims=True)
    m_ref[...] = m_last


def _mlstm(proj, gates, gate_bias):
    s_len = proj.shape[0]
    lc = ML_TILE
    nt = s_len // lc
    gt = gates.reshape(s_len, 2, 2, ML_HEADS).transpose(1, 3, 2, 0)
    gt = gt.reshape(2 * ML_HEADS, 2, nt, lc).transpose(0, 2, 1, 3)
    bias = gate_bias.reshape(2, 2, ML_HEADS).transpose(0, 2, 1).reshape(2 * ML_HEADS, 2, 1)

    def tile_of(dh, t):
        return jnp.where(dh >= ML_HEADS, nt - 1 - t, t)

    qb, kb = OFF_MQ // ML_QK_DIM, OFF_MK // ML_QK_DIM
    vb = OFF_MV // ML_V_DIM
    return pl.pallas_call(
        _mlstm_kernel,
        grid=(2 * ML_HEADS, nt),
        in_specs=[pl.BlockSpec((lc, ML_QK_DIM), lambda dh, t: (tile_of(dh, t), qb + dh % ML_HEADS)),
                  pl.BlockSpec((lc, ML_QK_DIM), lambda dh, t: (tile_of(dh, t), kb + dh % ML_HEADS)),
                  pl.BlockSpec((lc, ML_V_DIM), lambda dh, t: (tile_of(dh, t), vb + dh % ML_HEADS)),
                  pl.BlockSpec((1, 1, 2, lc), lambda dh, t: (dh, tile_of(dh, t), 0, 0)),
                  pl.BlockSpec((1, 2, 1), lambda dh, t: (dh, 0, 0))],
        out_specs=pl.BlockSpec((1, lc, ML_V_DIM),
                               lambda dh, t: (dh // ML_HEADS, tile_of(dh, t), dh % ML_HEADS)),
        out_shape=jax.ShapeDtypeStruct((2, s_len, ML_V_W), F32),
        scratch_shapes=[pltpu.VMEM((ML_QK_DIM, ML_V_DIM), F32),
                        pltpu.VMEM((1, ML_QK_DIM), F32),
                        pltpu.VMEM((1, 1), F32)],
        compiler_params=_cparams(("parallel", "arbitrary")),
        name="mlstm_scan",
    )(proj, proj, proj, gt, bias)


def _mlstm_out_kernel(h_ref, mo_ref, g_ref, o_ref):
    h = h_ref[0] + h_ref[1]
    y = h * lax.rsqrt(jnp.mean(h * h, axis=-1, keepdims=True) + EPS) * g_ref[0]
    o_ref[...] = (y * jax.nn.sigmoid(mo_ref[...].astype(F32))).astype(o_ref.dtype)


def _mlstm_out(hs, proj, head_gain, tm=512):
    s_len = proj.shape[0]
    ob = OFF_MO // ML_V_DIM
    return pl.pallas_call(
        _mlstm_out_kernel,
        grid=(s_len // tm, ML_HEADS),
        in_specs=[pl.BlockSpec((2, tm, ML_V_DIM), lambda i, h: (0, i, h)),
                  pl.BlockSpec((tm, ML_V_DIM), lambda i, h: (i, ob + h)),
                  pl.BlockSpec((1, 1, ML_V_DIM), lambda i, h: (h, 0, 0))],
        out_specs=pl.BlockSpec((tm, ML_V_DIM), lambda i, h: (i, h)),
        out_shape=jax.ShapeDtypeStruct((s_len, ML_V_W), BF16),
        compiler_params=_cparams(("parallel", "parallel")),
        name="mlstm_out",
    )(hs, proj, head_gain.reshape(ML_HEADS, 1, ML_V_DIM))


def _branch_kernel(att_ref, mem_ref, wa_ref, wm_ref, ga_ref, gm_ref, ba_ref, bm_ref, o_ref):
    ya = jnp.dot(att_ref[...], wa_ref[...], preferred_element_type=F32)
    ym = jnp.dot(mem_ref[...], wm_ref[...], preferred_element_type=F32)
    g_a = jax.nn.sigmoid(ga_ref[...].astype(F32) + ba_ref[...])
    g_m = jax.nn.sigmoid(gm_ref[...].astype(F32) + bm_ref[...])
    o_ref[...] = (g_a * ya + g_m * ym).astype(o_ref.dtype)


def _branch_mix(att, mem, w_a, w_m, proj, gate_bias, tm=512, tn=512):
    m, d = att.shape[0], w_a.shape[1]
    gab = OFF_GA // tn
    gmb = (OFF_GA + d) // tn
    return pl.pallas_call(
        _branch_kernel,
        grid=(m // tm, d // tn),
        in_specs=[pl.BlockSpec((tm, att.shape[1]), lambda i, j: (i, 0)),
                  pl.BlockSpec((tm, mem.shape[1]), lambda i, j: (i, 0)),
                  pl.BlockSpec((w_a.shape[0], tn), lambda i, j: (0, j)),
                  pl.BlockSpec((w_m.shape[0], tn), lambda i, j: (0, j)),
                  pl.BlockSpec((tm, tn), lambda i, j: (i, gab + j)),
                  pl.BlockSpec((tm, tn), lambda i, j: (i, gmb + j)),
                  pl.BlockSpec((1, tn), lambda i, j: (0, j)),
                  pl.BlockSpec((1, tn), lambda i, j: (0, j))],
        out_specs=pl.BlockSpec((tm, tn), lambda i, j: (i, j)),
        out_shape=jax.ShapeDtypeStruct((m, d), BF16),
        compiler_params=_cparams(("parallel", "arbitrary")),
        name="branch_mix",
    )(att, mem, w_a, w_m, proj, proj, gate_bias[0:1], gate_bias[1:2])


def _norm_router_kernel(x_ref, g_ref, wr_ref, br_ref, h_ref, lg_ref):
    x = x_ref[...]
    h = x * lax.rsqrt(jnp.mean(x * x, axis=-1, keepdims=True) + EPS) * g_ref[...]
    h_ref[...] = h
    lg_ref[...] = jnp.dot(h, wr_ref[...], preferred_element_type=F32,
                          precision=lax.Precision.HIGHEST) + br_ref[...]


def _norm_router(x, gain, w_r, b_r, tm=256):
    m, d = x.shape
    n = w_r.shape[1]
    return pl.pallas_call(
        _norm_router_kernel,
        grid=(m // tm,),
        in_specs=[pl.BlockSpec((tm, d), lambda i: (i, 0)),
                  pl.BlockSpec((1, d), lambda i: (0, 0)),
                  pl.BlockSpec((d, n), lambda i: (0, 0)),
                  pl.BlockSpec((1, n), lambda i: (0, 0))],
        out_specs=[pl.BlockSpec((tm, d), lambda i: (i, 0)),
                   pl.BlockSpec((tm, n), lambda i: (i, 0))],
        out_shape=[jax.ShapeDtypeStruct((m, d), F32),
                   jax.ShapeDtypeStruct((m, n), F32)],
        compiler_params=_cparams(("parallel",)),
        name="norm_router",
    )(x, gain.reshape(1, d), w_r, b_r)


def _moe_kernel(blk_e_ref, nused_ref, tok_ref, h_hbm, sw_ref, wg_ref, wu_ref, wd_ref, y_ref,
                xg_ref, xb_ref, act_ref, sem, *, n_ff_tiles):
    b = pl.program_id(0)
    j = pl.program_id(1)
    rows = MOE_ROWS
    tf = MOE_FF_TILE
    used = b < nused_ref[0]
    up_phase = j < n_ff_tiles

    def row_copy(i, tok):
        return pltpu.make_async_copy(h_hbm.at[pl.ds(tok, 1)], xg_ref.at[pl.ds(i, 1)], sem)

    @pl.when(jnp.logical_and(used, j == 0))
    def _():
        def issue(i, carry):
            row_copy(i, tok_ref[0, 0, i]).start()
            return carry

        def drain(i, carry):
            row_copy(i, 0).wait()
            return carry

        lax.fori_loop(0, rows, issue, 0, unroll=DMA_ISSUE_UNROLL)
        lax.fori_loop(0, rows, drain, 0, unroll=DMA_ISSUE_UNROLL)
        xb_ref[...] = xg_ref[...].astype(BF16)

    @pl.when(jnp.logical_and(used, up_phase))
    def _():
        x = xb_ref[...]
        g = jnp.dot(x, wg_ref[0].astype(BF16), preferred_element_type=F32)
        u = jnp.dot(x, wu_ref[0].astype(BF16), preferred_element_type=F32)
        act_ref[j] = (g * jax.nn.sigmoid(g) * u).astype(BF16)

    @pl.when(jnp.logical_and(used, jnp.logical_not(up_phase)))
    def _():
        acc = jnp.dot(act_ref[0], wd_ref[0, 0:tf, :].astype(BF16), preferred_element_type=F32)
        for t in range(1, n_ff_tiles):
            acc = acc + jnp.dot(act_ref[t], wd_ref[0, t * tf:(t + 1) * tf, :].astype(BF16),
                                preferred_element_type=F32)
        y_ref[...] = acc * sw_ref[...]

    @pl.when(jnp.logical_and(jnp.logical_not(used), jnp.logical_not(up_phase)))
    def _():
        y_ref[...] = jnp.zeros_like(y_ref)


def _moe_experts(h2, blk_expert, n_used, slot_tok, slot_w, w_gate, w_up, w_down):
    n_tok, d = h2.shape
    d_ff = w_gate.shape[2]
    rows = MOE_ROWS
    n_blk = slot_tok.shape[0] // rows
    tf = MOE_FF_TILE
    tn = MOE_OUT_TILE
    n_f = d_ff // tf
    n_o = d // tn

    def ff_tile(b, j, nu):
        return jnp.where(b < nu[0], jnp.minimum(j, n_f - 1), n_f - 1)

    def out_tile(b, j, nu):
        return jnp.clip(j - n_f, 0, n_o - 1)

    def w_out_tile(b, j, nu):
        return jnp.where(b < nu[0], out_tile(b, j, nu), n_o - 1)

    grid_spec = pltpu.PrefetchScalarGridSpec(
        num_scalar_prefetch=2,
        grid=(n_blk, n_f + n_o),
        in_specs=[pl.BlockSpec((1, 1, rows), lambda b, j, be, nu: (b, 0, 0), memory_space=pltpu.SMEM),
                  pl.BlockSpec(memory_space=pl.ANY),
                  pl.BlockSpec((rows, 1), lambda b, j, be, nu: (b, 0)),
                  pl.BlockSpec((1, d, tf), lambda b, j, be, nu: (be[b], 0, ff_tile(b, j, nu))),
                  pl.BlockSpec((1, d, tf), lambda b, j, be, nu: (be[b], 0, ff_tile(b, j, nu))),
                  pl.BlockSpec((1, d_ff, tn), lambda b, j, be, nu: (be[b], 0, w_out_tile(b, j, nu)))],
        out_specs=pl.BlockSpec((rows, tn), lambda b, j, be, nu: (b, out_tile(b, j, nu))),
        scratch_shapes=[pltpu.VMEM((rows, d), F32),
                        pltpu.VMEM((rows, d), BF16),
                        pltpu.VMEM((n_f, rows, tf), BF16),
                        pltpu.SemaphoreType.DMA(())],
    )
    return pl.pallas_call(
        functools.partial(_moe_kernel, n_ff_tiles=n_f),
        grid_spec=grid_spec,
        out_shape=jax.ShapeDtypeStruct((n_blk * rows, d), F32),
        compiler_params=_cparams(("arbitrary", "arbitrary")),
        name="moe_experts",
    )(blk_expert, n_used, slot_tok.reshape(n_blk, 1, rows), h2, slot_w.reshape(-1, 1),
      w_gate, w_up, w_down)


def _combine_kernel(pos_ref, x_ref, y_hbm, o_ref, buf_ref, sem):
    rows = COMBINE_ROWS

    def row_copy(i, src):
        return pltpu.make_async_copy(y_hbm.at[pl.ds(src, 1)], buf_ref.at[pl.ds(i, 1)], sem)

    def issue(i, carry):
        row_copy(i, pos_ref[0, 0, i]).start()
        return carry

    def drain(i, carry):
        row_copy(i, 0).wait()
        return carry

    lax.fori_loop(0, TOP_K * rows, issue, 0, unroll=DMA_ISSUE_UNROLL)
    lax.fori_loop(0, TOP_K * rows, drain, 0, unroll=DMA_ISSUE_UNROLL)
    o_ref[...] = x_ref[...] + (buf_ref[pl.ds(0, rows), :] + buf_ref[pl.ds(rows, rows), :])


def _moe_combine(x1, ys, pos):
    n_tok, d = x1.shape
    rows = COMBINE_ROWS
    n_blk = n_tok // rows
    pos_b = pos.reshape(n_blk, rows, TOP_K).transpose(0, 2, 1).reshape(n_blk, 1, TOP_K * rows)
    return pl.pallas_call(
        _combine_kernel,
        grid=(n_blk,),
        in_specs=[pl.BlockSpec((1, 1, TOP_K * rows), lambda i: (i, 0, 0), memory_space=pltpu.SMEM),
                  pl.BlockSpec((rows, d), lambda i: (i, 0)),
                  pl.BlockSpec(memory_space=pl.ANY)],
        out_specs=pl.BlockSpec((rows, d), lambda i: (i, 0)),
        out_shape=jax.ShapeDtypeStruct((n_tok, d), F32),
        scratch_shapes=[pltpu.VMEM((TOP_K * rows, d), F32), pltpu.SemaphoreType.DMA(())],
        compiler_params=_cparams(("arbitrary",)),
        name="moe_combine",
    )(pos_b, x1, ys)


def _route(logits, n_tok):
    glog = logits[:, :N_GROUPS]
    gprob = jax.nn.softmax(glog, axis=-1)
    g_sel = jnp.argmax(glog, axis=-1)
    p_g = jnp.take_along_axis(gprob, g_sel[:, None], axis=1)[:, 0]
    elog = logits[:, N_GROUPS:N_GROUPS + N_EXPERTS].reshape(n_tok, N_GROUPS, EXPERTS_PER_GROUP)
    elog = jnp.take_along_axis(elog, g_sel[:, None, None], axis=1)[:, 0]
    top_v, top_i = lax.top_k(elog, TOP_K)
    weights = p_g[:, None] * jax.nn.softmax(top_v, axis=-1)
    expert_ids = (g_sel[:, None] * EXPERTS_PER_GROUP + top_i).astype(jnp.int32)

    n_assign = n_tok * TOP_K
    rows = MOE_ROWS
    flat_e = expert_ids.reshape(n_assign)
    flat_tok = jnp.repeat(jnp.arange(n_tok, dtype=jnp.int32), TOP_K)
    flat_w = weights.reshape(n_assign)
    order = jnp.argsort(flat_e)
    se = flat_e[order]
    counts = jnp.zeros((N_EXPERTS,), jnp.int32).at[flat_e].add(1)
    blocks = (counts + rows - 1) // rows
    pad_end = jnp.cumsum(blocks * rows)
    pad_start = pad_end - blocks * rows
    start = jnp.cumsum(counts) - counts
    dest = pad_start[se] + jnp.arange(n_assign, dtype=jnp.int32) - start[se]
    n_slots = n_assign + N_EXPERTS * rows
    n_blk = n_slots // rows
    slot_tok = jnp.zeros((n_slots,), jnp.int32).at[dest].set(flat_tok[order])
    slot_w = jnp.zeros((n_slots,), F32).at[dest].set(flat_w[order])
    n_used = jnp.sum(blocks).astype(jnp.int32)
    blk_ids = jnp.arange(n_blk, dtype=jnp.int32)
    blk_expert = jnp.minimum(jnp.searchsorted(pad_end, blk_ids * rows, side='right'),
                             N_EXPERTS - 1).astype(jnp.int32)
    blk_expert = jnp.where(blk_ids < n_used, blk_expert, blk_expert[jnp.maximum(n_used - 1, 0)])
    pos = jnp.zeros((n_assign,), jnp.int32).at[order].set(dest).reshape(n_tok, TOP_K)
    return blk_expert, n_used.reshape(1), slot_tok, slot_w, pos


def kernel(x, norm1_gain, w_in, attn_q_norm_gain, attn_k_norm_gain, mlstm_gate_bias,
           mlstm_head_norm_gain, branch_gate_bias, w_attn_branch, w_mlstm_branch, w_out,
           norm2_gain, w_router_group, b_router_group, w_router_expert, b_router_expert,
           w_expert_gate, w_expert_up, w_expert_down):
    bsz, s_len, d = x.shape
    assert bsz == 1
    depth = w_in.shape[0]
    xs = x.reshape(s_len, d)
    mg_end = OFF_MG + N_ML_GATES
    for l in range(depth):
        w_l = w_in[l]
        w_main = jnp.concatenate([w_l[:, :OFF_MG], w_l[:, mg_end:]], axis=1).astype(BF16)
        w_mg = jnp.pad(w_l[:, OFF_MG:mg_end], ((0, 0), (0, LANES - N_ML_GATES))).astype(BF16)

        h = _rmsnorm(xs, norm1_gain[l])
        proj = _matmul(h, w_main, BF16, 1024, 512, "in_proj")
        gates = _matmul(h, w_mg, F32, 1024, LANES, "gate_proj")[:, :N_ML_GATES]

        att = _attention(proj, attn_q_norm_gain[l], attn_k_norm_gain[l])

        hs = _mlstm(proj, gates, mlstm_gate_bias[l])
        mem = _mlstm_out(hs, proj, mlstm_head_norm_gain[l])

        merged = _branch_mix(att, mem, w_attn_branch[l].astype(BF16), w_mlstm_branch[l].astype(BF16),
                             proj, branch_gate_bias[l])
        x1 = _matmul_residual(merged, w_out[l].astype(BF16), xs, 1024, 512)

        w_r = jnp.concatenate([w_router_group[l], w_router_expert[l]], axis=1)
        b_r = jnp.concatenate([b_router_group[l], b_router_expert[l]])
        n_r = w_r.shape[1]
        w_r = jnp.pad(w_r, ((0, 0), (0, LANES - n_r)))
        b_r = jnp.pad(b_r, (0, LANES - n_r)).reshape(1, LANES)
        h2, logits = _norm_router(x1, norm2_gain[l], w_r, b_r)

        blk_expert, n_used, slot_tok, slot_w, pos = _route(logits, s_len)
        ys = _moe_experts(h2, blk_expert, n_used, slot_tok, slot_w,
                          w_expert_gate[l], w_expert_up[l], w_expert_down[l])
        xs = _moe_combine(x1, ys, pos)
    return xs.reshape(bsz, s_len, d)
```

```python
import functools

import jax
import jax.numpy as jnp
from jax import lax
from jax.experimental import pallas as pl
from jax.experimental.pallas import tpu as pltpu

F32 = jnp.float32
BF16 = jnp.bfloat16

EPS = 1e-6
NEG = -1e30
LANES = 128

ATT_PATTERNS = ((128, 1), (512, 4), (2048, 16))
N_ATT_GROUPS = 3
ATT_HEADS = 8
ATT_HEAD_DIM = 128
ATT_GROUP_W = ATT_HEADS * ATT_HEAD_DIM
ATT_W = N_ATT_GROUPS * ATT_GROUP_W
ROPE_THETA = 10000.0
ATT_Q_TILE = 128
ATT_TILE_UNROLL = 4

ML_HEADS = 8
ML_QK_DIM = 256
ML_V_DIM = 512
ML_QK_W = ML_HEADS * ML_QK_DIM
ML_V_W = ML_HEADS * ML_V_DIM
N_ML_GATES = 4 * ML_HEADS
GATE_SOFTCAP = 15.0
ML_TILE = 256

N_GROUPS = 8
EXPERTS_PER_GROUP = 8
N_EXPERTS = N_GROUPS * EXPERTS_PER_GROUP
TOP_K = 2
MOE_ROWS = 512
MOE_SUB_ROWS = 128
MOE_FF_TILE = 256
MOE_OUT_TILE = 1024
COMBINE_ROWS = 256
DMA_ISSUE_UNROLL = 8

OFF_AQ = 0
OFF_AK = OFF_AQ + ATT_W
OFF_AV = OFF_AK + ATT_W
OFF_MQ = OFF_AV + ATT_W
OFF_MK = OFF_MQ + ML_QK_W
OFF_MV = OFF_MK + ML_QK_W
OFF_MO = OFF_MV + ML_V_W
OFF_MG = OFF_MO + ML_V_W
OFF_GA = OFF_MG

VMEM_LIMIT = 56 * 1024 * 1024


def _cparams(sem, vmem=VMEM_LIMIT):
    return pltpu.CompilerParams(dimension_semantics=sem, vmem_limit_bytes=vmem)


def _rmsnorm_kernel(x_ref, g_ref, o_ref):
    x = x_ref[...]
    ms = jnp.mean(x * x, axis=-1, keepdims=True)
    o_ref[...] = (x * lax.rsqrt(ms + EPS) * g_ref[...]).astype(o_ref.dtype)


def _rmsnorm(x, gain, tm=256):
    m, d = x.shape
    return pl.pallas_call(
        _rmsnorm_kernel,
        grid=(m // tm,),
        in_specs=[pl.BlockSpec((tm, d), lambda i: (i, 0)),
                  pl.BlockSpec((1, d), lambda i: (0, 0))],
        out_specs=pl.BlockSpec((tm, d), lambda i: (i, 0)),
        out_shape=jax.ShapeDtypeStruct((m, d), BF16),
        compiler_params=_cparams(("parallel",)),
        name="rmsnorm",
    )(x, gain.reshape(1, d))


def _mm_kernel(a_ref, w_ref, o_ref, wb_ref):
    @pl.when(pl.program_id(1) == 0)
    def _():
        wb_ref[...] = w_ref[0].astype(BF16)

    o_ref[...] = jnp.dot(a_ref[...], wb_ref[...], preferred_element_type=F32).astype(o_ref.dtype)


def _matmul(a, w, layer, col0, n, out_dtype, tm, tn, name):
    m, k = a.shape
    cb = col0 // tn
    return pl.pallas_call(
        _mm_kernel,
        grid=(n // tn, m // tm),
        in_specs=[pl.BlockSpec((tm, k), lambda j, i: (i, 0)),
                  pl.BlockSpec((1, k, tn), lambda j, i: (layer, 0, cb + j))],
        out_specs=pl.BlockSpec((tm, tn), lambda j, i: (i, j)),
        out_shape=jax.ShapeDtypeStruct((m, n), out_dtype),
        scratch_shapes=[pltpu.VMEM((k, tn), BF16)],
        compiler_params=_cparams(("parallel", "arbitrary")),
        name=name,
    )(a, w)


def _mm_res_kernel(a_ref, w_ref, r_ref, o_ref, wb_ref):
    @pl.when(pl.program_id(1) == 0)
    def _():
        wb_ref[...] = w_ref[0].astype(BF16)

    o_ref[...] = r_ref[...] + jnp.dot(a_ref[...], wb_ref[...], preferred_element_type=F32)


def _matmul_residual(a, w, layer, res, tm, tn):
    m, k = a.shape
    n = w.shape[2]
    return pl.pallas_call(
        _mm_res_kernel,
        grid=(n // tn, m // tm),
        in_specs=[pl.BlockSpec((tm, k), lambda j, i: (i, 0)),
                  pl.BlockSpec((1, k, tn), lambda j, i: (layer, 0, j)),
                  pl.BlockSpec((tm, tn), lambda j, i: (i, j))],
        out_specs=pl.BlockSpec((tm, tn), lambda j, i: (i, j)),
        out_shape=jax.ShapeDtypeStruct((m, n), F32),
        scratch_shapes=[pltpu.VMEM((k, tn), BF16)],
        compiler_params=_cparams(("parallel", "arbitrary")),
        name="out_proj",
    )(a, w, res)


def _attn_kernel(q_ref, k_ref, v_ref, tab_ref, qg_ref, kg_ref, o_ref,
                 qn_ref, kn_ref, vf_ref, acc_ref, lse_ref, *, s_len):
    grp = pl.program_id(1)
    tq = ATT_Q_TILE
    half = ATT_HEAD_DIM // 2
    scale = ATT_HEAD_DIM ** -0.5
    low = lax.broadcasted_iota(jnp.int32, (tq, ATT_HEAD_DIM), 1) < half

    def prep(i, carry):
        rows = pl.ds(pl.multiple_of(i * tq, tq), tq)
        tab = tab_ref[rows, :]
        swapped = pltpu.roll(tab, half, axis=1)
        c = jnp.where(low, tab, swapped)
        s = jnp.where(low, -swapped, tab)
        for src, g_ref, dst, mul in ((q_ref, qg_ref, qn_ref, scale), (k_ref, kg_ref, kn_ref, 1.0)):
            xf = src[rows, :].astype(F32)
            y = xf * lax.rsqrt(jnp.mean(xf * xf, axis=-1, keepdims=True) + EPS) * g_ref[0]
            dst[rows, :] = (y * c + pltpu.roll(y, half, axis=1) * s) * mul
        vf_ref[rows, :] = v_ref[rows, :].astype(F32)
        return carry

    lax.fori_loop(0, s_len // tq, prep, 0, unroll=2)

    def run_group(window, dilation, first):
        radius = window // (2 * dilation)
        sub_len = s_len // dilation
        win = tq + 2 * radius
        tiles_per_class = sub_len // tq
        rel = (lax.broadcasted_iota(jnp.int32, (tq, win), 1)
               - lax.broadcasted_iota(jnp.int32, (tq, win), 0))

        def rows_of(start, size):
            if dilation == 1:
                return pl.ds(start, size)
            return pl.ds(start, size, stride=dilation)

        def tile(idx, carry):
            r = idx // tiles_per_class
            q0 = (idx % tiles_per_class) * tq
            k0 = jnp.clip(q0 - radius, 0, sub_len - win)
            q_rows = rows_of(r + dilation * q0, tq)
            k_rows = rows_of(r + dilation * k0, win)
            q = qn_ref[q_rows, :].astype(BF16)
            k = kn_ref[k_rows, :].astype(BF16)
            v = vf_ref[k_rows, :].astype(BF16)
            s = lax.dot_general(q, k, (((1,), (1,)), ((), ())), preferred_element_type=F32)
            s = jnp.where(jnp.abs(rel + (k0 - q0)) <= radius, s, NEG)
            m = jnp.max(s, axis=-1, keepdims=True)
            p = jnp.exp(s - m)
            den = jnp.sum(p, axis=-1, keepdims=True)
            o = jnp.dot(p.astype(BF16), v, preferred_element_type=F32) / den
            lse = jnp.broadcast_to(m + jnp.log(den), (tq, ATT_HEAD_DIM))
            if first:
                acc_ref[q_rows, :] = o
                lse_ref[q_rows, :] = lse
            else:
                old_l = lse_ref[q_rows, :]
                top = jnp.maximum(old_l, lse)
                a = jnp.exp(old_l - top)
                b = jnp.exp(lse - top)
                tot = a + b
                acc_ref[q_rows, :] = (a * acc_ref[q_rows, :] + b * o) / tot
                lse_ref[q_rows, :] = top + jnp.log(tot)
            return carry

        lax.fori_loop(0, s_len // tq, tile, 0, unroll=ATT_TILE_UNROLL)

    for gi, (window, dilation) in enumerate(ATT_PATTERNS):
        @pl.when(grp == gi)
        def _(window=window, dilation=dilation, gi=gi):
            run_group(window, dilation, gi == 0)

    @pl.when(grp == N_ATT_GROUPS - 1)
    def _():
        o_ref[...] = acc_ref[...].astype(o_ref.dtype)


def _attention(proj, q_gain, k_gain):
    s_len = proj.shape[0]
    half = ATT_HEAD_DIM // 2
    inv_freq = ROPE_THETA ** (-jnp.arange(half, dtype=F32) / half)
    ang = jnp.arange(s_len).astype(F32)[:, None] * inv_freq[None, :]
    table = jnp.concatenate([jnp.cos(ang), jnp.sin(ang)], axis=1)
    qb, kb, vb = OFF_AQ // LANES, OFF_AK // LANES, OFF_AV // LANES
    blk = (s_len, ATT_HEAD_DIM)
    gain_spec = pl.BlockSpec((1, 1, ATT_HEAD_DIM), lambda h, g: (g, 0, 0))
    return pl.pallas_call(
        functools.partial(_attn_kernel, s_len=s_len),
        grid=(ATT_HEADS, N_ATT_GROUPS),
        in_specs=[pl.BlockSpec(blk, lambda h, g: (0, qb + g * ATT_HEADS + h)),
                  pl.BlockSpec(blk, lambda h, g: (0, kb + g * ATT_HEADS + h)),
                  pl.BlockSpec(blk, lambda h, g: (0, vb + g * ATT_HEADS + h)),
                  pl.BlockSpec(blk, lambda h, g: (0, 0)),
                  gain_spec, gain_spec],
        out_specs=pl.BlockSpec(blk, lambda h, g: (0, h)),
        out_shape=jax.ShapeDtypeStruct((s_len, ATT_GROUP_W), BF16),
        scratch_shapes=[pltpu.VMEM(blk, F32)] * 5,
        compiler_params=_cparams(("parallel", "arbitrary")),
        name="dilated_attn",
    )(proj, proj, proj, table,
      q_gain.reshape(N_ATT_GROUPS, 1, ATT_HEAD_DIM), k_gain.reshape(N_ATT_GROUPS, 1, ATT_HEAD_DIM))


def _mlstm_kernel(q_ref, k_ref, v_ref, g_ref, b_ref, o_ref, c_ref, n_ref, m_ref):
    lc = ML_TILE
    dh = pl.program_id(0)
    step = pl.program_id(1)
    is_bw = dh >= ML_HEADS
    sgn = jnp.where(is_bw, -1, 1)

    @pl.when(step == 0)
    def _():
        c_ref[...] = jnp.zeros_like(c_ref)
        n_ref[...] = jnp.zeros_like(n_ref)
        m_ref[...] = jnp.zeros_like(m_ref)

    g = g_ref[0, 0] + b_ref[0]
    g = GATE_SOFTCAP * jnp.tanh(g / GATE_SOFTCAP)
    ig_row = g[0:1, :]
    f = g[1:2, :]
    lf_row = jnp.minimum(f, 0.0) - jnp.log1p(jnp.exp(-jnp.abs(f)))

    t_idx = lax.broadcasted_iota(jnp.int32, (lc, lc), 0)
    s_idx = lax.broadcasted_iota(jnp.int32, (lc, lc), 1)
    valid = (s_idx - t_idx) * sgn <= 0
    eye = s_idx == t_idx
    b_col = jnp.sum(jnp.where(valid, lf_row, 0.0), axis=1, keepdims=True)
    b_row = jnp.sum(jnp.where(eye, b_col, 0.0), axis=0, keepdims=True)
    ig_col = jnp.sum(jnp.where(eye, ig_row, 0.0), axis=1, keepdims=True)

    log_d = jnp.where(valid, b_col - b_row + ig_row, NEG)
    m_prev = m_ref[...]
    m_inter = b_col + m_prev
    m_t = jnp.maximum(m_inter, jnp.max(log_d, axis=1, keepdims=True))
    decay_q = jnp.exp(m_inter - m_t)
    d_mat = jnp.exp(log_d - m_t)

    q = q_ref[...]
    k = k_ref[...] * (ML_QK_DIM ** -0.5)
    v = v_ref[...]
    s = lax.dot_general(q, k, (((1,), (1,)), ((), ())), preferred_element_type=F32) * d_mat
    c_old = c_ref[...]
    num = jnp.dot(s.astype(BF16), v, preferred_element_type=F32)
    num = num + decay_q * jnp.dot(q, c_old.astype(BF16), preferred_element_type=F32)
    qn = jnp.sum(q.astype(F32) * n_ref[...], axis=1, keepdims=True)
    den = jnp.sum(s, axis=1, keepdims=True) + decay_q * qn
    o_ref[0] = num / jnp.maximum(jnp.abs(den), jnp.exp(-m_t))

    last = jnp.where(is_bw, 0, lc - 1)
    sel = lax.broadcasted_iota(jnp.int32, (lc, 1), 0) == last
    m_last = jnp.max(jnp.where(sel, m_t, -jnp.inf), axis=0, keepdims=True)
    b_last = jnp.sum(jnp.where(sel, b_col, 0.0), axis=0, keepdims=True)
    w_col = jnp.exp(b_last - b_col + ig_col - m_last)
    decay_c = jnp.exp(b_last + m_prev - m_last)
    kw = k.astype(F32) * w_col
    c_ref[...] = decay_c * c_old + lax.dot_general(
        kw.astype(BF16), v, (((0,), (0,)), ((), ())), preferred_element_type=F32)
    n_ref[...] = decay_c * n_ref[...] + jnp.sum(kw, axis=0, keepdims=True)
    m_ref[...] = m_last


def _mlstm(proj, gates, gate_bias):
    s_len = proj.shape[0]
    lc = ML_TILE
    nt = s_len // lc
    gt = gates.reshape(s_len, 2, 2, ML_HEADS).transpose(1, 3, 2, 0)
    gt = gt.reshape(2 * ML_HEADS, 2, nt, lc).transpose(0, 2, 1, 3)
    bias = gate_bias.reshape(2, 2, ML_HEADS).transpose(0, 2, 1).reshape(2 * ML_HEADS, 2, 1)

    def tile_of(dh, t):
        return jnp.where(dh >= ML_HEADS, nt - 1 - t, t)

    qb, kb = OFF_MQ // ML_QK_DIM, OFF_MK // ML_QK_DIM
    vb = OFF_MV // ML_V_DIM
    return pl.pallas_call(
        _mlstm_kernel,
        grid=(2 * ML_HEADS, nt),
        in_specs=[pl.BlockSpec((lc, ML_QK_DIM), lambda dh, t: (tile_of(dh, t), qb + dh % ML_HEADS)),
                  pl.BlockSpec((lc, ML_QK_DIM), lambda dh, t: (tile_of(dh, t), kb + dh % ML_HEADS)),
                  pl.BlockSpec((lc, ML_V_DIM), lambda dh, t: (tile_of(dh, t), vb + dh % ML_HEADS)),
                  pl.BlockSpec((1, 1, 2, lc), lambda dh, t: (dh, tile_of(dh, t), 0, 0)),
                  pl.BlockSpec((1, 2, 1), lambda dh, t: (dh, 0, 0))],
        out_specs=pl.BlockSpec((1, lc, ML_V_DIM),
                               lambda dh, t: (dh // ML_HEADS, tile_of(dh, t), dh % ML_HEADS)),
        out_shape=jax.ShapeDtypeStruct((2, s_len, ML_V_W), F32),
        scratch_shapes=[pltpu.VMEM((ML_QK_DIM, ML_V_DIM), F32),
                        pltpu.VMEM((1, ML_QK_DIM), F32),
                        pltpu.VMEM((1, 1), F32)],
        compiler_params=_cparams(("parallel", "arbitrary")),
        name="mlstm_scan",
    )(proj, proj, proj, gt, bias)


def _mlstm_out_kernel(h_ref, mo_ref, g_ref, o_ref):
    h = h_ref[0] + h_ref[1]
    y = h * lax.rsqrt(jnp.mean(h * h, axis=-1, keepdims=True) + EPS) * g_ref[0]
    o_ref[...] = (y * jax.nn.sigmoid(mo_ref[...].astype(F32))).astype(o_ref.dtype)


def _mlstm_out(hs, proj, head_gain, tm=512):
    s_len = proj.shape[0]
    ob = OFF_MO // ML_V_DIM
    return pl.pallas_call(
        _mlstm_out_kernel,
        grid=(s_len // tm, ML_HEADS),
        in_specs=[pl.BlockSpec((2, tm, ML_V_DIM), lambda i, h: (0, i, h)),
                  pl.BlockSpec((tm, ML_V_DIM), lambda i, h: (i, ob + h)),
                  pl.BlockSpec((1, 1, ML_V_DIM), lambda i, h: (h, 0, 0))],
        out_specs=pl.BlockSpec((tm, ML_V_DIM), lambda i, h: (i, h)),
        out_shape=jax.ShapeDtypeStruct((s_len, ML_V_W), BF16),
        compiler_params=_cparams(("parallel", "parallel")),
        name="mlstm_out",
    )(hs, proj, head_gain.reshape(ML_HEADS, 1, ML_V_DIM))


def _branch_kernel(att_ref, mem_ref, wa_ref, wm_ref, ga_ref, gm_ref, ba_ref, bm_ref, o_ref,
                   wab_ref, wmb_ref):
    @pl.when(pl.program_id(1) == 0)
    def _():
        wab_ref[...] = wa_ref[0].astype(BF16)
        wmb_ref[...] = wm_ref[0].astype(BF16)

    ya = jnp.dot(att_ref[...], wab_ref[...], preferred_element_type=F32)
    ym = jnp.dot(mem_ref[...], wmb_ref[...], preferred_element_type=F32)
    g_a = jax.nn.sigmoid(ga_ref[...].astype(F32) + ba_ref[0])
    g_m = jax.nn.sigmoid(gm_ref[...].astype(F32) + bm_ref[0])
    o_ref[...] = (g_a * ya + g_m * ym).astype(o_ref.dtype)


def _branch_mix(att, mem, w_a, w_m, layer, gates, gate_bias, tm=512, tn=512):
    m = att.shape[0]
    ka, km, d = w_a.shape[1], w_m.shape[1], w_a.shape[2]
    bias = gate_bias.reshape(-1, 1, d)
    return pl.pallas_call(
        _branch_kernel,
        grid=(d // tn, m // tm),
        in_specs=[pl.BlockSpec((tm, ka), lambda j, i: (i, 0)),
                  pl.BlockSpec((tm, km), lambda j, i: (i, 0)),
                  pl.BlockSpec((1, ka, tn), lambda j, i: (layer, 0, j)),
                  pl.BlockSpec((1, km, tn), lambda j, i: (layer, 0, j)),
                  pl.BlockSpec((tm, tn), lambda j, i: (i, j)),
                  pl.BlockSpec((tm, tn), lambda j, i: (i, d // tn + j)),
                  pl.BlockSpec((1, 1, tn), lambda j, i: (2 * layer, 0, j)),
                  pl.BlockSpec((1, 1, tn), lambda j, i: (2 * layer + 1, 0, j))],
        out_specs=pl.BlockSpec((tm, tn), lambda j, i: (i, j)),
        out_shape=jax.ShapeDtypeStruct((m, d), BF16),
        scratch_shapes=[pltpu.VMEM((ka, tn), BF16), pltpu.VMEM((km, tn), BF16)],
        compiler_params=_cparams(("parallel", "arbitrary")),
        name="branch_mix",
    )(att, mem, w_a, w_m, gates, gates, bias, bias)


def _norm_router_kernel(x_ref, g_ref, wr_ref, br_ref, h_ref, lg_ref):
    x = x_ref[...]
    h = x * lax.rsqrt(jnp.mean(x * x, axis=-1, keepdims=True) + EPS) * g_ref[...]
    h_ref[...] = h
    lg_ref[...] = jnp.dot(h, wr_ref[...], preferred_element_type=F32,
                          precision=lax.Precision.HIGHEST) + br_ref[...]


def _norm_router(x, gain, w_r, b_r, tm=256):
    m, d = x.shape
    n = w_r.shape[1]
    return pl.pallas_call(
        _norm_router_kernel,
        grid=(m // tm,),
        in_specs=[pl.BlockSpec((tm, d), lambda i: (i, 0)),
                  pl.BlockSpec((1, d), lambda i: (0, 0)),
                  pl.BlockSpec((d, n), lambda i: (0, 0)),
                  pl.BlockSpec((1, n), lambda i: (0, 0))],
        out_specs=[pl.BlockSpec((tm, d), lambda i: (i, 0)),
                   pl.BlockSpec((tm, n), lambda i: (i, 0))],
        out_shape=[jax.ShapeDtypeStruct((m, d), F32),
                   jax.ShapeDtypeStruct((m, n), F32)],
        compiler_params=_cparams(("parallel",)),
        name="norm_router",
    )(x, gain.reshape(1, d), w_r, b_r)


def _moe_kernel(blk_e_ref, cnt_ref, tok_ref, tok_next_ref, h_hbm, sw_ref, wg_ref, wu_ref, wd_ref,
                y_ref, xg_ref, xb_ref, act_ref, sem, *, n_ff_tiles, n_blk):
    b = pl.program_id(0)
    j = pl.program_id(1)
    rows = MOE_ROWS
    tf = MOE_FF_TILE
    grp = DMA_ISSUE_UNROLL
    cnt = cnt_ref[b]
    used = cnt > 0
    up_phase = j < n_ff_tiles
    n_sub = (cnt + MOE_SUB_ROWS - 1) // MOE_SUB_ROWS

    def row_copy(i, tok):
        return pltpu.make_async_copy(h_hbm.at[pl.ds(tok, 1)], xg_ref.at[pl.ds(i, 1)], sem)

    def issue_rows(toks_ref, count):
        def body(gi, carry):
            for r in range(grp):
                i = gi * grp + r
                row_copy(i, toks_ref[0, 0, i]).start()
            return carry
        lax.fori_loop(0, (count + grp - 1) // grp, body, 0)

    def wait_rows(count):
        def body(gi, carry):
            for r in range(grp):
                row_copy(gi * grp + r, 0).wait()
            return carry
        lax.fori_loop(0, (count + grp - 1) // grp, body, 0)

    @pl.when(jnp.logical_and(used, j == 0))
    def _():
        @pl.when(b == 0)
        def _():
            xg_ref[...] = jnp.zeros_like(xg_ref)
            issue_rows(tok_ref, cnt)

        wait_rows(cnt)
        xb_ref[...] = xg_ref[...].astype(BF16)
        nxt = jnp.minimum(b + 1, n_blk - 1)
        issue_rows(tok_next_ref, jnp.where(b + 1 < n_blk, cnt_ref[nxt], 0))

    for m in range(MOE_SUB_ROWS, rows + 1, MOE_SUB_ROWS):
        live = jnp.logical_and(used, n_sub == m // MOE_SUB_ROWS)

        @pl.when(jnp.logical_and(live, up_phase))
        def _(m=m):
            x = xb_ref[0:m, :]
            g = jnp.dot(x, wg_ref[0, 0].astype(BF16), preferred_element_type=F32)
            u = jnp.dot(x, wu_ref[0, 0].astype(BF16), preferred_element_type=F32)
            act_ref[j, 0:m, :] = (g * jax.nn.sigmoid(g) * u).astype(BF16)

        @pl.when(jnp.logical_and(live, jnp.logical_not(up_phase)))
        def _(m=m):
            acc = jnp.dot(act_ref[0, 0:m, :], wd_ref[0, 0, 0:tf, :].astype(BF16),
                          preferred_element_type=F32)
            for t in range(1, n_ff_tiles):
                acc = acc + jnp.dot(act_ref[t, 0:m, :],
                                    wd_ref[0, 0, t * tf:(t + 1) * tf, :].astype(BF16),
                                    preferred_element_type=F32)
            y_ref[0:m, :] = acc * sw_ref[0:m, :]
            if m < rows:
                y_ref[m:rows, :] = jnp.zeros((rows - m, y_ref.shape[1]), y_ref.dtype)

    @pl.when(jnp.logical_and(jnp.logical_not(used), jnp.logical_not(up_phase)))
    def _():
        y_ref[...] = jnp.zeros_like(y_ref)


def _moe_experts(h2, blk_expert, blk_cnt, slot_tok, slot_w, w_gate, w_up, w_down, layer):
    n_tok, d = h2.shape
    d_ff = w_gate.shape[3]
    rows = MOE_ROWS
    n_blk = slot_tok.shape[0] // rows
    tf = MOE_FF_TILE
    tn = MOE_OUT_TILE
    n_f = d_ff // tf
    n_o = d // tn

    def ff_tile(b, j, cnt):
        return jnp.where(cnt[b] > 0, jnp.minimum(j, n_f - 1), n_f - 1)

    def out_tile(b, j):
        return jnp.clip(j - n_f, 0, n_o - 1)

    def w_out_tile(b, j, cnt):
        return jnp.where(cnt[b] > 0, out_tile(b, j), n_o - 1)

    toks = slot_tok.reshape(n_blk, 1, rows)
    grid_spec = pltpu.PrefetchScalarGridSpec(
        num_scalar_prefetch=2,
        grid=(n_blk, n_f + n_o),
        in_specs=[pl.BlockSpec((1, 1, rows), lambda b, j, be, cnt: (b, 0, 0), memory_space=pltpu.SMEM),
                  pl.BlockSpec((1, 1, rows), lambda b, j, be, cnt: (jnp.minimum(b + 1, n_blk - 1), 0, 0),
                               memory_space=pltpu.SMEM),
                  pl.BlockSpec(memory_space=pl.ANY),
                  pl.BlockSpec((rows, 1), lambda b, j, be, cnt: (b, 0)),
                  pl.BlockSpec((1, 1, d, tf), lambda b, j, be, cnt: (layer, be[b], 0, ff_tile(b, j, cnt))),
                  pl.BlockSpec((1, 1, d, tf), lambda b, j, be, cnt: (layer, be[b], 0, ff_tile(b, j, cnt))),
                  pl.BlockSpec((1, 1, d_ff, tn),
                               lambda b, j, be, cnt: (layer, be[b], 0, w_out_tile(b, j, cnt)))],
        out_specs=pl.BlockSpec((rows, tn), lambda b, j, be, cnt: (b, out_tile(b, j))),
        scratch_shapes=[pltpu.VMEM((rows, d), F32),
                        pltpu.VMEM((rows, d), BF16),
                        pltpu.VMEM((n_f, rows, tf), BF16),
                        pltpu.SemaphoreType.DMA(())],
    )
    return pl.pallas_call(
        functools.partial(_moe_kernel, n_ff_tiles=n_f, n_blk=n_blk),
        grid_spec=grid_spec,
        out_shape=jax.ShapeDtypeStruct((n_blk * rows, d), F32),
        compiler_params=_cparams(("arbitrary", "arbitrary")),
        name="moe_experts",
    )(blk_expert, blk_cnt, toks, toks, h2, slot_w.reshape(-1, 1), w_gate, w_up, w_down)


def _combine_kernel(pos_ref, x_ref, y_hbm, o_ref, buf_ref, sem):
    rows = COMBINE_ROWS

    def row_copy(i, src):
        return pltpu.make_async_copy(y_hbm.at[pl.ds(src, 1)], buf_ref.at[pl.ds(i, 1)], sem)

    def issue(i, carry):
        row_copy(i, pos_ref[0, 0, i]).start()
        return carry

    def drain(i, carry):
        row_copy(i, 0).wait()
        return carry

    lax.fori_loop(0, TOP_K * rows, issue, 0, unroll=DMA_ISSUE_UNROLL)
    lax.fori_loop(0, TOP_K * rows, drain, 0, unroll=DMA_ISSUE_UNROLL)
    o_ref[...] = x_ref[...] + (buf_ref[pl.ds(0, rows), :] + buf_ref[pl.ds(rows, rows), :])


def _moe_combine(x1, ys, pos):
    n_tok, d = x1.shape
    rows = COMBINE_ROWS
    n_blk = n_tok // rows
    pos_b = pos.reshape(n_blk, rows, TOP_K).transpose(0, 2, 1).reshape(n_blk, 1, TOP_K * rows)
    return pl.pallas_call(
        _combine_kernel,
        grid=(n_blk,),
        in_specs=[pl.BlockSpec((1, 1, TOP_K * rows), lambda i: (i, 0, 0), memory_space=pltpu.SMEM),
                  pl.BlockSpec((rows, d), lambda i: (i, 0)),
                  pl.BlockSpec(memory_space=pl.ANY)],
        out_specs=pl.BlockSpec((rows, d), lambda i: (i, 0)),
        out_shape=jax.ShapeDtypeStruct((n_tok, d), F32),
        scratch_shapes=[pltpu.VMEM((TOP_K * rows, d), F32), pltpu.SemaphoreType.DMA(())],
        compiler_params=_cparams(("arbitrary",)),
        name="moe_combine",
    )(pos_b, x1, ys)


def _route(logits, n_tok):
    glog = logits[:, :N_GROUPS]
    gprob = jax.nn.softmax(glog, axis=-1)
    g_sel = jnp.argmax(glog, axis=-1)
    p_g = jnp.take_along_axis(gprob, g_sel[:, None], axis=1)[:, 0]
    elog = logits[:, N_GROUPS:N_GROUPS + N_EXPERTS].reshape(n_tok, N_GROUPS, EXPERTS_PER_GROUP)
    elog = jnp.take_along_axis(elog, g_sel[:, None, None], axis=1)[:, 0]
    top_v, top_i = lax.top_k(elog, TOP_K)
    weights = p_g[:, None] * jax.nn.softmax(top_v, axis=-1)
    expert_ids = (g_sel[:, None] * EXPERTS_PER_GROUP + top_i).astype(jnp.int32)

    n_assign = n_tok * TOP_K
    rows = MOE_ROWS
    flat_e = expert_ids.reshape(n_assign)
    flat_tok = jnp.repeat(jnp.arange(n_tok, dtype=jnp.int32), TOP_K)
    flat_w = weights.reshape(n_assign)
    order = jnp.argsort(flat_e)
    se = flat_e[order]
    counts = jnp.zeros((N_EXPERTS,), jnp.int32).at[flat_e].add(1)
    blocks = (counts + rows - 1) // rows
    pad_end = jnp.cumsum(blocks * rows)
    pad_start = pad_end - blocks * rows
    start = jnp.cumsum(counts) - counts
    dest = pad_start[se] + jnp.arange(n_assign, dtype=jnp.int32) - start[se]
    n_slots = n_assign + N_EXPERTS * rows
    n_blk = n_slots // rows
    slot_tok = (jnp.arange(n_slots, dtype=jnp.int32) % n_tok).at[dest].set(flat_tok[order])
    slot_w = jnp.zeros((n_slots,), F32).at[dest].set(flat_w[order])
    n_used = jnp.sum(blocks).astype(jnp.int32)
    blk_ids = jnp.arange(n_blk, dtype=jnp.int32)
    blk_expert = jnp.minimum(jnp.searchsorted(pad_end, blk_ids * rows, side='right'),
                             N_EXPERTS - 1).astype(jnp.int32)
    blk_cnt = jnp.clip(counts[blk_expert] - (blk_ids * rows - pad_start[blk_expert]), 0, rows)
    blk_cnt = jnp.where(blk_ids < n_used, blk_cnt, 0).astype(jnp.int32)
    blk_expert = jnp.where(blk_ids < n_used, blk_expert, blk_expert[jnp.maximum(n_used - 1, 0)])
    pos = jnp.zeros((n_assign,), jnp.int32).at[order].set(dest).reshape(n_tok, TOP_K)
    return blk_expert, blk_cnt, slot_tok, slot_w, pos


def kernel(x, norm1_gain, w_in, attn_q_norm_gain, attn_k_norm_gain, mlstm_gate_bias,
           mlstm_head_norm_gain, branch_gate_bias, w_attn_branch, w_mlstm_branch, w_out,
           norm2_gain, w_router_group, b_router_group, w_router_expert, b_router_expert,
           w_expert_gate, w_expert_up, w_expert_down):
    bsz, s_len, d = x.shape
    assert bsz == 1
    depth = w_in.shape[0]
    xs = x.reshape(s_len, d)
    w_bg = w_in[:, :, OFF_MG + N_ML_GATES:]
    for l in range(depth):
        h = _rmsnorm(xs, norm1_gain[l])
        proj = _matmul(h, w_in, l, 0, OFF_MG, BF16, 1024, 512, "in_proj")
        gates = _matmul(h, w_in, l, OFF_MG, LANES, F32, 1024, LANES, "gate_proj")[:, :N_ML_GATES]
        bgates = _matmul(h, w_bg, l, 0, 2 * d, BF16, 1024, 512, "branch_gate_proj")

        att = _attention(proj, attn_q_norm_gain[l], attn_k_norm_gain[l])

        hs = _mlstm(proj, gates, mlstm_gate_bias[l])
        mem = _mlstm_out(hs, proj, mlstm_head_norm_gain[l])

        merged = _branch_mix(att, mem, w_attn_branch, w_mlstm_branch, l, bgates, branch_gate_bias)
        x1 = _matmul_residual(merged, w_out, l, xs, 1024, 512)

        w_r = jnp.concatenate([w_router_group[l], w_router_expert[l]], axis=1)
        b_r = jnp.concatenate([b_router_group[l], b_router_expert[l]])
        n_r = w_r.shape[1]
        w_r = jnp.pad(w_r, ((0, 0), (0, LANES - n_r)))
        b_r = jnp.pad(b_r, (0, LANES - n_r)).reshape(1, LANES)
        h2, logits = _norm_router(x1, norm2_gain[l], w_r, b_r)

        blk_expert, blk_cnt, slot_tok, slot_w, pos = _route(logits, s_len)
        ys = _moe_experts(h2, blk_expert, blk_cnt, slot_tok, slot_w,
                          w_expert_gate, w_expert_up, w_expert_down, l)
        xs = _moe_combine(x1, ys, pos)
    return xs.reshape(bsz, s_len, d)
```

```python
import functools

import jax
import jax.numpy as jnp
from jax import lax
from jax.experimental import pallas as pl
from jax.experimental.pallas import tpu as pltpu

F32 = jnp.float32
BF16 = jnp.bfloat16

EPS = 1e-6
NEG = -1e30
LANES = 128

ATT_PATTERNS = ((128, 1), (512, 4), (2048, 16))
N_ATT_GROUPS = 3
ATT_HEADS = 8
ATT_HEAD_DIM = 128
ATT_GROUP_W = ATT_HEADS * ATT_HEAD_DIM
ATT_W = N_ATT_GROUPS * ATT_GROUP_W
ROPE_THETA = 10000.0
ATT_Q_TILE = 128
ATT_TILE_UNROLL = 4

ML_HEADS = 8
ML_QK_DIM = 256
ML_V_DIM = 512
ML_QK_W = ML_HEADS * ML_QK_DIM
ML_V_W = ML_HEADS * ML_V_DIM
N_ML_GATES = 4 * ML_HEADS
GATE_SOFTCAP = 15.0
ML_TILE = 256

N_GROUPS = 8
EXPERTS_PER_GROUP = 8
N_EXPERTS = N_GROUPS * EXPERTS_PER_GROUP
TOP_K = 2
MOE_ROWS = 512
MOE_SUB_ROWS = 128
MOE_FF_TILE = 256
MOE_OUT_TILE = 1024
COMBINE_ROWS = 256
DMA_ISSUE_UNROLL = 8

OFF_AQ = 0
OFF_AK = OFF_AQ + ATT_W
OFF_AV = OFF_AK + ATT_W
OFF_MQ = OFF_AV + ATT_W
OFF_MK = OFF_MQ + ML_QK_W
OFF_MV = OFF_MK + ML_QK_W
OFF_MO = OFF_MV + ML_V_W
OFF_MG = OFF_MO + ML_V_W
OFF_GA = OFF_MG

VMEM_LIMIT = 56 * 1024 * 1024


def _cparams(sem, vmem=VMEM_LIMIT):
    return pltpu.CompilerParams(dimension_semantics=sem, vmem_limit_bytes=vmem)


def _rmsnorm_kernel(x_ref, g_ref, o_ref):
    x = x_ref[...]
    ms = jnp.mean(x * x, axis=-1, keepdims=True)
    o_ref[...] = (x * lax.rsqrt(ms + EPS) * g_ref[...]).astype(o_ref.dtype)


def _rmsnorm(x, gain, tm=256):
    m, d = x.shape
    return pl.pallas_call(
        _rmsnorm_kernel,
        grid=(m // tm,),
        in_specs=[pl.BlockSpec((tm, d), lambda i: (i, 0)),
                  pl.BlockSpec((1, d), lambda i: (0, 0))],
        out_specs=pl.BlockSpec((tm, d), lambda i: (i, 0)),
        out_shape=jax.ShapeDtypeStruct((m, d), BF16),
        compiler_params=_cparams(("parallel",)),
        name="rmsnorm",
    )(x, gain.reshape(1, d))


def _mm_kernel(a_ref, w_ref, o_ref, wb_ref):
    @pl.when(pl.program_id(1) == 0)
    def _():
        wb_ref[...] = w_ref[0].astype(BF16)

    o_ref[...] = jnp.dot(a_ref[...], wb_ref[...], preferred_element_type=F32).astype(o_ref.dtype)


def _matmul(a, w, layer, col0, n, out_dtype, tm, tn, name):
    m, k = a.shape
    cb = col0 // tn
    return pl.pallas_call(
        _mm_kernel,
        grid=(n // tn, m // tm),
        in_specs=[pl.BlockSpec((tm, k), lambda j, i: (i, 0)),
                  pl.BlockSpec((1, k, tn), lambda j, i: (layer, 0, cb + j))],
        out_specs=pl.BlockSpec((tm, tn), lambda j, i: (i, j)),
        out_shape=jax.ShapeDtypeStruct((m, n), out_dtype),
        scratch_shapes=[pltpu.VMEM((k, tn), BF16)],
        compiler_params=_cparams(("parallel", "arbitrary")),
        name=name,
    )(a, w)


def _mm_nt_kernel(a_ref, wt_ref, o_ref, wb_ref):
    @pl.when(pl.program_id(1) == 0)
    def _():
        wb_ref[...] = wt_ref[0].astype(BF16)

    o_ref[...] = lax.dot_general(a_ref[...], wb_ref[...], (((1,), (1,)), ((), ())),
                                 preferred_element_type=F32).astype(o_ref.dtype)


def _matmul_nt(a, w_t, layer, row_of_tile, n, out_dtype, tm, tn, name):
    m, k = a.shape
    return pl.pallas_call(
        _mm_nt_kernel,
        grid=(n // tn, m // tm),
        in_specs=[pl.BlockSpec((tm, k), lambda j, i: (i, 0)),
                  pl.BlockSpec((pl.Element(1), pl.Element(tn), pl.Element(k)),
                               lambda j, i: (layer, pl.multiple_of(row_of_tile(j), 8), 0))],
        out_specs=pl.BlockSpec((tm, tn), lambda j, i: (i, j)),
        out_shape=jax.ShapeDtypeStruct((m, n), out_dtype),
        scratch_shapes=[pltpu.VMEM((tn, k), BF16)],
        compiler_params=_cparams(("parallel", "arbitrary")),
        name=name,
    )(a, w_t)


def _mm_res_kernel(a_ref, w_ref, r_ref, o_ref, wb_ref):
    @pl.when(pl.program_id(1) == 0)
    def _():
        wb_ref[...] = w_ref[0].astype(BF16)

    o_ref[...] = r_ref[...] + jnp.dot(a_ref[...], wb_ref[...], preferred_element_type=F32)


def _matmul_residual(a, w, layer, res, tm, tn):
    m, k = a.shape
    n = w.shape[2]
    return pl.pallas_call(
        _mm_res_kernel,
        grid=(n // tn, m // tm),
        in_specs=[pl.BlockSpec((tm, k), lambda j, i: (i, 0)),
                  pl.BlockSpec((1, k, tn), lambda j, i: (layer, 0, j)),
                  pl.BlockSpec((tm, tn), lambda j, i: (i, j))],
        out_specs=pl.BlockSpec((tm, tn), lambda j, i: (i, j)),
        out_shape=jax.ShapeDtypeStruct((m, n), F32),
        scratch_shapes=[pltpu.VMEM((k, tn), BF16)],
        compiler_params=_cparams(("parallel", "arbitrary")),
        name="out_proj",
    )(a, w, res)


def _attn_kernel(q_ref, k_ref, v_ref, tab_ref, qg_ref, kg_ref, o_ref,
                 qn_ref, kn_ref, vf_ref, acc_ref, lse_ref, *, s_len):
    grp = pl.program_id(1)
    tq = ATT_Q_TILE
    half = ATT_HEAD_DIM // 2
    scale = ATT_HEAD_DIM ** -0.5
    low = lax.broadcasted_iota(jnp.int32, (tq, ATT_HEAD_DIM), 1) < half

    def prep(i, carry):
        rows = pl.ds(pl.multiple_of(i * tq, tq), tq)
        tab = tab_ref[rows, :]
        swapped = pltpu.roll(tab, half, axis=1)
        c = jnp.where(low, tab, swapped)
        s = jnp.where(low, -swapped, tab)
        for src, g_ref, dst, mul in ((q_ref, qg_ref, qn_ref, scale), (k_ref, kg_ref, kn_ref, 1.0)):
            xf = src[rows, :].astype(F32)
            y = xf * lax.rsqrt(jnp.mean(xf * xf, axis=-1, keepdims=True) + EPS) * g_ref[0]
            dst[rows, :] = (y * c + pltpu.roll(y, half, axis=1) * s) * mul
        vf_ref[rows, :] = v_ref[rows, :].astype(F32)
        return carry

    lax.fori_loop(0, s_len // tq, prep, 0, unroll=2)

    def run_group(window, dilation, first):
        radius = window // (2 * dilation)
        sub_len = s_len // dilation
        win = tq + 2 * radius
        tiles_per_class = sub_len // tq
        rel = (lax.broadcasted_iota(jnp.int32, (tq, win), 1)
               - lax.broadcasted_iota(jnp.int32, (tq, win), 0))

        def rows_of(start, size):
            if dilation == 1:
                return pl.ds(start, size)
            return pl.ds(start, size, stride=dilation)

        def tile(idx, carry):
            r = idx // tiles_per_class
            q0 = (idx % tiles_per_class) * tq
            k0 = jnp.clip(q0 - radius, 0, sub_len - win)
            q_rows = rows_of(r + dilation * q0, tq)
            k_rows = rows_of(r + dilation * k0, win)
            q = qn_ref[q_rows, :].astype(BF16)
            k = kn_ref[k_rows, :].astype(BF16)
            v = vf_ref[k_rows, :].astype(BF16)
            s = lax.dot_general(q, k, (((1,), (1,)), ((), ())), preferred_element_type=F32)
            s = jnp.where(jnp.abs(rel + (k0 - q0)) <= radius, s, NEG)
            m = jnp.max(s, axis=-1, keepdims=True)
            p = jnp.exp(s - m)
            den = jnp.sum(p, axis=-1, keepdims=True)
            o = jnp.dot(p.astype(BF16), v, preferred_element_type=F32) / den
            lse = jnp.broadcast_to(m + jnp.log(den), (tq, ATT_HEAD_DIM))
            if first:
                acc_ref[q_rows, :] = o
                lse_ref[q_rows, :] = lse
            else:
                old_l = lse_ref[q_rows, :]
                top = jnp.maximum(old_l, lse)
                a = jnp.exp(old_l - top)
                b = jnp.exp(lse - top)
                tot = a + b
                acc_ref[q_rows, :] = (a * acc_ref[q_rows, :] + b * o) / tot
                lse_ref[q_rows, :] = top + jnp.log(tot)
            return carry

        lax.fori_loop(0, s_len // tq, tile, 0, unroll=ATT_TILE_UNROLL)

    for gi, (window, dilation) in enumerate(ATT_PATTERNS):
        @pl.when(grp == gi)
        def _(window=window, dilation=dilation, gi=gi):
            run_group(window, dilation, gi == 0)

    @pl.when(grp == N_ATT_GROUPS - 1)
    def _():
        o_ref[...] = acc_ref[...].astype(o_ref.dtype)


def _attention(proj, q_gain, k_gain):
    s_len = proj.shape[0]
    half = ATT_HEAD_DIM // 2
    inv_freq = ROPE_THETA ** (-jnp.arange(half, dtype=F32) / half)
    ang = jnp.arange(s_len).astype(F32)[:, None] * inv_freq[None, :]
    table = jnp.concatenate([jnp.cos(ang), jnp.sin(ang)], axis=1)
    qb, kb, vb = OFF_AQ // LANES, OFF_AK // LANES, OFF_AV // LANES
    blk = (s_len, ATT_HEAD_DIM)
    gain_spec = pl.BlockSpec((1, 1, ATT_HEAD_DIM), lambda h, g: (g, 0, 0))
    return pl.pallas_call(
        functools.partial(_attn_kernel, s_len=s_len),
        grid=(ATT_HEADS, N_ATT_GROUPS),
        in_specs=[pl.BlockSpec(blk, lambda h, g: (0, qb + g * ATT_HEADS + h)),
                  pl.BlockSpec(blk, lambda h, g: (0, kb + g * ATT_HEADS + h)),
                  pl.BlockSpec(blk, lambda h, g: (0, vb + g * ATT_HEADS + h)),
                  pl.BlockSpec(blk, lambda h, g: (0, 0)),
                  gain_spec, gain_spec],
        out_specs=pl.BlockSpec(blk, lambda h, g: (0, h)),
        out_shape=jax.ShapeDtypeStruct((s_len, ATT_GROUP_W), BF16),
        scratch_shapes=[pltpu.VMEM(blk, F32)] * 5,
        compiler_params=_cparams(("parallel", "arbitrary")),
        name="dilated_attn",
    )(proj, proj, proj, table,
      q_gain.reshape(N_ATT_GROUPS, 1, ATT_HEAD_DIM), k_gain.reshape(N_ATT_GROUPS, 1, ATT_HEAD_DIM))


def _mlstm_kernel(q_ref, k_ref, v_ref, g_ref, b_ref, o_ref, c_ref, n_ref, m_ref):
    lc = ML_TILE
    dh = pl.program_id(0)
    step = pl.program_id(1)
    is_bw = dh >= ML_HEADS
    sgn = jnp.where(is_bw, -1, 1)

    @pl.when(step == 0)
    def _():
        c_ref[...] = jnp.zeros_like(c_ref)
        n_ref[...] = jnp.zeros_like(n_ref)
        m_ref[...] = jnp.zeros_like(m_ref)

    g = g_ref[0, 0] + b_ref[0]
    g = GATE_SOFTCAP * jnp.tanh(g / GATE_SOFTCAP)
    ig_row = g[0:1, :]
    f = g[1:2, :]
    lf_row = jnp.minimum(f, 0.0) - jnp.log1p(jnp.exp(-jnp.abs(f)))

    t_idx = lax.broadcasted_iota(jnp.int32, (lc, lc), 0)
    s_idx = lax.broadcasted_iota(jnp.int32, (lc, lc), 1)
    valid = (s_idx - t_idx) * sgn <= 0
    eye = s_idx == t_idx
    b_col = jnp.sum(jnp.where(valid, lf_row, 0.0), axis=1, keepdims=True)
    b_row = jnp.sum(jnp.where(eye, b_col, 0.0), axis=0, keepdims=True)
    ig_col = jnp.sum(jnp.where(eye, ig_row, 0.0), axis=1, keepdims=True)

    log_d = jnp.where(valid, b_col - b_row + ig_row, NEG)
    m_prev = m_ref[...]
    m_inter = b_col + m_prev
    m_t = jnp.maximum(m_inter, jnp.max(log_d, axis=1, keepdims=True))
    decay_q = jnp.exp(m_inter - m_t)
    d_mat = jnp.exp(log_d - m_t)

    q = q_ref[...]
    k = k_ref[...] * (ML_QK_DIM ** -0.5)
    v = v_ref[...]
    s = lax.dot_general(q, k, (((1,), (1,)), ((), ())), preferred_element_type=F32) * d_mat
    c_old = c_ref[...]
    num = jnp.dot(s.astype(BF16), v, preferred_element_type=F32)
    num = num + decay_q * jnp.dot(q, c_old.astype(BF16), preferred_element_type=F32)
    qn = jnp.sum(q.astype(F32) * n_ref[...], axis=1, keepdims=True)
    den = jnp.sum(s, axis=1, keepdims=True) + decay_q * qn
    o_ref[0] = num / jnp.maximum(jnp.abs(den), jnp.exp(-m_t))

    last = jnp.where(is_bw, 0, lc - 1)
    sel = lax.broadcasted_iota(jnp.int32, (lc, 1), 0) == last
    m_last = jnp.max(jnp.where(sel, m_t, -jnp.inf), axis=0, keepdims=True)
    b_last = jnp.sum(jnp.where(sel, b_col, 0.0), axis=0, keepdims=True)
    w_col = jnp.exp(b_last - b_col + ig_col - m_last)
    decay_c = jnp.exp(b_last + m_prev - m_last)
    kw = k.astype(F32) * w_col
    c_ref[...] = decay_c * c_old + lax.dot_general(
        kw.astype(BF16), v, (((0,), (0,)), ((), ())), preferred_element_type=F32)
    n_ref[...] = decay_c * n_ref[...] + jnp.sum(kw, axis=0, keepdims=True)
    m_ref[...] = m_last


def _mlstm(proj, gates, gate_bias):
    s_len = proj.shape[0]
    lc = ML_TILE
    nt = s_len // lc
    gt = gates.reshape(s_len, 2, 2, ML_HEADS).transpose(1, 3, 2, 0)
    gt = gt.reshape(2 * ML_HEADS, 2, nt, lc).transpose(0, 2, 1, 3)
    bias = gate_bias.reshape(2, 2, ML_HEADS).transpose(0, 2, 1).reshape(2 * ML_HEADS, 2, 1)

    def tile_of(dh, t):
        return jnp.where(dh >= ML_HEADS, nt - 1 - t, t)

    qb, kb = OFF_MQ // ML_QK_DIM, OFF_MK // ML_QK_DIM
    vb = OFF_MV // ML_V_DIM
    return pl.pallas_call(
        _mlstm_kernel,
        grid=(2 * ML_HEADS, nt),
        in_specs=[pl.BlockSpec((lc, ML_QK_DIM), lambda dh, t: (tile_of(dh, t), qb + dh % ML_HEADS)),
                  pl.BlockSpec((lc, ML_QK_DIM), lambda dh, t: (tile_of(dh, t), kb + dh % ML_HEADS)),
                  pl.BlockSpec((lc, ML_V_DIM), lambda dh, t: (tile_of(dh, t), vb + dh % ML_HEADS)),
                  pl.BlockSpec((1, 1, 2, lc), lambda dh, t: (dh, tile_of(dh, t), 0, 0)),
                  pl.BlockSpec((1, 2, 1), lambda dh, t: (dh, 0, 0))],
        out_specs=pl.BlockSpec((1, lc, ML_V_DIM),
                               lambda dh, t: (dh // ML_HEADS, tile_of(dh, t), dh % ML_HEADS)),
        out_shape=jax.ShapeDtypeStruct((2, s_len, ML_V_W), F32),
        scratch_shapes=[pltpu.VMEM((ML_QK_DIM, ML_V_DIM), F32),
                        pltpu.VMEM((1, ML_QK_DIM), F32),
                        pltpu.VMEM((1, 1), F32)],
        compiler_params=_cparams(("parallel", "arbitrary")),
        name="mlstm_scan",
    )(proj, proj, proj, gt, bias)


def _mlstm_out_kernel(h_ref, mo_ref, g_ref, o_ref):
    h = h_ref[0] + h_ref[1]
    y = h * lax.rsqrt(jnp.mean(h * h, axis=-1, keepdims=True) + EPS) * g_ref[0]
    o_ref[...] = (y * jax.nn.sigmoid(mo_ref[...].astype(F32))).astype(o_ref.dtype)


def _mlstm_out(hs, proj, head_gain, tm=512):
    s_len = proj.shape[0]
    ob = OFF_MO // ML_V_DIM
    return pl.pallas_call(
        _mlstm_out_kernel,
        grid=(s_len // tm, ML_HEADS),
        in_specs=[pl.BlockSpec((2, tm, ML_V_DIM), lambda i, h: (0, i, h)),
                  pl.BlockSpec((tm, ML_V_DIM), lambda i, h: (i, ob + h)),
                  pl.BlockSpec((1, 1, ML_V_DIM), lambda i, h: (h, 0, 0))],
        out_specs=pl.BlockSpec((tm, ML_V_DIM), lambda i, h: (i, h)),
        out_shape=jax.ShapeDtypeStruct((s_len, ML_V_W), BF16),
        compiler_params=_cparams(("parallel", "parallel")),
        name="mlstm_out",
    )(hs, proj, head_gain.reshape(ML_HEADS, 1, ML_V_DIM))


def _branch_kernel(att_ref, mem_ref, wa_ref, wm_ref, ga_ref, gm_ref, ba_ref, bm_ref, o_ref,
                   wab_ref, wmb_ref):
    @pl.when(pl.program_id(1) == 0)
    def _():
        wab_ref[...] = wa_ref[0].astype(BF16)
        wmb_ref[...] = wm_ref[0].astype(BF16)

    ya = jnp.dot(att_ref[...], wab_ref[...], preferred_element_type=F32)
    ym = jnp.dot(mem_ref[...], wmb_ref[...], preferred_element_type=F32)
    g_a = jax.nn.sigmoid(ga_ref[...].astype(F32) + ba_ref[0])
    g_m = jax.nn.sigmoid(gm_ref[...].astype(F32) + bm_ref[0])
    o_ref[...] = (g_a * ya + g_m * ym).astype(o_ref.dtype)


def _branch_mix(att, mem, w_a, w_m, layer, proj, gate_bias, tm=512, tn=512):
    m = att.shape[0]
    ka, km, d = w_a.shape[1], w_m.shape[1], w_a.shape[2]
    bias = gate_bias.reshape(-1, 1, d)
    gab = OFF_GA // tn
    gmb = (OFF_GA + d) // tn
    return pl.pallas_call(
        _branch_kernel,
        grid=(d // tn, m // tm),
        in_specs=[pl.BlockSpec((tm, ka), lambda j, i: (i, 0)),
                  pl.BlockSpec((tm, km), lambda j, i: (i, 0)),
                  pl.BlockSpec((1, ka, tn), lambda j, i: (layer, 0, j)),
                  pl.BlockSpec((1, km, tn), lambda j, i: (layer, 0, j)),
                  pl.BlockSpec((tm, tn), lambda j, i: (i, gab + j)),
                  pl.BlockSpec((tm, tn), lambda j, i: (i, gmb + j)),
                  pl.BlockSpec((1, 1, tn), lambda j, i: (2 * layer, 0, j)),
                  pl.BlockSpec((1, 1, tn), lambda j, i: (2 * layer + 1, 0, j))],
        out_specs=pl.BlockSpec((tm, tn), lambda j, i: (i, j)),
        out_shape=jax.ShapeDtypeStruct((m, d), BF16),
        scratch_shapes=[pltpu.VMEM((ka, tn), BF16), pltpu.VMEM((km, tn), BF16)],
        compiler_params=_cparams(("parallel", "arbitrary")),
        name="branch_mix",
    )(att, mem, w_a, w_m, proj, proj, bias, bias)


def _norm_router_kernel(x_ref, g_ref, wr_ref, br_ref, h_ref, lg_ref):
    x = x_ref[...]
    h = x * lax.rsqrt(jnp.mean(x * x, axis=-1, keepdims=True) + EPS) * g_ref[...]
    h_ref[...] = h
    lg_ref[...] = jnp.dot(h, wr_ref[...], preferred_element_type=F32,
                          precision=lax.Precision.HIGHEST) + br_ref[...]


def _norm_router(x, gain, w_r, b_r, tm=256):
    m, d = x.shape
    n = w_r.shape[1]
    return pl.pallas_call(
        _norm_router_kernel,
        grid=(m // tm,),
        in_specs=[pl.BlockSpec((tm, d), lambda i: (i, 0)),
                  pl.BlockSpec((1, d), lambda i: (0, 0)),
                  pl.BlockSpec((d, n), lambda i: (0, 0)),
                  pl.BlockSpec((1, n), lambda i: (0, 0))],
        out_specs=[pl.BlockSpec((tm, d), lambda i: (i, 0)),
                   pl.BlockSpec((tm, n), lambda i: (i, 0))],
        out_shape=[jax.ShapeDtypeStruct((m, d), F32),
                   jax.ShapeDtypeStruct((m, n), F32)],
        compiler_params=_cparams(("parallel",)),
        name="norm_router",
    )(x, gain.reshape(1, d), w_r, b_r)


def _moe_kernel(blk_e_ref, cnt_ref, tok_ref, tok_next_ref, h_hbm, sw_ref, wg_ref, wu_ref, wd_ref,
                y_ref, xg_ref, xb_ref, act_ref, sem, *, n_ff_tiles, n_blk):
    b = pl.program_id(0)
    j = pl.program_id(1)
    rows = MOE_ROWS
    tf = MOE_FF_TILE
    grp = DMA_ISSUE_UNROLL
    cnt = cnt_ref[b]
    used = cnt > 0
    up_phase = j < n_ff_tiles
    n_sub = (cnt + MOE_SUB_ROWS - 1) // MOE_SUB_ROWS

    def row_copy(i, tok):
        return pltpu.make_async_copy(h_hbm.at[pl.ds(tok, 1)], xg_ref.at[pl.ds(i, 1)], sem)

    def issue_rows(toks_ref, count):
        def body(gi, carry):
            for r in range(grp):
                i = gi * grp + r
                row_copy(i, toks_ref[0, 0, i]).start()
            return carry
        lax.fori_loop(0, (count + grp - 1) // grp, body, 0)

    def wait_rows(count):
        def body(gi, carry):
            for r in range(grp):
                row_copy(gi * grp + r, 0).wait()
            return carry
        lax.fori_loop(0, (count + grp - 1) // grp, body, 0)

    @pl.when(jnp.logical_and(used, j == 0))
    def _():
        @pl.when(b == 0)
        def _():
            xg_ref[...] = jnp.zeros_like(xg_ref)
            issue_rows(tok_ref, cnt)

        wait_rows(cnt)
        xb_ref[...] = xg_ref[...].astype(BF16)
        nxt = jnp.minimum(b + 1, n_blk - 1)
        issue_rows(tok_next_ref, jnp.where(b + 1 < n_blk, cnt_ref[nxt], 0))

    for m in range(MOE_SUB_ROWS, rows + 1, MOE_SUB_ROWS):
        live = jnp.logical_and(used, n_sub == m // MOE_SUB_ROWS)

        @pl.when(jnp.logical_and(live, up_phase))
        def _(m=m):
            x = xb_ref[0:m, :]
            g = jnp.dot(x, wg_ref[0, 0].astype(BF16), preferred_element_type=F32)
            u = jnp.dot(x, wu_ref[0, 0].astype(BF16), preferred_element_type=F32)
            act_ref[j, 0:m, :] = (g * jax.nn.sigmoid(g) * u).astype(BF16)

        @pl.when(jnp.logical_and(live, jnp.logical_not(up_phase)))
        def _(m=m):
            acc = jnp.dot(act_ref[0, 0:m, :], wd_ref[0, 0, 0:tf, :].astype(BF16),
                          preferred_element_type=F32)
            for t in range(1, n_ff_tiles):
                acc = acc + jnp.dot(act_ref[t, 0:m, :],
                                    wd_ref[0, 0, t * tf:(t + 1) * tf, :].astype(BF16),
                                    preferred_element_type=F32)
            y_ref[0:m, :] = acc * sw_ref[0:m, :]
            if m < rows:
                y_ref[m:rows, :] = jnp.zeros((rows - m, y_ref.shape[1]), y_ref.dtype)

    @pl.when(jnp.logical_and(jnp.logical_not(used), jnp.logical_not(up_phase)))
    def _():
        y_ref[...] = jnp.zeros_like(y_ref)


def _moe_experts(h2, blk_expert, blk_cnt, slot_tok, slot_w, w_gate, w_up, w_down, layer):
    n_tok, d = h2.shape
    d_ff = w_gate.shape[3]
    rows = MOE_ROWS
    n_blk = slot_tok.shape[0] // rows
    tf = MOE_FF_TILE
    tn = MOE_OUT_TILE
    n_f = d_ff // tf
    n_o = d // tn

    def ff_tile(b, j, cnt):
        return jnp.where(cnt[b] > 0, jnp.minimum(j, n_f - 1), n_f - 1)

    def out_tile(b, j):
        return jnp.clip(j - n_f, 0, n_o - 1)

    def w_out_tile(b, j, cnt):
        return jnp.where(cnt[b] > 0, out_tile(b, j), n_o - 1)

    toks = slot_tok.reshape(n_blk, 1, rows)
    grid_spec = pltpu.PrefetchScalarGridSpec(
        num_scalar_prefetch=2,
        grid=(n_blk, n_f + n_o),
        in_specs=[pl.BlockSpec((1, 1, rows), lambda b, j, be, cnt: (b, 0, 0), memory_space=pltpu.SMEM),
                  pl.BlockSpec((1, 1, rows), lambda b, j, be, cnt: (jnp.minimum(b + 1, n_blk - 1), 0, 0),
                               memory_space=pltpu.SMEM),
                  pl.BlockSpec(memory_space=pl.ANY),
                  pl.BlockSpec((rows, 1), lambda b, j, be, cnt: (b, 0)),
                  pl.BlockSpec((1, 1, d, tf), lambda b, j, be, cnt: (layer, be[b], 0, ff_tile(b, j, cnt))),
                  pl.BlockSpec((1, 1, d, tf), lambda b, j, be, cnt: (layer, be[b], 0, ff_tile(b, j, cnt))),
                  pl.BlockSpec((1, 1, d_ff, tn),
                               lambda b, j, be, cnt: (layer, be[b], 0, w_out_tile(b, j, cnt)))],
        out_specs=pl.BlockSpec((rows, tn), lambda b, j, be, cnt: (b, out_tile(b, j))),
        scratch_shapes=[pltpu.VMEM((rows, d), F32),
                        pltpu.VMEM((rows, d), BF16),
                        pltpu.VMEM((n_f, rows, tf), BF16),
                        pltpu.SemaphoreType.DMA(())],
    )
    return pl.pallas_call(
        functools.partial(_moe_kernel, n_ff_tiles=n_f, n_blk=n_blk),
        grid_spec=grid_spec,
        out_shape=jax.ShapeDtypeStruct((n_blk * rows, d), F32),
        compiler_params=_cparams(("arbitrary", "arbitrary")),
        name="moe_experts",
    )(blk_expert, blk_cnt, toks, toks, h2, slot_w.reshape(-1, 1), w_gate, w_up, w_down)


def _combine_kernel(pos_ref, x_ref, y_hbm, o_ref, buf_ref, sem):
    rows = COMBINE_ROWS

    def row_copy(i, src):
        return pltpu.make_async_copy(y_hbm.at[pl.ds(src, 1)], buf_ref.at[pl.ds(i, 1)], sem)

    def issue(i, carry):
        row_copy(i, pos_ref[0, 0, i]).start()
        return carry

    def drain(i, carry):
        row_copy(i, 0).wait()
        return carry

    lax.fori_loop(0, TOP_K * rows, issue, 0, unroll=DMA_ISSUE_UNROLL)
    lax.fori_loop(0, TOP_K * rows, drain, 0, unroll=DMA_ISSUE_UNROLL)
    o_ref[...] = x_ref[...] + (buf_ref[pl.ds(0, rows), :] + buf_ref[pl.ds(rows, rows), :])


def _moe_combine(x1, ys, pos):
    n_tok, d = x1.shape
    rows = COMBINE_ROWS
    n_blk = n_tok // rows
    pos_b = pos.reshape(n_blk, rows, TOP_K).transpose(0, 2, 1).reshape(n_blk, 1, TOP_K * rows)
    return pl.pallas_call(
        _combine_kernel,
        grid=(n_blk,),
        in_specs=[pl.BlockSpec((1, 1, TOP_K * rows), lambda i: (i, 0, 0), memory_space=pltpu.SMEM),
                  pl.BlockSpec((rows, d), lambda i: (i, 0)),
                  pl.BlockSpec(memory_space=pl.ANY)],
        out_specs=pl.BlockSpec((rows, d), lambda i: (i, 0)),
        out_shape=jax.ShapeDtypeStruct((n_tok, d), F32),
        scratch_shapes=[pltpu.VMEM((TOP_K * rows, d), F32), pltpu.SemaphoreType.DMA(())],
        compiler_params=_cparams(("arbitrary",)),
        name="moe_combine",
    )(pos_b, x1, ys)


def _route(logits, n_tok):
    glog = logits[:, :N_GROUPS]
    gprob = jax.nn.softmax(glog, axis=-1)
    g_sel = jnp.argmax(glog, axis=-1)
    p_g = jnp.take_along_axis(gprob, g_sel[:, None], axis=1)[:, 0]
    elog = logits[:, N_GROUPS:N_GROUPS + N_EXPERTS].reshape(n_tok, N_GROUPS, EXPERTS_PER_GROUP)
    elog = jnp.take_along_axis(elog, g_sel[:, None, None], axis=1)[:, 0]
    top_v, top_i = lax.top_k(elog, TOP_K)
    weights = p_g[:, None] * jax.nn.softmax(top_v, axis=-1)
    expert_ids = (g_sel[:, None] * EXPERTS_PER_GROUP + top_i).astype(jnp.int32)

    n_assign = n_tok * TOP_K
    rows = MOE_ROWS
    flat_e = expert_ids.reshape(n_assign)
    flat_tok = jnp.repeat(jnp.arange(n_tok, dtype=jnp.int32), TOP_K)
    flat_w = weights.reshape(n_assign)
    order = jnp.argsort(flat_e)
    se = flat_e[order]
    counts = jnp.zeros((N_EXPERTS,), jnp.int32).at[flat_e].add(1)
    blocks = (counts + rows - 1) // rows
    pad_end = jnp.cumsum(blocks * rows)
    pad_start = pad_end - blocks * rows
    start = jnp.cumsum(counts) - counts
    dest = pad_start[se] + jnp.arange(n_assign, dtype=jnp.int32) - start[se]
    n_slots = n_assign + N_EXPERTS * rows
    n_blk = n_slots // rows
    slot_tok = (jnp.arange(n_slots, dtype=jnp.int32) % n_tok).at[dest].set(flat_tok[order])
    slot_w = jnp.zeros((n_slots,), F32).at[dest].set(flat_w[order])
    n_used = jnp.sum(blocks).astype(jnp.int32)
    blk_ids = jnp.arange(n_blk, dtype=jnp.int32)
    blk_expert = jnp.minimum(jnp.searchsorted(pad_end, blk_ids * rows, side='right'),
                             N_EXPERTS - 1).astype(jnp.int32)
    blk_cnt = jnp.clip(counts[blk_expert] - (blk_ids * rows - pad_start[blk_expert]), 0, rows)
    blk_cnt = jnp.where(blk_ids < n_used, blk_cnt, 0).astype(jnp.int32)
    blk_expert = jnp.where(blk_ids < n_used, blk_expert, blk_expert[jnp.maximum(n_used - 1, 0)])
    pos = jnp.zeros((n_assign,), jnp.int32).at[order].set(dest).reshape(n_tok, TOP_K)
    return blk_expert, blk_cnt, slot_tok, slot_w, pos


def kernel(x, norm1_gain, w_in, attn_q_norm_gain, attn_k_norm_gain, mlstm_gate_bias,
           mlstm_head_norm_gain, branch_gate_bias, w_attn_branch, w_mlstm_branch, w_out,
           norm2_gain, w_router_group, b_router_group, w_router_expert, b_router_expert,
           w_expert_gate, w_expert_up, w_expert_down):
    bsz, s_len, d = x.shape
    assert bsz == 1
    depth = w_in.shape[0]
    xs = x.reshape(s_len, d)
    w_in_t = jnp.swapaxes(w_in, 1, 2)
    n_proj = OFF_GA + 2 * d
    tn = 512

    def proj_rows(j):
        return j * tn + jnp.where(j * tn >= OFF_GA, N_ML_GATES, 0)

    for l in range(depth):
        h = _rmsnorm(xs, norm1_gain[l])
        proj = _matmul_nt(h, w_in_t, l, proj_rows, n_proj, BF16, 1024, tn, "in_proj")
        gates = _matmul_nt(h, w_in_t, l, lambda j: OFF_MG + j * LANES, LANES, F32, 1024, LANES,
                           "gate_proj")[:, :N_ML_GATES]

        att = _attention(proj, attn_q_norm_gain[l], attn_k_norm_gain[l])

        hs = _mlstm(proj, gates, mlstm_gate_bias[l])
        mem = _mlstm_out(hs, proj, mlstm_head_norm_gain[l])

        merged = _branch_mix(att, mem, w_attn_branch, w_mlstm_branch, l, proj, branch_gate_bias)
        x1 = _matmul_residual(merged, w_out, l, xs, 1024, 512)

        w_r = jnp.concatenate([w_router_group[l], w_router_expert[l]], axis=1)
        b_r = jnp.concatenate([b_router_group[l], b_router_expert[l]])
        n_r = w_r.shape[1]
        w_r = jnp.pad(w_r, ((0, 0), (0, LANES - n_r)))
        b_r = jnp.pad(b_r, (0, LANES - n_r)).reshape(1, LANES)
        h2, logits = _norm_router(x1, norm2_gain[l], w_r, b_r)

        blk_expert, blk_cnt, slot_tok, slot_w, pos = _route(logits, s_len)
        ys = _moe_experts(h2, blk_expert, blk_cnt, slot_tok, slot_w,
                          w_expert_gate, w_expert_up, w_expert_down, l)
        xs = _moe_combine(x1, ys, pos)
    return xs.reshape(bsz, s_len, d)
```

```python
import functools

import jax
import jax.numpy as jnp
from jax import lax
from jax.experimental import pallas as pl
from jax.experimental.pallas import tpu as pltpu

F32 = jnp.float32
BF16 = jnp.bfloat16

EPS = 1e-6
NEG = -1e30
LANES = 128

ATT_PATTERNS = ((128, 1), (512, 4), (2048, 16))
N_ATT_GROUPS = 3
ATT_HEADS = 8
ATT_HEAD_DIM = 128
ATT_GROUP_W = ATT_HEADS * ATT_HEAD_DIM
ATT_W = N_ATT_GROUPS * ATT_GROUP_W
ROPE_THETA = 10000.0
ATT_Q_TILE = 128
ATT_TILE_UNROLL = 4

ML_HEADS = 8
ML_QK_DIM = 256
ML_V_DIM = 512
ML_QK_W = ML_HEADS * ML_QK_DIM
ML_V_W = ML_HEADS * ML_V_DIM
N_ML_GATES = 4 * ML_HEADS
GATE_SOFTCAP = 15.0
ML_TILE = 256

N_GROUPS = 8
EXPERTS_PER_GROUP = 8
N_EXPERTS = N_GROUPS * EXPERTS_PER_GROUP
TOP_K = 2
MOE_ROWS = 512
MOE_SUB_ROWS = 128
MOE_FF_TILE = 256
MOE_OUT_TILE = 1024
COMBINE_ROWS = 256
DMA_ISSUE_UNROLL = 8

OFF_AQ = 0
OFF_AK = OFF_AQ + ATT_W
OFF_AV = OFF_AK + ATT_W
OFF_MQ = OFF_AV + ATT_W
OFF_MK = OFF_MQ + ML_QK_W
OFF_MV = OFF_MK + ML_QK_W
OFF_MO = OFF_MV + ML_V_W
OFF_MG = OFF_MO + ML_V_W
OFF_GA = OFF_MG

VMEM_LIMIT = 56 * 1024 * 1024


def _cparams(sem, vmem=VMEM_LIMIT):
    return pltpu.CompilerParams(dimension_semantics=sem, vmem_limit_bytes=vmem)


def _rmsnorm_kernel(x_ref, g_ref, o_ref):
    x = x_ref[...]
    ms = jnp.mean(x * x, axis=-1, keepdims=True)
    o_ref[...] = (x * lax.rsqrt(ms + EPS) * g_ref[...]).astype(o_ref.dtype)


def _rmsnorm(x, gain, tm=256):
    m, d = x.shape
    return pl.pallas_call(
        _rmsnorm_kernel,
        grid=(m // tm,),
        in_specs=[pl.BlockSpec((tm, d), lambda i: (i, 0)),
                  pl.BlockSpec((1, d), lambda i: (0, 0))],
        out_specs=pl.BlockSpec((tm, d), lambda i: (i, 0)),
        out_shape=jax.ShapeDtypeStruct((m, d), BF16),
        compiler_params=_cparams(("parallel",)),
        name="rmsnorm",
    )(x, gain.reshape(1, d))


def _mm_kernel(a_ref, w_ref, o_ref, wb_ref):
    @pl.when(pl.program_id(1) == 0)
    def _():
        wb_ref[...] = w_ref[0].astype(BF16)

    o_ref[...] = jnp.dot(a_ref[...], wb_ref[...], preferred_element_type=F32).astype(o_ref.dtype)


def _matmul(a, w, layer, col0, n, out_dtype, tm, tn, name):
    m, k = a.shape
    cb = col0 // tn
    return pl.pallas_call(
        _mm_kernel,
        grid=(n // tn, m // tm),
        in_specs=[pl.BlockSpec((tm, k), lambda j, i: (i, 0)),
                  pl.BlockSpec((1, k, tn), lambda j, i: (layer, 0, cb + j))],
        out_specs=pl.BlockSpec((tm, tn), lambda j, i: (i, j)),
        out_shape=jax.ShapeDtypeStruct((m, n), out_dtype),
        scratch_shapes=[pltpu.VMEM((k, tn), BF16)],
        compiler_params=_cparams(("parallel", "arbitrary")),
        name=name,
    )(a, w)


def _mm_nt_kernel(a_ref, wt_ref, o_ref, wb_ref):
    @pl.when(pl.program_id(1) == 0)
    def _():
        wb_ref[...] = wt_ref[0].astype(BF16)

    o_ref[...] = lax.dot_general(a_ref[...], wb_ref[...], (((1,), (1,)), ((), ())),
                                 preferred_element_type=F32).astype(o_ref.dtype)


def _matmul_nt(a, w_t, layer, row_of_tile, n, out_dtype, tm, tn, name):
    m, k = a.shape
    return pl.pallas_call(
        _mm_nt_kernel,
        grid=(n // tn, m // tm),
        in_specs=[pl.BlockSpec((tm, k), lambda j, i: (i, 0)),
                  pl.BlockSpec((pl.Element(1), pl.Element(tn), pl.Element(k)),
                               lambda j, i: (layer, pl.multiple_of(row_of_tile(j), 8), 0))],
        out_specs=pl.BlockSpec((tm, tn), lambda j, i: (i, j)),
        out_shape=jax.ShapeDtypeStruct((m, n), out_dtype),
        scratch_shapes=[pltpu.VMEM((tn, k), BF16)],
        compiler_params=_cparams(("parallel", "arbitrary")),
        name=name,
    )(a, w_t)


def _mm_res_kernel(a_ref, w_ref, r_ref, o_ref, wb_ref):
    @pl.when(pl.program_id(1) == 0)
    def _():
        wb_ref[...] = w_ref[0].astype(BF16)

    o_ref[...] = r_ref[...] + jnp.dot(a_ref[...], wb_ref[...], preferred_element_type=F32)


def _matmul_residual(a, w, layer, res, tm, tn):
    m, k = a.shape
    n = w.shape[2]
    return pl.pallas_call(
        _mm_res_kernel,
        grid=(n // tn, m // tm),
        in_specs=[pl.BlockSpec((tm, k), lambda j, i: (i, 0)),
                  pl.BlockSpec((1, k, tn), lambda j, i: (layer, 0, j)),
                  pl.BlockSpec((tm, tn), lambda j, i: (i, j))],
        out_specs=pl.BlockSpec((tm, tn), lambda j, i: (i, j)),
        out_shape=jax.ShapeDtypeStruct((m, n), F32),
        scratch_shapes=[pltpu.VMEM((k, tn), BF16)],
        compiler_params=_cparams(("parallel", "arbitrary")),
        name="out_proj",
    )(a, w, res)


def _attn_kernel(q_ref, k_ref, v_ref, tab_ref, qg_ref, kg_ref, o_ref,
                 qn_ref, kn_ref, vf_ref, acc_ref, lse_ref, *, s_len):
    grp = pl.program_id(1)
    tq = ATT_Q_TILE
    half = ATT_HEAD_DIM // 2
    scale = ATT_HEAD_DIM ** -0.5
    low = lax.broadcasted_iota(jnp.int32, (tq, ATT_HEAD_DIM), 1) < half

    def prep(i, carry):
        rows = pl.ds(pl.multiple_of(i * tq, tq), tq)
        tab = tab_ref[rows, :]
        swapped = pltpu.roll(tab, half, axis=1)
        c = jnp.where(low, tab, swapped)
        s = jnp.where(low, -swapped, tab)
        for src, g_ref, dst, mul in ((q_ref, qg_ref, qn_ref, scale), (k_ref, kg_ref, kn_ref, 1.0)):
            xf = src[rows, :].astype(F32)
            y = xf * lax.rsqrt(jnp.mean(xf * xf, axis=-1, keepdims=True) + EPS) * g_ref[0]
            dst[rows, :] = (y * c + pltpu.roll(y, half, axis=1) * s) * mul
        vf_ref[rows, :] = v_ref[rows, :].astype(F32)
        return carry

    lax.fori_loop(0, s_len // tq, prep, 0, unroll=2)

    def run_group(window, dilation, first):
        radius = window // (2 * dilation)
        sub_len = s_len // dilation
        win = tq + 2 * radius
        tiles_per_class = sub_len // tq
        rel = (lax.broadcasted_iota(jnp.int32, (tq, win), 1)
               - lax.broadcasted_iota(jnp.int32, (tq, win), 0))

        def rows_of(start, size):
            if dilation == 1:
                return pl.ds(start, size)
            return pl.ds(start, size, stride=dilation)

        def tile(idx, carry):
            r = idx // tiles_per_class
            q0 = (idx % tiles_per_class) * tq
            k0 = jnp.clip(q0 - radius, 0, sub_len - win)
            q_rows = rows_of(r + dilation * q0, tq)
            k_rows = rows_of(r + dilation * k0, win)
            q = qn_ref[q_rows, :].astype(BF16)
            k = kn_ref[k_rows, :].astype(BF16)
            v = vf_ref[k_rows, :].astype(BF16)
            s = lax.dot_general(q, k, (((1,), (1,)), ((), ())), preferred_element_type=F32)
            s = jnp.where(jnp.abs(rel + (k0 - q0)) <= radius, s, NEG)
            m = jnp.max(s, axis=-1, keepdims=True)
            p = jnp.exp(s - m)
            den = jnp.sum(p, axis=-1, keepdims=True)
            o = jnp.dot(p.astype(BF16), v, preferred_element_type=F32) / den
            lse = jnp.broadcast_to(m + jnp.log(den), (tq, ATT_HEAD_DIM))
            if first:
                acc_ref[q_rows, :] = o
                lse_ref[q_rows, :] = lse
            else:
                old_l = lse_ref[q_rows, :]
                top = jnp.maximum(old_l, lse)
                a = jnp.exp(old_l - top)
                b = jnp.exp(lse - top)
                tot = a + b
                acc_ref[q_rows, :] = (a * acc_ref[q_rows, :] + b * o) / tot
                lse_ref[q_rows, :] = top + jnp.log(tot)
            return carry

        lax.fori_loop(0, s_len // tq, tile, 0, unroll=ATT_TILE_UNROLL)

    for gi, (window, dilation) in enumerate(ATT_PATTERNS):
        @pl.when(grp == gi)
        def _(window=window, dilation=dilation, gi=gi):
            run_group(window, dilation, gi == 0)

    @pl.when(grp == N_ATT_GROUPS - 1)
    def _():
        o_ref[...] = acc_ref[...].astype(o_ref.dtype)


def _attention(proj, q_gain, k_gain):
    s_len = proj.shape[0]
    half = ATT_HEAD_DIM // 2
    inv_freq = ROPE_THETA ** (-jnp.arange(half, dtype=F32) / half)
    ang = jnp.arange(s_len).astype(F32)[:, None] * inv_freq[None, :]
    table = jnp.concatenate([jnp.cos(ang), jnp.sin(ang)], axis=1)
    qb, kb, vb = OFF_AQ // LANES, OFF_AK // LANES, OFF_AV // LANES
    blk = (s_len, ATT_HEAD_DIM)
    gain_spec = pl.BlockSpec((1, 1, ATT_HEAD_DIM), lambda h, g: (g, 0, 0))
    return pl.pallas_call(
        functools.partial(_attn_kernel, s_len=s_len),
        grid=(ATT_HEADS, N_ATT_GROUPS),
        in_specs=[pl.BlockSpec(blk, lambda h, g: (0, qb + g * ATT_HEADS + h)),
                  pl.BlockSpec(blk, lambda h, g: (0, kb + g * ATT_HEADS + h)),
                  pl.BlockSpec(blk, lambda h, g: (0, vb + g * ATT_HEADS + h)),
                  pl.BlockSpec(blk, lambda h, g: (0, 0)),
                  gain_spec, gain_spec],
        out_specs=pl.BlockSpec(blk, lambda h, g: (0, h)),
        out_shape=jax.ShapeDtypeStruct((s_len, ATT_GROUP_W), BF16),
        scratch_shapes=[pltpu.VMEM(blk, F32)] * 5,
        compiler_params=_cparams(("parallel", "arbitrary")),
        name="dilated_attn",
    )(proj, proj, proj, table,
      q_gain.reshape(N_ATT_GROUPS, 1, ATT_HEAD_DIM), k_gain.reshape(N_ATT_GROUPS, 1, ATT_HEAD_DIM))


def _mlstm_kernel(q_ref, k_ref, v_ref, g_ref, b_ref, o_ref, c_ref, n_ref, m_ref):
    lc = ML_TILE
    dh = pl.program_id(0)
    step = pl.program_id(1)
    is_bw = dh >= ML_HEADS
    sgn = jnp.where(is_bw, -1, 1)

    @pl.when(step == 0)
    def _():
        c_ref[...] = jnp.zeros_like(c_ref)
        n_ref[...] = jnp.zeros_like(n_ref)
        m_ref[...] = jnp.zeros_like(m_ref)

    g = g_ref[0, 0] + b_ref[0]
    g = GATE_SOFTCAP * jnp.tanh(g / GATE_SOFTCAP)
    ig_row = g[0:1, :]
    f = g[1:2, :]
    lf_row = jnp.minimum(f, 0.0) - jnp.log1p(jnp.exp(-jnp.abs(f)))

    t_idx = lax.broadcasted_iota(jnp.int32, (lc, lc), 0)
    s_idx = lax.broadcasted_iota(jnp.int32, (lc, lc), 1)
    valid = (s_idx - t_idx) * sgn <= 0
    eye = s_idx == t_idx
    b_col = jnp.sum(jnp.where(valid, lf_row, 0.0), axis=1, keepdims=True)
    b_row = jnp.sum(jnp.where(eye, b_col, 0.0), axis=0, keepdims=True)
    ig_col = jnp.sum(jnp.where(eye, ig_row, 0.0), axis=1, keepdims=True)

    log_d = jnp.where(valid, b_col - b_row + ig_row, NEG)
    m_prev = m_ref[...]
    m_inter = b_col + m_prev
    m_t = jnp.maximum(m_inter, jnp.max(log_d, axis=1, keepdims=True))
    decay_q = jnp.exp(m_inter - m_t)
    d_mat = jnp.exp(log_d - m_t)

    q = q_ref[...]
    k = k_ref[...] * (ML_QK_DIM ** -0.5)
    v = v_ref[...]
    s = lax.dot_general(q, k, (((1,), (1,)), ((), ())), preferred_element_type=F32) * d_mat
    c_old = c_ref[...]
    num = jnp.dot(s.astype(BF16), v, preferred_element_type=F32)
    num = num + decay_q * jnp.dot(q, c_old.astype(BF16), preferred_element_type=F32)
    qn = jnp.sum(q.astype(F32) * n_ref[...], axis=1, keepdims=True)
    den = jnp.sum(s, axis=1, keepdims=True) + decay_q * qn
    o_ref[0] = num / jnp.maximum(jnp.abs(den), jnp.exp(-m_t))

    last = jnp.where(is_bw, 0, lc - 1)
    sel = lax.broadcasted_iota(jnp.int32, (lc, 1), 0) == last
    m_last = jnp.max(jnp.where(sel, m_t, -jnp.inf), axis=0, keepdims=True)
    b_last = jnp.sum(jnp.where(sel, b_col, 0.0), axis=0, keepdims=True)
    w_col = jnp.exp(b_last - b_col + ig_col - m_last)
    decay_c = jnp.exp(b_last + m_prev - m_last)
    kw = k.astype(F32) * w_col
    c_ref[...] = decay_c * c_old + lax.dot_general(
        kw.astype(BF16), v, (((0,), (0,)), ((), ())), preferred_element_type=F32)
    n_ref[...] = decay_c * n_ref[...] + jnp.sum(kw, axis=0, keepdims=True)
    m_ref[...] = m_last


def _mlstm(proj, gates, gate_bias):
    s_len = proj.shape[0]
    lc = ML_TILE
    nt = s_len // lc
    gt = gates.reshape(s_len, 2, 2, ML_HEADS).transpose(1, 3, 2, 0)
    gt = gt.reshape(2 * ML_HEADS, 2, nt, lc).transpose(0, 2, 1, 3)
    bias = gate_bias.reshape(2, 2, ML_HEADS).transpose(0, 2, 1).reshape(2 * ML_HEADS, 2, 1)

    def tile_of(dh, t):
        return jnp.where(dh >= ML_HEADS, nt - 1 - t, t)

    qb, kb = OFF_MQ // ML_QK_DIM, OFF_MK // ML_QK_DIM
    vb = OFF_MV // ML_V_DIM
    return pl.pallas_call(
        _mlstm_kernel,
        grid=(2 * ML_HEADS, nt),
        in_specs=[pl.BlockSpec((lc, ML_QK_DIM), lambda dh, t: (tile_of(dh, t), qb + dh % ML_HEADS)),
                  pl.BlockSpec((lc, ML_QK_DIM), lambda dh, t: (tile_of(dh, t), kb + dh % ML_HEADS)),
                  pl.BlockSpec((lc, ML_V_DIM), lambda dh, t: (tile_of(dh, t), vb + dh % ML_HEADS)),
                  pl.BlockSpec((1, 1, 2, lc), lambda dh, t: (dh, tile_of(dh, t), 0, 0)),
                  pl.BlockSpec((1, 2, 1), lambda dh, t: (dh, 0, 0))],
        out_specs=pl.BlockSpec((1, lc, ML_V_DIM),
                               lambda dh, t: (dh // ML_HEADS, tile_of(dh, t), dh % ML_HEADS)),
        out_shape=jax.ShapeDtypeStruct((2, s_len, ML_V_W), F32),
        scratch_shapes=[pltpu.VMEM((ML_QK_DIM, ML_V_DIM), F32),
                        pltpu.VMEM((1, ML_QK_DIM), F32),
                        pltpu.VMEM((1, 1), F32)],
        compiler_params=_cparams(("parallel", "arbitrary")),
        name="mlstm_scan",
    )(proj, proj, proj, gt, bias)


def _mlstm_out_kernel(h_ref, mo_ref, g_ref, o_ref):
    h = h_ref[0] + h_ref[1]
    y = h * lax.rsqrt(jnp.mean(h * h, axis=-1, keepdims=True) + EPS) * g_ref[0]
    o_ref[...] = (y * jax.nn.sigmoid(mo_ref[...].astype(F32))).astype(o_ref.dtype)


def _mlstm_out(hs, proj, head_gain, tm=512):
    s_len = proj.shape[0]
    ob = OFF_MO // ML_V_DIM
    return pl.pallas_call(
        _mlstm_out_kernel,
        grid=(s_len // tm, ML_HEADS),
        in_specs=[pl.BlockSpec((2, tm, ML_V_DIM), lambda i, h: (0, i, h)),
                  pl.BlockSpec((tm, ML_V_DIM), lambda i, h: (i, ob + h)),
                  pl.BlockSpec((1, 1, ML_V_DIM), lambda i, h: (h, 0, 0))],
        out_specs=pl.BlockSpec((tm, ML_V_DIM), lambda i, h: (i, h)),
        out_shape=jax.ShapeDtypeStruct((s_len, ML_V_W), BF16),
        compiler_params=_cparams(("parallel", "parallel")),
        name="mlstm_out",
    )(hs, proj, head_gain.reshape(ML_HEADS, 1, ML_V_DIM))


def _branch_kernel(att_ref, mem_ref, wa_ref, wm_ref, ga_ref, gm_ref, ba_ref, bm_ref, o_ref,
                   wab_ref, wmb_ref):
    @pl.when(pl.program_id(1) == 0)
    def _():
        wab_ref[...] = wa_ref[0].astype(BF16)
        wmb_ref[...] = wm_ref[0].astype(BF16)

    ya = jnp.dot(att_ref[...], wab_ref[...], preferred_element_type=F32)
    ym = jnp.dot(mem_ref[...], wmb_ref[...], preferred_element_type=F32)
    g_a = jax.nn.sigmoid(ga_ref[...].astype(F32) + ba_ref[0])
    g_m = jax.nn.sigmoid(gm_ref[...].astype(F32) + bm_ref[0])
    o_ref[...] = (g_a * ya + g_m * ym).astype(o_ref.dtype)


def _branch_mix(att, mem, w_a, w_m, layer, proj, gate_bias, tm=512, tn=512):
    m = att.shape[0]
    ka, km, d = w_a.shape[1], w_m.shape[1], w_a.shape[2]
    bias = gate_bias.reshape(-1, 1, d)
    gab = OFF_GA // tn
    gmb = (OFF_GA + d) // tn
    return pl.pallas_call(
        _branch_kernel,
        grid=(d // tn, m // tm),
        in_specs=[pl.BlockSpec((tm, ka), lambda j, i: (i, 0)),
                  pl.BlockSpec((tm, km), lambda j, i: (i, 0)),
                  pl.BlockSpec((1, ka, tn), lambda j, i: (layer, 0, j)),
                  pl.BlockSpec((1, km, tn), lambda j, i: (layer, 0, j)),
                  pl.BlockSpec((tm, tn), lambda j, i: (i, gab + j)),
                  pl.BlockSpec((tm, tn), lambda j, i: (i, gmb + j)),
                  pl.BlockSpec((1, 1, tn), lambda j, i: (2 * layer, 0, j)),
                  pl.BlockSpec((1, 1, tn), lambda j, i: (2 * layer + 1, 0, j))],
        out_specs=pl.BlockSpec((tm, tn), lambda j, i: (i, j)),
        out_shape=jax.ShapeDtypeStruct((m, d), BF16),
        scratch_shapes=[pltpu.VMEM((ka, tn), BF16), pltpu.VMEM((km, tn), BF16)],
        compiler_params=_cparams(("parallel", "arbitrary")),
        name="branch_mix",
    )(att, mem, w_a, w_m, proj, proj, bias, bias)


def _pack_bf16_pair(lo, hi):
    lo_bits = lax.bitcast_convert_type(lo.astype(BF16).astype(F32), jnp.uint32)
    hi_bits = lax.bitcast_convert_type(hi.astype(BF16).astype(F32), jnp.uint32)
    return (lo_bits >> 16) | (hi_bits & jnp.uint32(0xFFFF0000))


def _unpack_bf16_pair(word):
    lo = lax.bitcast_convert_type(word << 16, F32)
    hi = lax.bitcast_convert_type(word & jnp.uint32(0xFFFF0000), F32)
    return lo, hi


def _norm_router_kernel(x_ref, g_ref, wr_ref, br_ref, h_ref, lg_ref):
    x = x_ref[...]
    h = x * lax.rsqrt(jnp.mean(x * x, axis=-1, keepdims=True) + EPS) * g_ref[...]
    half = h.shape[1] // 2
    h_ref[...] = _pack_bf16_pair(h[:, :half], h[:, half:])
    lg_ref[...] = jnp.dot(h, wr_ref[...], preferred_element_type=F32,
                          precision=lax.Precision.HIGHEST) + br_ref[...]


def _norm_router(x, gain, w_r, b_r, tm=256):
    m, d = x.shape
    n = w_r.shape[1]
    return pl.pallas_call(
        _norm_router_kernel,
        grid=(m // tm,),
        in_specs=[pl.BlockSpec((tm, d), lambda i: (i, 0)),
                  pl.BlockSpec((1, d), lambda i: (0, 0)),
                  pl.BlockSpec((d, n), lambda i: (0, 0)),
                  pl.BlockSpec((1, n), lambda i: (0, 0))],
        out_specs=[pl.BlockSpec((tm, d // 2), lambda i: (i, 0)),
                   pl.BlockSpec((tm, n), lambda i: (i, 0))],
        out_shape=[jax.ShapeDtypeStruct((m, d // 2), jnp.uint32),
                   jax.ShapeDtypeStruct((m, n), F32)],
        compiler_params=_cparams(("parallel",)),
        name="norm_router",
    )(x, gain.reshape(1, d), w_r, b_r)


def _moe_kernel(blk_e_ref, cnt_ref, tok_ref, tok_next_ref, h_hbm, wg_ref, wu_ref,
                wd_lo_ref, wd_hi_ref, y_ref, xg_ref, xb_ref, act_ref, sem, *, n_ff_tiles, n_blk):
    b = pl.program_id(0)
    j = pl.program_id(1)
    rows = MOE_ROWS
    tf = MOE_FF_TILE
    grp = DMA_ISSUE_UNROLL
    cnt = cnt_ref[b]
    used = cnt > 0
    up_phase = j < n_ff_tiles
    n_sub = (cnt + MOE_SUB_ROWS - 1) // MOE_SUB_ROWS

    def row_copy(i, tok):
        return pltpu.make_async_copy(h_hbm.at[pl.ds(tok, 1)], xg_ref.at[pl.ds(i, 1)], sem)

    def issue_rows(toks_ref, count):
        def body(gi, carry):
            for r in range(grp):
                i = gi * grp + r
                row_copy(i, toks_ref[0, 0, i]).start()
            return carry
        lax.fori_loop(0, (count + grp - 1) // grp, body, 0)

    def wait_rows(count):
        def body(gi, carry):
            for r in range(grp):
                row_copy(gi * grp + r, 0).wait()
            return carry
        lax.fori_loop(0, (count + grp - 1) // grp, body, 0)

    @pl.when(jnp.logical_and(used, j == 0))
    def _():
        @pl.when(b == 0)
        def _():
            xg_ref[...] = jnp.zeros_like(xg_ref)
            issue_rows(tok_ref, cnt)

        wait_rows(cnt)
        half = xg_ref.shape[1]
        lo, hi = _unpack_bf16_pair(xg_ref[...])
        xb_ref[:, 0:half] = lo.astype(BF16)
        xb_ref[:, half:2 * half] = hi.astype(BF16)
        nxt = jnp.minimum(b + 1, n_blk - 1)
        issue_rows(tok_next_ref, jnp.where(b + 1 < n_blk, cnt_ref[nxt], 0))

    for m in range(MOE_SUB_ROWS, rows + 1, MOE_SUB_ROWS):
        live = jnp.logical_and(used, n_sub == m // MOE_SUB_ROWS)

        @pl.when(jnp.logical_and(live, up_phase))
        def _(m=m):
            x = xb_ref[0:m, :]
            g = jnp.dot(x, wg_ref[0, 0].astype(BF16), preferred_element_type=F32)
            u = jnp.dot(x, wu_ref[0, 0].astype(BF16), preferred_element_type=F32)
            act_ref[j, 0:m, :] = (g * jax.nn.sigmoid(g) * u).astype(BF16)

        @pl.when(jnp.logical_and(live, jnp.logical_not(up_phase)))
        def _(m=m):
            def down(wd_ref):
                acc = jnp.dot(act_ref[0, 0:m, :], wd_ref[0, 0, 0:tf, :].astype(BF16),
                              preferred_element_type=F32)
                for t in range(1, n_ff_tiles):
                    acc = acc + jnp.dot(act_ref[t, 0:m, :],
                                        wd_ref[0, 0, t * tf:(t + 1) * tf, :].astype(BF16),
                                        preferred_element_type=F32)
                return acc

            y_ref[0:m, :] = _pack_bf16_pair(down(wd_lo_ref), down(wd_hi_ref))
            if m < rows:
                y_ref[m:rows, :] = jnp.zeros((rows - m, y_ref.shape[1]), y_ref.dtype)

    @pl.when(jnp.logical_and(jnp.logical_not(used), jnp.logical_not(up_phase)))
    def _():
        y_ref[...] = jnp.zeros_like(y_ref)


def _moe_experts(h2p, blk_expert, blk_cnt, slot_tok, w_gate, w_up, w_down, layer):
    d = 2 * h2p.shape[1]
    d_ff = w_gate.shape[3]
    rows = MOE_ROWS
    n_blk = slot_tok.shape[0] // rows
    tf = MOE_FF_TILE
    tn = MOE_OUT_TILE
    n_f = d_ff // tf
    n_o = d // 2 // tn

    def ff_tile(b, j, cnt):
        return jnp.where(cnt[b] > 0, jnp.minimum(j, n_f - 1), n_f - 1)

    def out_tile(b, j):
        return jnp.clip(j - n_f, 0, n_o - 1)

    def w_out_tile(b, j, cnt):
        return jnp.where(cnt[b] > 0, out_tile(b, j), n_o - 1)

    toks = slot_tok.reshape(n_blk, 1, rows)
    grid_spec = pltpu.PrefetchScalarGridSpec(
        num_scalar_prefetch=2,
        grid=(n_blk, n_f + n_o),
        in_specs=[pl.BlockSpec((1, 1, rows), lambda b, j, be, cnt: (b, 0, 0),
                               memory_space=pltpu.SMEM),
                  pl.BlockSpec((1, 1, rows),
                               lambda b, j, be, cnt: (jnp.minimum(b + 1, n_blk - 1), 0, 0),
                               memory_space=pltpu.SMEM),
                  pl.BlockSpec(memory_space=pl.ANY),
                  pl.BlockSpec((1, 1, d, tf),
                               lambda b, j, be, cnt: (layer, be[b], 0, ff_tile(b, j, cnt))),
                  pl.BlockSpec((1, 1, d, tf),
                               lambda b, j, be, cnt: (layer, be[b], 0, ff_tile(b, j, cnt))),
                  pl.BlockSpec((1, 1, d_ff, tn),
                               lambda b, j, be, cnt: (layer, be[b], 0, w_out_tile(b, j, cnt))),
                  pl.BlockSpec((1, 1, d_ff, tn),
                               lambda b, j, be, cnt: (layer, be[b], 0, n_o + w_out_tile(b, j, cnt)))],
        out_specs=pl.BlockSpec((rows, tn), lambda b, j, be, cnt: (b, out_tile(b, j))),
        scratch_shapes=[pltpu.VMEM((rows, d // 2), jnp.uint32),
                        pltpu.VMEM((rows, d), BF16),
                        pltpu.VMEM((n_f, rows, tf), BF16),
                        pltpu.SemaphoreType.DMA(())],
    )
    return pl.pallas_call(
        functools.partial(_moe_kernel, n_ff_tiles=n_f, n_blk=n_blk),
        grid_spec=grid_spec,
        out_shape=jax.ShapeDtypeStruct((n_blk * rows, d // 2), jnp.uint32),
        compiler_params=_cparams(("arbitrary", "arbitrary")),
        name="moe_experts",
    )(blk_expert, blk_cnt, toks, toks, h2p, w_gate, w_up, w_down, w_down)


def _combine_kernel(pos_ref, pos_next_ref, x_ref, w_ref, y_hbm, o_ref, buf_ref, sem, *, n_steps):
    rows = COMBINE_ROWS
    n_copy = TOP_K * rows
    step = pl.program_id(0)
    slot = step % 2

    def row_copy(s, i, src):
        return pltpu.make_async_copy(y_hbm.at[pl.ds(src, 1)], buf_ref.at[s, pl.ds(i, 1)], sem.at[s])

    def issue(s, idx_ref):
        def body(i, carry):
            row_copy(s, i, idx_ref[0, 0, i]).start()
            return carry
        lax.fori_loop(0, n_copy, body, 0, unroll=DMA_ISSUE_UNROLL)

    @pl.when(step == 0)
    def _():
        issue(0, pos_ref)

    @pl.when(step + 1 < n_steps)
    def _():
        issue(1 - slot, pos_next_ref)

    def drain(i, carry):
        row_copy(slot, i, 0).wait()
        return carry

    lax.fori_loop(0, n_copy, drain, 0, unroll=DMA_ISSUE_UNROLL)
    half = x_ref.shape[1] // 2
    lo0, hi0 = _unpack_bf16_pair(buf_ref[slot, pl.ds(0, rows), :])
    lo1, hi1 = _unpack_bf16_pair(buf_ref[slot, pl.ds(rows, rows), :])
    w0 = w_ref[:, 0:1]
    w1 = w_ref[:, 1:2]
    o_ref[:, 0:half] = x_ref[:, 0:half] + (w0 * lo0 + w1 * lo1)
    o_ref[:, half:2 * half] = x_ref[:, half:2 * half] + (w0 * hi0 + w1 * hi1)


def _moe_combine(x1, ys, pos, weights):
    n_tok, d = x1.shape
    rows = COMBINE_ROWS
    n_blk = n_tok // rows
    pos_b = pos.reshape(n_blk, rows, TOP_K).transpose(0, 2, 1).reshape(n_blk, 1, TOP_K * rows)
    idx_block = (1, 1, TOP_K * rows)
    return pl.pallas_call(
        functools.partial(_combine_kernel, n_steps=n_blk),
        grid=(n_blk,),
        in_specs=[pl.BlockSpec(idx_block, lambda i: (i, 0, 0), memory_space=pltpu.SMEM),
                  pl.BlockSpec(idx_block, lambda i: (jnp.minimum(i + 1, n_blk - 1), 0, 0),
                               memory_space=pltpu.SMEM),
                  pl.BlockSpec((rows, d), lambda i: (i, 0)),
                  pl.BlockSpec((rows, TOP_K), lambda i: (i, 0)),
                  pl.BlockSpec(memory_space=pl.ANY)],
        out_specs=pl.BlockSpec((rows, d), lambda i: (i, 0)),
        out_shape=jax.ShapeDtypeStruct((n_tok, d), F32),
        scratch_shapes=[pltpu.VMEM((2, TOP_K * rows, d // 2), jnp.uint32),
                        pltpu.SemaphoreType.DMA((2,))],
        compiler_params=_cparams(("arbitrary",)),
        name="moe_combine",
    )(pos_b, pos_b, x1, weights, ys)


def _route(logits, n_tok):
    glog = logits[:, :N_GROUPS]
    gprob = jax.nn.softmax(glog, axis=-1)
    g_sel = jnp.argmax(glog, axis=-1)
    p_g = jnp.take_along_axis(gprob, g_sel[:, None], axis=1)[:, 0]
    elog = logits[:, N_GROUPS:N_GROUPS + N_EXPERTS].reshape(n_tok, N_GROUPS, EXPERTS_PER_GROUP)
    elog = jnp.take_along_axis(elog, g_sel[:, None, None], axis=1)[:, 0]
    top_v, top_i = lax.top_k(elog, TOP_K)
    weights = p_g[:, None] * jax.nn.softmax(top_v, axis=-1)
    expert_ids = (g_sel[:, None] * EXPERTS_PER_GROUP + top_i).astype(jnp.int32)

    n_assign = n_tok * TOP_K
    rows = MOE_ROWS
    flat_e = expert_ids.reshape(n_assign)
    a_ids = jnp.arange(n_assign, dtype=jnp.int32)
    skey = jnp.sort(flat_e * n_assign + a_ids)
    order = skey % n_assign
    inv = jnp.argsort(order).astype(jnp.int32)
    experts = jnp.arange(N_EXPERTS, dtype=jnp.int32)
    counts = jnp.sum((flat_e[:, None] == experts[None, :]).astype(jnp.int32), axis=0)
    blocks = (counts + rows - 1) // rows
    pad_end = jnp.cumsum(blocks * rows)
    pad_start = pad_end - blocks * rows
    start = jnp.cumsum(counts) - counts
    n_slots = n_assign + N_EXPERTS * rows
    n_blk = n_slots // rows
    n_used = jnp.sum(blocks).astype(jnp.int32)
    blk_ids = jnp.arange(n_blk, dtype=jnp.int32)
    blk_expert = jnp.minimum(
        jnp.sum((pad_end[None, :] <= (blk_ids * rows)[:, None]).astype(jnp.int32), axis=1),
        N_EXPERTS - 1)
    blk_cnt = jnp.clip(counts[blk_expert] - (blk_ids * rows - pad_start[blk_expert]), 0, rows)
    blk_cnt = jnp.where(blk_ids < n_used, blk_cnt, 0).astype(jnp.int32)
    slots = jnp.arange(n_slots, dtype=jnp.int32)
    e_s = blk_expert[slots // rows]
    off = slots - pad_start[e_s]
    real = jnp.logical_and(off < counts[e_s], slots // rows < n_used)
    tok_sorted = order // TOP_K
    slot_tok = jnp.where(real, tok_sorted[jnp.clip(start[e_s] + off, 0, n_assign - 1)],
                         slots % n_tok).astype(jnp.int32)
    blk_expert = jnp.where(blk_ids < n_used, blk_expert,
                           blk_expert[jnp.maximum(n_used - 1, 0)]).astype(jnp.int32)
    pos = (pad_start[flat_e] + inv - start[flat_e]).astype(jnp.int32).reshape(n_tok, TOP_K)
    return blk_expert, blk_cnt, slot_tok, pos, weights


def kernel(x, norm1_gain, w_in, attn_q_norm_gain, attn_k_norm_gain, mlstm_gate_bias,
           mlstm_head_norm_gain, branch_gate_bias, w_attn_branch, w_mlstm_branch, w_out,
           norm2_gain, w_router_group, b_router_group, w_router_expert, b_router_expert,
           w_expert_gate, w_expert_up, w_expert_down):
    bsz, s_len, d = x.shape
    assert bsz == 1
    depth = w_in.shape[0]
    xs = x.reshape(s_len, d)
    w_in_t = jnp.swapaxes(w_in, 1, 2)
    n_proj = OFF_GA + 2 * d
    tn = 512

    def proj_rows(j):
        return j * tn + jnp.where(j * tn >= OFF_GA, N_ML_GATES, 0)

    for l in range(depth):
        h = _rmsnorm(xs, norm1_gain[l])
        proj = _matmul_nt(h, w_in_t, l, proj_rows, n_proj, BF16, 1024, tn, "in_proj")
        gates = _matmul_nt(h, w_in_t, l, lambda j: OFF_MG + j * LANES, LANES, F32, 1024, LANES,
                           "gate_proj")[:, :N_ML_GATES]

        att = _attention(proj, attn_q_norm_gain[l], attn_k_norm_gain[l])

        hs = _mlstm(proj, gates, mlstm_gate_bias[l])
        mem = _mlstm_out(hs, proj, mlstm_head_norm_gain[l])

        merged = _branch_mix(att, mem, w_attn_branch, w_mlstm_branch, l, proj, branch_gate_bias)
        x1 = _matmul_residual(merged, w_out, l, xs, 1024, 512)

        w_r = jnp.concatenate([w_router_group[l], w_router_expert[l]], axis=1)
        b_r = jnp.concatenate([b_router_group[l], b_router_expert[l]])
        n_r = w_r.shape[1]
        w_r = jnp.pad(w_r, ((0, 0), (0, LANES - n_r)))
        b_r = jnp.pad(b_r, (0, LANES - n_r)).reshape(1, LANES)
        h2, logits = _norm_router(x1, norm2_gain[l], w_r, b_r)

        blk_expert, blk_cnt, slot_tok, pos, weights = _route(logits, s_len)
        ys = _moe_experts(h2, blk_expert, blk_cnt, slot_tok,
                          w_expert_gate, w_expert_up, w_expert_down, l)
        xs = _moe_combine(x1, ys, pos, weights)
    return xs.reshape(bsz, s_len, d)
```

```python
import functools

import jax
import jax.numpy as jnp
from jax import lax
from jax.experimental import pallas as pl
from jax.experimental.pallas import tpu as pltpu

F32 = jnp.float32
BF16 = jnp.bfloat16

EPS = 1e-6
NEG = -1e30
LANES = 128

ATT_PATTERNS = ((128, 1), (512, 4), (2048, 16))
N_ATT_GROUPS = 3
ATT_HEADS = 8
ATT_HEAD_DIM = 128
ATT_GROUP_W = ATT_HEADS * ATT_HEAD_DIM
ATT_W = N_ATT_GROUPS * ATT_GROUP_W
ROPE_THETA = 10000.0
ATT_Q_TILE = 128
ATT_TILE_UNROLL = 4

ML_HEADS = 8
ML_QK_DIM = 256
ML_V_DIM = 512
ML_QK_W = ML_HEADS * ML_QK_DIM
ML_V_W = ML_HEADS * ML_V_DIM
N_ML_GATES = 4 * ML_HEADS
GATE_SOFTCAP = 15.0
ML_TILE = 256

N_GROUPS = 8
EXPERTS_PER_GROUP = 8
N_EXPERTS = N_GROUPS * EXPERTS_PER_GROUP
TOP_K = 2
MOE_ROWS = 512
MOE_SUB_ROWS = 128
MOE_FF_TILE = 256
MOE_OUT_TILE = 1024
COMBINE_ROWS = 256
DMA_ISSUE_UNROLL = 8

OFF_AQ = 0
OFF_AK = OFF_AQ + ATT_W
OFF_AV = OFF_AK + ATT_W
OFF_MQ = OFF_AV + ATT_W
OFF_MK = OFF_MQ + ML_QK_W
OFF_MV = OFF_MK + ML_QK_W
OFF_MO = OFF_MV + ML_V_W
OFF_MG = OFF_MO + ML_V_W
OFF_GA = OFF_MG

VMEM_LIMIT = 56 * 1024 * 1024


def _cparams(sem, vmem=VMEM_LIMIT):
    return pltpu.CompilerParams(dimension_semantics=sem, vmem_limit_bytes=vmem)


def _rmsnorm_kernel(x_ref, g_ref, o_ref):
    x = x_ref[...]
    ms = jnp.mean(x * x, axis=-1, keepdims=True)
    o_ref[...] = (x * lax.rsqrt(ms + EPS) * g_ref[...]).astype(o_ref.dtype)


def _rmsnorm(x, gain, tm=256):
    m, d = x.shape
    return pl.pallas_call(
        _rmsnorm_kernel,
        grid=(m // tm,),
        in_specs=[pl.BlockSpec((tm, d), lambda i: (i, 0)),
                  pl.BlockSpec((1, d), lambda i: (0, 0))],
        out_specs=pl.BlockSpec((tm, d), lambda i: (i, 0)),
        out_shape=jax.ShapeDtypeStruct((m, d), BF16),
        compiler_params=_cparams(("parallel",)),
        name="rmsnorm",
    )(x, gain.reshape(1, d))


def _mm_kernel(a_ref, w_ref, o_ref, wb_ref):
    @pl.when(pl.program_id(1) == 0)
    def _():
        wb_ref[...] = w_ref[0].astype(BF16)

    o_ref[...] = jnp.dot(a_ref[...], wb_ref[...], preferred_element_type=F32).astype(o_ref.dtype)


def _matmul(a, w, layer, col0, n, out_dtype, tm, tn, name):
    m, k = a.shape
    cb = col0 // tn
    return pl.pallas_call(
        _mm_kernel,
        grid=(n // tn, m // tm),
        in_specs=[pl.BlockSpec((tm, k), lambda j, i: (i, 0)),
                  pl.BlockSpec((1, k, tn), lambda j, i: (layer, 0, cb + j))],
        out_specs=pl.BlockSpec((tm, tn), lambda j, i: (i, j)),
        out_shape=jax.ShapeDtypeStruct((m, n), out_dtype),
        scratch_shapes=[pltpu.VMEM((k, tn), BF16)],
        compiler_params=_cparams(("parallel", "arbitrary")),
        name=name,
    )(a, w)


def _mm_nt_kernel(a_ref, wt_ref, o_ref, wb_ref):
    @pl.when(pl.program_id(1) == 0)
    def _():
        wb_ref[...] = wt_ref[0].astype(BF16)

    o_ref[...] = lax.dot_general(a_ref[...], wb_ref[...], (((1,), (1,)), ((), ())),
                                 preferred_element_type=F32).astype(o_ref.dtype)


def _matmul_nt(a, w_t, layer, row_of_tile, n, out_dtype, tm, tn, name):
    m, k = a.shape
    return pl.pallas_call(
        _mm_nt_kernel,
        grid=(n // tn, m // tm),
        in_specs=[pl.BlockSpec((tm, k), lambda j, i: (i, 0)),
                  pl.BlockSpec((pl.Element(1), pl.Element(tn), pl.Element(k)),
                               lambda j, i: (layer, pl.multiple_of(row_of_tile(j), 8), 0))],
        out_specs=pl.BlockSpec((tm, tn), lambda j, i: (i, j)),
        out_shape=jax.ShapeDtypeStruct((m, n), out_dtype),
        scratch_shapes=[pltpu.VMEM((tn, k), BF16)],
        compiler_params=_cparams(("parallel", "arbitrary")),
        name=name,
    )(a, w_t)


def _mm_res_kernel(a_ref, w_ref, r_ref, o_ref, wb_ref):
    @pl.when(pl.program_id(1) == 0)
    def _():
        wb_ref[...] = w_ref[0].astype(BF16)

    o_ref[...] = r_ref[...] + jnp.dot(a_ref[...], wb_ref[...], preferred_element_type=F32)


def _matmul_residual(a, w, layer, res, tm, tn):
    m, k = a.shape
    n = w.shape[2]
    return pl.pallas_call(
        _mm_res_kernel,
        grid=(n // tn, m // tm),
        in_specs=[pl.BlockSpec((tm, k), lambda j, i: (i, 0)),
                  pl.BlockSpec((1, k, tn), lambda j, i: (layer, 0, j)),
                  pl.BlockSpec((tm, tn), lambda j, i: (i, j))],
        out_specs=pl.BlockSpec((tm, tn), lambda j, i: (i, j)),
        out_shape=jax.ShapeDtypeStruct((m, n), F32),
        scratch_shapes=[pltpu.VMEM((k, tn), BF16)],
        compiler_params=_cparams(("parallel", "arbitrary")),
        name="out_proj",
    )(a, w, res)


def _attn_kernel(q_ref, k_ref, v_ref, tab_ref, qg_ref, kg_ref, o_ref,
                 qn_ref, kn_ref, vf_ref, acc_ref, lse_ref, *, s_len):
    grp = pl.program_id(1)
    tq = ATT_Q_TILE
    half = ATT_HEAD_DIM // 2
    scale = ATT_HEAD_DIM ** -0.5
    low = lax.broadcasted_iota(jnp.int32, (tq, ATT_HEAD_DIM), 1) < half

    def prep(i, carry):
        rows = pl.ds(pl.multiple_of(i * tq, tq), tq)
        tab = tab_ref[rows, :]
        swapped = pltpu.roll(tab, half, axis=1)
        c = jnp.where(low, tab, swapped)
        s = jnp.where(low, -swapped, tab)
        for src, g_ref, dst, mul in ((q_ref, qg_ref, qn_ref, scale), (k_ref, kg_ref, kn_ref, 1.0)):
            xf = src[rows, :].astype(F32)
            y = xf * lax.rsqrt(jnp.mean(xf * xf, axis=-1, keepdims=True) + EPS) * g_ref[0]
            dst[rows, :] = (y * c + pltpu.roll(y, half, axis=1) * s) * mul
        vf_ref[rows, :] = v_ref[rows, :].astype(F32)
        return carry

    lax.fori_loop(0, s_len // tq, prep, 0, unroll=2)

    def run_group(window, dilation, first):
        radius = window // (2 * dilation)
        sub_len = s_len // dilation
        win = tq + 2 * radius
        tiles_per_class = sub_len // tq
        rel = (lax.broadcasted_iota(jnp.int32, (tq, win), 1)
               - lax.broadcasted_iota(jnp.int32, (tq, win), 0))

        def rows_of(start, size):
            if dilation == 1:
                return pl.ds(start, size)
            return pl.ds(start, size, stride=dilation)

        def tile(idx, carry):
            r = idx // tiles_per_class
            q0 = (idx % tiles_per_class) * tq
            k0 = jnp.clip(q0 - radius, 0, sub_len - win)
            q_rows = rows_of(r + dilation * q0, tq)
            k_rows = rows_of(r + dilation * k0, win)
            q = qn_ref[q_rows, :].astype(BF16)
            k = kn_ref[k_rows, :].astype(BF16)
            v = vf_ref[k_rows, :].astype(BF16)
            s = lax.dot_general(q, k, (((1,), (1,)), ((), ())), preferred_element_type=F32)
            s = jnp.where(jnp.abs(rel + (k0 - q0)) <= radius, s, NEG)
            m = jnp.max(s, axis=-1, keepdims=True)
            p = jnp.exp(s - m)
            den = jnp.sum(p, axis=-1, keepdims=True)
            o = jnp.dot(p.astype(BF16), v, preferred_element_type=F32) / den
            lse = jnp.broadcast_to(m + jnp.log(den), (tq, ATT_HEAD_DIM))
            if first:
                acc_ref[q_rows, :] = o
                lse_ref[q_rows, :] = lse
            else:
                old_l = lse_ref[q_rows, :]
                top = jnp.maximum(old_l, lse)
                a = jnp.exp(old_l - top)
                b = jnp.exp(lse - top)
                tot = a + b
                acc_ref[q_rows, :] = (a * acc_ref[q_rows, :] + b * o) / tot
                lse_ref[q_rows, :] = top + jnp.log(tot)
            return carry

        lax.fori_loop(0, s_len // tq, tile, 0, unroll=ATT_TILE_UNROLL)

    for gi, (window, dilation) in enumerate(ATT_PATTERNS):
        @pl.when(grp == gi)
        def _(window=window, dilation=dilation, gi=gi):
            run_group(window, dilation, gi == 0)

    @pl.when(grp == N_ATT_GROUPS - 1)
    def _():
        o_ref[...] = acc_ref[...].astype(o_ref.dtype)


def _attention(proj, q_gain, k_gain):
    s_len = proj.shape[0]
    half = ATT_HEAD_DIM // 2
    inv_freq = ROPE_THETA ** (-jnp.arange(half, dtype=F32) / half)
    ang = jnp.arange(s_len).astype(F32)[:, None] * inv_freq[None, :]
    table = jnp.concatenate([jnp.cos(ang), jnp.sin(ang)], axis=1)
    qb, kb, vb = OFF_AQ // LANES, OFF_AK // LANES, OFF_AV // LANES
    blk = (s_len, ATT_HEAD_DIM)
    gain_spec = pl.BlockSpec((1, 1, ATT_HEAD_DIM), lambda h, g: (g, 0, 0))
    return pl.pallas_call(
        functools.partial(_attn_kernel, s_len=s_len),
        grid=(ATT_HEADS, N_ATT_GROUPS),
        in_specs=[pl.BlockSpec(blk, lambda h, g: (0, qb + g * ATT_HEADS + h)),
                  pl.BlockSpec(blk, lambda h, g: (0, kb + g * ATT_HEADS + h)),
                  pl.BlockSpec(blk, lambda h, g: (0, vb + g * ATT_HEADS + h)),
                  pl.BlockSpec(blk, lambda h, g: (0, 0)),
                  gain_spec, gain_spec],
        out_specs=pl.BlockSpec(blk, lambda h, g: (0, h)),
        out_shape=jax.ShapeDtypeStruct((s_len, ATT_GROUP_W), BF16),
        scratch_shapes=[pltpu.VMEM(blk, F32)] * 5,
        compiler_params=_cparams(("parallel", "arbitrary")),
        name="dilated_attn",
    )(proj, proj, proj, table,
      q_gain.reshape(N_ATT_GROUPS, 1, ATT_HEAD_DIM), k_gain.reshape(N_ATT_GROUPS, 1, ATT_HEAD_DIM))


def _mlstm_kernel(q_ref, k_ref, v_ref, g_ref, b_ref, o_ref, c_ref, n_ref, m_ref):
    lc = ML_TILE
    dh = pl.program_id(0)
    step = pl.program_id(1)
    is_bw = dh >= ML_HEADS
    sgn = jnp.where(is_bw, -1, 1)

    @pl.when(step == 0)
    def _():
        c_ref[...] = jnp.zeros_like(c_ref)
        n_ref[...] = jnp.zeros_like(n_ref)
        m_ref[...] = jnp.zeros_like(m_ref)

    g = g_ref[0, 0] + b_ref[0]
    g = GATE_SOFTCAP * jnp.tanh(g / GATE_SOFTCAP)
    ig_row = g[0:1, :]
    f = g[1:2, :]
    lf_row = jnp.minimum(f, 0.0) - jnp.log1p(jnp.exp(-jnp.abs(f)))

    t_idx = lax.broadcasted_iota(jnp.int32, (lc, lc), 0)
    s_idx = lax.broadcasted_iota(jnp.int32, (lc, lc), 1)
    valid = (s_idx - t_idx) * sgn <= 0
    eye = s_idx == t_idx
    b_col = jnp.sum(jnp.where(valid, lf_row, 0.0), axis=1, keepdims=True)
    b_row = jnp.sum(jnp.where(eye, b_col, 0.0), axis=0, keepdims=True)
    ig_col = jnp.sum(jnp.where(eye, ig_row, 0.0), axis=1, keepdims=True)

    log_d = jnp.where(valid, b_col - b_row + ig_row, NEG)
    m_prev = m_ref[...]
    m_inter = b_col + m_prev
    m_t = jnp.maximum(m_inter, jnp.max(log_d, axis=1, keepdims=True))
    decay_q = jnp.exp(m_inter - m_t)
    d_mat = jnp.exp(log_d - m_t)

    q = q_ref[...]
    k = k_ref[...] * (ML_QK_DIM ** -0.5)
    v = v_ref[...]
    s = lax.dot_general(q, k, (((1,), (1,)), ((), ())), preferred_element_type=F32) * d_mat
    c_old = c_ref[...]
    num = jnp.dot(s.astype(BF16), v, preferred_element_type=F32)
    num = num + decay_q * jnp.dot(q, c_old.astype(BF16), preferred_element_type=F32)
    qn = jnp.sum(q.astype(F32) * n_ref[...], axis=1, keepdims=True)
    den = jnp.sum(s, axis=1, keepdims=True) + decay_q * qn
    o_ref[0] = num / jnp.maximum(jnp.abs(den), jnp.exp(-m_t))

    last = jnp.where(is_bw, 0, lc - 1)
    sel = lax.broadcasted_iota(jnp.int32, (lc, 1), 0) == last
    m_last = jnp.max(jnp.where(sel, m_t, -jnp.inf), axis=0, keepdims=True)
    b_last = jnp.sum(jnp.where(sel, b_col, 0.0), axis=0, keepdims=True)
    w_col = jnp.exp(b_last - b_col + ig_col - m_last)
    decay_c = jnp.exp(b_last + m_prev - m_last)
    kw = k.astype(F32) * w_col
    c_ref[...] = decay_c * c_old + lax.dot_general(
        kw.astype(BF16), v, (((0,), (0,)), ((), ())), preferred_element_type=F32)
    n_ref[...] = decay_c * n_ref[...] + jnp.sum(kw, axis=0, keepdims=True)
    m_ref[...] = m_last


def _mlstm(proj, gates, gate_bias):
    s_len = proj.shape[0]
    lc = ML_TILE
    nt = s_len // lc
    gt = gates.reshape(s_len, 2, 2, ML_HEADS).transpose(1, 3, 2, 0)
    gt = gt.reshape(2 * ML_HEADS, 2, nt, lc).transpose(0, 2, 1, 3)
    bias = gate_bias.reshape(2, 2, ML_HEADS).transpose(0, 2, 1).reshape(2 * ML_HEADS, 2, 1)

    def tile_of(dh, t):
        return jnp.where(dh >= ML_HEADS, nt - 1 - t, t)

    qb, kb = OFF_MQ // ML_QK_DIM, OFF_MK // ML_QK_DIM
    vb = OFF_MV // ML_V_DIM
    return pl.pallas_call(
        _mlstm_kernel,
        grid=(2 * ML_HEADS, nt),
        in_specs=[pl.BlockSpec((lc, ML_QK_DIM), lambda dh, t: (tile_of(dh, t), qb + dh % ML_HEADS)),
                  pl.BlockSpec((lc, ML_QK_DIM), lambda dh, t: (tile_of(dh, t), kb + dh % ML_HEADS)),
                  pl.BlockSpec((lc, ML_V_DIM), lambda dh, t: (tile_of(dh, t), vb + dh % ML_HEADS)),
                  pl.BlockSpec((1, 1, 2, lc), lambda dh, t: (dh, tile_of(dh, t), 0, 0)),
                  pl.BlockSpec((1, 2, 1), lambda dh, t: (dh, 0, 0))],
        out_specs=pl.BlockSpec((1, lc, ML_V_DIM),
                               lambda dh, t: (dh // ML_HEADS, tile_of(dh, t), dh % ML_HEADS)),
        out_shape=jax.ShapeDtypeStruct((2, s_len, ML_V_W), F32),
        scratch_shapes=[pltpu.VMEM((ML_QK_DIM, ML_V_DIM), F32),
                        pltpu.VMEM((1, ML_QK_DIM), F32),
                        pltpu.VMEM((1, 1), F32)],
        compiler_params=_cparams(("parallel", "arbitrary")),
        name="mlstm_scan",
    )(proj, proj, proj, gt, bias)


def _mlstm_out_kernel(h_ref, mo_ref, g_ref, o_ref):
    h = h_ref[0] + h_ref[1]
    y = h * lax.rsqrt(jnp.mean(h * h, axis=-1, keepdims=True) + EPS) * g_ref[0]
    o_ref[...] = (y * jax.nn.sigmoid(mo_ref[...].astype(F32))).astype(o_ref.dtype)


def _mlstm_out(hs, proj, head_gain, tm=512):
    s_len = proj.shape[0]
    ob = OFF_MO // ML_V_DIM
    return pl.pallas_call(
        _mlstm_out_kernel,
        grid=(s_len // tm, ML_HEADS),
        in_specs=[pl.BlockSpec((2, tm, ML_V_DIM), lambda i, h: (0, i, h)),
                  pl.BlockSpec((tm, ML_V_DIM), lambda i, h: (i, ob + h)),
                  pl.BlockSpec((1, 1, ML_V_DIM), lambda i, h: (h, 0, 0))],
        out_specs=pl.BlockSpec((tm, ML_V_DIM), lambda i, h: (i, h)),
        out_shape=jax.ShapeDtypeStruct((s_len, ML_V_W), BF16),
        compiler_params=_cparams(("parallel", "parallel")),
        name="mlstm_out",
    )(hs, proj, head_gain.reshape(ML_HEADS, 1, ML_V_DIM))


def _branch_kernel(att_ref, mem_ref, wa_ref, wm_ref, ga_ref, gm_ref, ba_ref, bm_ref, o_ref,
                   wab_ref, wmb_ref):
    @pl.when(pl.program_id(1) == 0)
    def _():
        wab_ref[...] = wa_ref[0].astype(BF16)
        wmb_ref[...] = wm_ref[0].astype(BF16)

    ya = jnp.dot(att_ref[...], wab_ref[...], preferred_element_type=F32)
    ym = jnp.dot(mem_ref[...], wmb_ref[...], preferred_element_type=F32)
    g_a = jax.nn.sigmoid(ga_ref[...].astype(F32) + ba_ref[0])
    g_m = jax.nn.sigmoid(gm_ref[...].astype(F32) + bm_ref[0])
    o_ref[...] = (g_a * ya + g_m * ym).astype(o_ref.dtype)


def _branch_mix(att, mem, w_a, w_m, layer, proj, gate_bias, tm=512, tn=512):
    m = att.shape[0]
    ka, km, d = w_a.shape[1], w_m.shape[1], w_a.shape[2]
    bias = gate_bias.reshape(-1, 1, d)
    gab = OFF_GA // tn
    gmb = (OFF_GA + d) // tn
    return pl.pallas_call(
        _branch_kernel,
        grid=(d // tn, m // tm),
        in_specs=[pl.BlockSpec((tm, ka), lambda j, i: (i, 0)),
                  pl.BlockSpec((tm, km), lambda j, i: (i, 0)),
                  pl.BlockSpec((1, ka, tn), lambda j, i: (layer, 0, j)),
                  pl.BlockSpec((1, km, tn), lambda j, i: (layer, 0, j)),
                  pl.BlockSpec((tm, tn), lambda j, i: (i, gab + j)),
                  pl.BlockSpec((tm, tn), lambda j, i: (i, gmb + j)),
                  pl.BlockSpec((1, 1, tn), lambda j, i: (2 * layer, 0, j)),
                  pl.BlockSpec((1, 1, tn), lambda j, i: (2 * layer + 1, 0, j))],
        out_specs=pl.BlockSpec((tm, tn), lambda j, i: (i, j)),
        out_shape=jax.ShapeDtypeStruct((m, d), BF16),
        scratch_shapes=[pltpu.VMEM((ka, tn), BF16), pltpu.VMEM((km, tn), BF16)],
        compiler_params=_cparams(("parallel", "arbitrary")),
        name="branch_mix",
    )(att, mem, w_a, w_m, proj, proj, bias, bias)


def _pack_bf16_pair(lo, hi):
    lo_bits = lax.bitcast_convert_type(lo.astype(BF16).astype(F32), jnp.uint32)
    hi_bits = lax.bitcast_convert_type(hi.astype(BF16).astype(F32), jnp.uint32)
    return (lo_bits >> 16) | (hi_bits & jnp.uint32(0xFFFF0000))


def _unpack_bf16_pair(word):
    lo = lax.bitcast_convert_type(word << 16, F32)
    hi = lax.bitcast_convert_type(word & jnp.uint32(0xFFFF0000), F32)
    return lo, hi


def _norm_router_kernel(x_ref, g_ref, wr_ref, br_ref, h_ref, lg_ref):
    x = x_ref[...]
    h = x * lax.rsqrt(jnp.mean(x * x, axis=-1, keepdims=True) + EPS) * g_ref[...]
    half = h.shape[1] // 2
    h_ref[...] = _pack_bf16_pair(h[:, :half], h[:, half:])
    lg_ref[...] = jnp.dot(h, wr_ref[...], preferred_element_type=F32,
                          precision=lax.Precision.HIGHEST) + br_ref[...]


def _norm_router(x, gain, w_r, b_r, tm=256):
    m, d = x.shape
    n = w_r.shape[1]
    return pl.pallas_call(
        _norm_router_kernel,
        grid=(m // tm,),
        in_specs=[pl.BlockSpec((tm, d), lambda i: (i, 0)),
                  pl.BlockSpec((1, d), lambda i: (0, 0)),
                  pl.BlockSpec((d, n), lambda i: (0, 0)),
                  pl.BlockSpec((1, n), lambda i: (0, 0))],
        out_specs=[pl.BlockSpec((tm, d // 2), lambda i: (i, 0)),
                   pl.BlockSpec((tm, n), lambda i: (i, 0))],
        out_shape=[jax.ShapeDtypeStruct((m, d // 2), jnp.uint32),
                   jax.ShapeDtypeStruct((m, n), F32)],
        compiler_params=_cparams(("parallel",)),
        name="norm_router",
    )(x, gain.reshape(1, d), w_r, b_r)


def _moe_kernel(blk_e_ref, cnt_ref, first_ref, tok_ref, h_hbm, wg_ref, wu_ref,
                wd_lo_ref, wd_hi_ref, y_ref, xg_ref, xb_ref, act_ref, sem, *, n_ff_tiles, n_blk):
    b = pl.program_id(0)
    j = pl.program_id(1)
    rows = MOE_ROWS
    tf = MOE_FF_TILE
    grp = DMA_ISSUE_UNROLL
    cnt = cnt_ref[b]
    used = cnt > 0
    up_phase = j < n_ff_tiles
    n_sub = (cnt + MOE_SUB_ROWS - 1) // MOE_SUB_ROWS

    def row_copy(i, tok):
        return pltpu.make_async_copy(h_hbm.at[pl.ds(tok, 1)], xg_ref.at[pl.ds(i, 1)], sem)

    def issue_rows(blk):
        first = first_ref[blk]
        last_entry = tok_ref.shape[0] - 1

        def body(gi, carry):
            for r in range(grp):
                i = gi * grp + r
                row_copy(i, tok_ref[jnp.minimum(first + i, last_entry)]).start()
            return carry
        lax.fori_loop(0, (cnt_ref[blk] + grp - 1) // grp, body, 0)

    def wait_rows(count):
        def body(gi, carry):
            for r in range(grp):
                row_copy(gi * grp + r, 0).wait()
            return carry
        lax.fori_loop(0, (count + grp - 1) // grp, body, 0)

    @pl.when(jnp.logical_and(used, j == 0))
    def _():
        @pl.when(b == 0)
        def _():
            xg_ref[...] = jnp.zeros_like(xg_ref)
            issue_rows(b)

        wait_rows(cnt)
        half = xg_ref.shape[1]
        lo, hi = _unpack_bf16_pair(xg_ref[...])
        xb_ref[:, 0:half] = lo.astype(BF16)
        xb_ref[:, half:2 * half] = hi.astype(BF16)
        @pl.when(b + 1 < n_blk)
        def _():
            issue_rows(b + 1)

    for m in range(MOE_SUB_ROWS, rows + 1, MOE_SUB_ROWS):
        live = jnp.logical_and(used, n_sub == m // MOE_SUB_ROWS)

        @pl.when(jnp.logical_and(live, up_phase))
        def _(m=m):
            x = xb_ref[0:m, :]
            g = jnp.dot(x, wg_ref[0, 0].astype(BF16), preferred_element_type=F32)
            u = jnp.dot(x, wu_ref[0, 0].astype(BF16), preferred_element_type=F32)
            act_ref[j, 0:m, :] = (g * jax.nn.sigmoid(g) * u).astype(BF16)

        @pl.when(jnp.logical_and(live, jnp.logical_not(up_phase)))
        def _(m=m):
            def down(wd_ref):
                acc = jnp.dot(act_ref[0, 0:m, :], wd_ref[0, 0, 0:tf, :].astype(BF16),
                              preferred_element_type=F32)
                for t in range(1, n_ff_tiles):
                    acc = acc + jnp.dot(act_ref[t, 0:m, :],
                                        wd_ref[0, 0, t * tf:(t + 1) * tf, :].astype(BF16),
                                        preferred_element_type=F32)
                return acc

            y_ref[0:m, :] = _pack_bf16_pair(down(wd_lo_ref), down(wd_hi_ref))
            if m < rows:
                y_ref[m:rows, :] = jnp.zeros((rows - m, y_ref.shape[1]), y_ref.dtype)

    @pl.when(jnp.logical_and(jnp.logical_not(used), jnp.logical_not(up_phase)))
    def _():
        y_ref[...] = jnp.zeros_like(y_ref)


def _moe_experts(h2p, blk_expert, blk_cnt, blk_first, tok_sorted, w_gate, w_up, w_down, layer):
    d = 2 * h2p.shape[1]
    d_ff = w_gate.shape[3]
    rows = MOE_ROWS
    n_blk = blk_expert.shape[0]
    tf = MOE_FF_TILE
    tn = MOE_OUT_TILE
    n_f = d_ff // tf
    n_o = d // 2 // tn

    def ff_tile(b, j, cnt):
        return jnp.where(cnt[b] > 0, jnp.minimum(j, n_f - 1), n_f - 1)

    def out_tile(b, j):
        return jnp.clip(j - n_f, 0, n_o - 1)

    def w_out_tile(b, j, cnt):
        return jnp.where(cnt[b] > 0, out_tile(b, j), n_o - 1)

    grid_spec = pltpu.PrefetchScalarGridSpec(
        num_scalar_prefetch=4,
        grid=(n_blk, n_f + n_o),
        in_specs=[pl.BlockSpec(memory_space=pl.ANY),
                  pl.BlockSpec((1, 1, d, tf),
                               lambda b, j, be, cnt, *_: (layer, be[b], 0, ff_tile(b, j, cnt))),
                  pl.BlockSpec((1, 1, d, tf),
                               lambda b, j, be, cnt, *_: (layer, be[b], 0, ff_tile(b, j, cnt))),
                  pl.BlockSpec((1, 1, d_ff, tn),
                               lambda b, j, be, cnt, *_: (layer, be[b], 0, w_out_tile(b, j, cnt))),
                  pl.BlockSpec((1, 1, d_ff, tn),
                               lambda b, j, be, cnt, *_: (layer, be[b], 0, n_o + w_out_tile(b, j, cnt)))],
        out_specs=pl.BlockSpec((rows, tn), lambda b, j, *_: (b, out_tile(b, j))),
        scratch_shapes=[pltpu.VMEM((rows, d // 2), jnp.uint32),
                        pltpu.VMEM((rows, d), BF16),
                        pltpu.VMEM((n_f, rows, tf), BF16),
                        pltpu.SemaphoreType.DMA(())],
    )
    return pl.pallas_call(
        functools.partial(_moe_kernel, n_ff_tiles=n_f, n_blk=n_blk),
        grid_spec=grid_spec,
        out_shape=jax.ShapeDtypeStruct((n_blk * rows, d // 2), jnp.uint32),
        compiler_params=_cparams(("arbitrary", "arbitrary")),
        name="moe_experts",
    )(blk_expert, blk_cnt, blk_first, tok_sorted, h2p, w_gate, w_up, w_down, w_down)


def _combine_kernel(pos_ref, pos_next_ref, x_ref, w_ref, y_hbm, o_ref, buf_ref, sem, *, n_steps):
    rows = COMBINE_ROWS
    n_copy = TOP_K * rows
    step = pl.program_id(0)
    slot = step % 2

    def row_copy(s, i, src):
        return pltpu.make_async_copy(y_hbm.at[pl.ds(src, 1)], buf_ref.at[s, pl.ds(i, 1)], sem.at[s])

    def issue(s, idx_ref):
        def body(i, carry):
            row_copy(s, i, idx_ref[0, 0, i]).start()
            return carry
        lax.fori_loop(0, n_copy, body, 0, unroll=DMA_ISSUE_UNROLL)

    @pl.when(step == 0)
    def _():
        issue(0, pos_ref)

    @pl.when(step + 1 < n_steps)
    def _():
        issue(1 - slot, pos_next_ref)

    def drain(i, carry):
        row_copy(slot, i, 0).wait()
        return carry

    lax.fori_loop(0, n_copy, drain, 0, unroll=DMA_ISSUE_UNROLL)
    half = x_ref.shape[1] // 2
    lo0, hi0 = _unpack_bf16_pair(buf_ref[slot, pl.ds(0, rows), :])
    lo1, hi1 = _unpack_bf16_pair(buf_ref[slot, pl.ds(rows, rows), :])
    w0 = w_ref[:, 0:1]
    w1 = w_ref[:, 1:2]
    o_ref[:, 0:half] = x_ref[:, 0:half] + (w0 * lo0 + w1 * lo1)
    o_ref[:, half:2 * half] = x_ref[:, half:2 * half] + (w0 * hi0 + w1 * hi1)


def _moe_combine(x1, ys, pos, weights):
    n_tok, d = x1.shape
    rows = COMBINE_ROWS
    n_blk = n_tok // rows
    pos_b = pos.reshape(n_blk, rows, TOP_K).transpose(0, 2, 1).reshape(n_blk, 1, TOP_K * rows)
    idx_block = (1, 1, TOP_K * rows)
    return pl.pallas_call(
        functools.partial(_combine_kernel, n_steps=n_blk),
        grid=(n_blk,),
        in_specs=[pl.BlockSpec(idx_block, lambda i: (i, 0, 0), memory_space=pltpu.SMEM),
                  pl.BlockSpec(idx_block, lambda i: (jnp.minimum(i + 1, n_blk - 1), 0, 0),
                               memory_space=pltpu.SMEM),
                  pl.BlockSpec((rows, d), lambda i: (i, 0)),
                  pl.BlockSpec((rows, TOP_K), lambda i: (i, 0)),
                  pl.BlockSpec(memory_space=pl.ANY)],
        out_specs=pl.BlockSpec((rows, d), lambda i: (i, 0)),
        out_shape=jax.ShapeDtypeStruct((n_tok, d), F32),
        scratch_shapes=[pltpu.VMEM((2, TOP_K * rows, d // 2), jnp.uint32),
                        pltpu.SemaphoreType.DMA((2,))],
        compiler_params=_cparams(("arbitrary",)),
        name="moe_combine",
    )(pos_b, pos_b, x1, weights, ys)


def _route(logits, n_tok):
    glog = logits[:, :N_GROUPS]
    gprob = jax.nn.softmax(glog, axis=-1)
    g_sel = jnp.argmax(glog, axis=-1)
    p_g = jnp.take_along_axis(gprob, g_sel[:, None], axis=1)[:, 0]
    elog = logits[:, N_GROUPS:N_GROUPS + N_EXPERTS].reshape(n_tok, N_GROUPS, EXPERTS_PER_GROUP)
    elog = jnp.take_along_axis(elog, g_sel[:, None, None], axis=1)[:, 0]
    top_v, top_i = lax.top_k(elog, TOP_K)
    weights = p_g[:, None] * jax.nn.softmax(top_v, axis=-1)
    expert_ids = (g_sel[:, None] * EXPERTS_PER_GROUP + top_i).astype(jnp.int32)

    n_assign = n_tok * TOP_K
    rows = MOE_ROWS
    flat_e = expert_ids.reshape(n_assign)
    a_ids = jnp.arange(n_assign, dtype=jnp.int32)
    skey = jnp.sort(flat_e * n_assign + a_ids)
    order = skey % n_assign
    inv = jnp.argsort(order).astype(jnp.int32)
    experts = jnp.arange(N_EXPERTS, dtype=jnp.int32)
    counts = jnp.sum((flat_e[:, None] == experts[None, :]).astype(jnp.int32), axis=0)
    blocks = (counts + rows - 1) // rows
    pad_end = jnp.cumsum(blocks * rows)
    pad_start = pad_end - blocks * rows
    start = jnp.cumsum(counts) - counts
    n_slots = n_assign + N_EXPERTS * rows
    n_blk = n_slots // rows
    n_used = jnp.sum(blocks).astype(jnp.int32)
    blk_ids = jnp.arange(n_blk, dtype=jnp.int32)
    blk_expert = jnp.minimum(
        jnp.sum((pad_end[None, :] <= (blk_ids * rows)[:, None]).astype(jnp.int32), axis=1),
        N_EXPERTS - 1)
    blk_cnt = jnp.clip(counts[blk_expert] - (blk_ids * rows - pad_start[blk_expert]), 0, rows)
    blk_cnt = jnp.where(blk_ids < n_used, blk_cnt, 0).astype(jnp.int32)
    blk_first = (start[blk_expert] + blk_ids * rows - pad_start[blk_expert]).astype(jnp.int32)
    tok_sorted = (order // TOP_K).astype(jnp.int32)
    blk_expert = jnp.where(blk_ids < n_used, blk_expert,
                           blk_expert[jnp.maximum(n_used - 1, 0)]).astype(jnp.int32)
    shift = jnp.sum(jnp.where(flat_e[:, None] == experts[None, :], (pad_start - start)[None, :], 0),
                    axis=1)
    pos = (inv + shift).astype(jnp.int32).reshape(n_tok, TOP_K)
    return blk_expert, blk_cnt, blk_first, tok_sorted, pos, weights


def kernel(x, norm1_gain, w_in, attn_q_norm_gain, attn_k_norm_gain, mlstm_gate_bias,
           mlstm_head_norm_gain, branch_gate_bias, w_attn_branch, w_mlstm_branch, w_out,
           norm2_gain, w_router_group, b_router_group, w_router_expert, b_router_expert,
           w_expert_gate, w_expert_up, w_expert_down):
    bsz, s_len, d = x.shape
    assert bsz == 1
    depth = w_in.shape[0]
    xs = x.reshape(s_len, d)
    w_in_t = jnp.swapaxes(w_in, 1, 2)
    n_proj = OFF_GA + 2 * d
    tn = 512

    def proj_rows(j):
        return j * tn + jnp.where(j * tn >= OFF_GA, N_ML_GATES, 0)

    for l in range(depth):
        h = _rmsnorm(xs, norm1_gain[l])
        proj = _matmul_nt(h, w_in_t, l, proj_rows, n_proj, BF16, 1024, tn, "in_proj")
        gates = _matmul_nt(h, w_in_t, l, lambda j: OFF_MG + j * LANES, LANES, F32, 1024, LANES,
                           "gate_proj")[:, :N_ML_GATES]

        att = _attention(proj, attn_q_norm_gain[l], attn_k_norm_gain[l])

        hs = _mlstm(proj, gates, mlstm_gate_bias[l])
        mem = _mlstm_out(hs, proj, mlstm_head_norm_gain[l])

        merged = _branch_mix(att, mem, w_attn_branch, w_mlstm_branch, l, proj, branch_gate_bias)
        x1 = _matmul_residual(merged, w_out, l, xs, 1024, 512)

        w_r = jnp.concatenate([w_router_group[l], w_router_expert[l]], axis=1)
        b_r = jnp.concatenate([b_router_group[l], b_router_expert[l]])
        n_r = w_r.shape[1]
        w_r = jnp.pad(w_r, ((0, 0), (0, LANES - n_r)))
        b_r = jnp.pad(b_r, (0, LANES - n_r)).reshape(1, LANES)
        h2, logits = _norm_router(x1, norm2_gain[l], w_r, b_r)

        blk_expert, blk_cnt, blk_first, tok_sorted, pos, weights = _route(logits, s_len)
        ys = _moe_experts(h2, blk_expert, blk_cnt, blk_first, tok_sorted,
                          w_expert_gate, w_expert_up, w_expert_down, l)
        xs = _moe_combine(x1, ys, pos, weights)
    return xs.reshape(bsz, s_len, d)
```

```python
import functools

import jax
import jax.numpy as jnp
from jax import lax
from jax.experimental import pallas as pl
from jax.experimental.pallas import tpu as pltpu

F32 = jnp.float32
BF16 = jnp.bfloat16

EPS = 1e-6
NEG = -1e30
LANES = 128

ATT_PATTERNS = ((128, 1), (512, 4), (2048, 16))
N_ATT_GROUPS = 3
ATT_HEADS = 8
ATT_HEAD_DIM = 128
ATT_GROUP_W = ATT_HEADS * ATT_HEAD_DIM
ATT_W = N_ATT_GROUPS * ATT_GROUP_W
ROPE_THETA = 10000.0
ATT_Q_TILE = 128
ATT_TILE_UNROLL = 4

ML_HEADS = 8
ML_QK_DIM = 256
ML_V_DIM = 512
ML_QK_W = ML_HEADS * ML_QK_DIM
ML_V_W = ML_HEADS * ML_V_DIM
N_ML_GATES = 4 * ML_HEADS
GATE_SOFTCAP = 15.0
ML_TILE = 256

N_GROUPS = 8
EXPERTS_PER_GROUP = 8
N_EXPERTS = N_GROUPS * EXPERTS_PER_GROUP
TOP_K = 2
MOE_ROWS = 512
MOE_SUB_ROWS = 128
MOE_FF_TILE = 256
MOE_OUT_TILE = 1024
COMBINE_ROWS = 256
DMA_ISSUE_UNROLL = 8

OFF_AQ = 0
OFF_AK = OFF_AQ + ATT_W
OFF_AV = OFF_AK + ATT_W
OFF_MQ = OFF_AV + ATT_W
OFF_MK = OFF_MQ + ML_QK_W
OFF_MV = OFF_MK + ML_QK_W
OFF_MO = OFF_MV + ML_V_W
OFF_MG = OFF_MO + ML_V_W
OFF_GA = OFF_MG

VMEM_LIMIT = 56 * 1024 * 1024


def _cparams(sem, vmem=VMEM_LIMIT):
    return pltpu.CompilerParams(dimension_semantics=sem, vmem_limit_bytes=vmem)


def _rmsnorm_kernel(x_ref, g_ref, o_ref):
    x = x_ref[...]
    ms = jnp.mean(x * x, axis=-1, keepdims=True)
    o_ref[...] = (x * lax.rsqrt(ms + EPS) * g_ref[...]).astype(o_ref.dtype)


def _rmsnorm(x, gain, tm=256):
    m, d = x.shape
    return pl.pallas_call(
        _rmsnorm_kernel,
        grid=(m // tm,),
        in_specs=[pl.BlockSpec((tm, d), lambda i: (i, 0)),
                  pl.BlockSpec((1, d), lambda i: (0, 0))],
        out_specs=pl.BlockSpec((tm, d), lambda i: (i, 0)),
        out_shape=jax.ShapeDtypeStruct((m, d), BF16),
        compiler_params=_cparams(("parallel",)),
        name="rmsnorm",
    )(x, gain.reshape(1, d))


def _mm_kernel(a_ref, w_ref, o_ref, wb_ref):
    @pl.when(pl.program_id(1) == 0)
    def _():
        wb_ref[...] = w_ref[0].astype(BF16)

    o_ref[...] = jnp.dot(a_ref[...], wb_ref[...], preferred_element_type=F32).astype(o_ref.dtype)


def _matmul(a, w, layer, col0, n, out_dtype, tm, tn, name):
    m, k = a.shape
    cb = col0 // tn
    return pl.pallas_call(
        _mm_kernel,
        grid=(n // tn, m // tm),
        in_specs=[pl.BlockSpec((tm, k), lambda j, i: (i, 0)),
                  pl.BlockSpec((1, k, tn), lambda j, i: (layer, 0, cb + j))],
        out_specs=pl.BlockSpec((tm, tn), lambda j, i: (i, j)),
        out_shape=jax.ShapeDtypeStruct((m, n), out_dtype),
        scratch_shapes=[pltpu.VMEM((k, tn), BF16)],
        compiler_params=_cparams(("parallel", "arbitrary")),
        name=name,
    )(a, w)


def _mm_nt_kernel(a_ref, wt_ref, o_ref, wb_ref):
    @pl.when(pl.program_id(1) == 0)
    def _():
        wb_ref[...] = wt_ref[0].astype(BF16)

    o_ref[...] = lax.dot_general(a_ref[...], wb_ref[...], (((1,), (1,)), ((), ())),
                                 preferred_element_type=F32).astype(o_ref.dtype)


def _matmul_nt(a, w_t, layer, row_of_tile, n, out_dtype, tm, tn, name):
    m, k = a.shape
    return pl.pallas_call(
        _mm_nt_kernel,
        grid=(n // tn, m // tm),
        in_specs=[pl.BlockSpec((tm, k), lambda j, i: (i, 0)),
                  pl.BlockSpec((pl.Element(1), pl.Element(tn), pl.Element(k)),
                               lambda j, i: (layer, pl.multiple_of(row_of_tile(j), 8), 0))],
        out_specs=pl.BlockSpec((tm, tn), lambda j, i: (i, j)),
        out_shape=jax.ShapeDtypeStruct((m, n), out_dtype),
        scratch_shapes=[pltpu.VMEM((tn, k), BF16)],
        compiler_params=_cparams(("parallel", "arbitrary")),
        name=name,
    )(a, w_t)


def _mm_res_kernel(a_ref, w_ref, r_ref, o_ref, wb_ref):
    @pl.when(pl.program_id(1) == 0)
    def _():
        wb_ref[...] = w_ref[0].astype(BF16)

    o_ref[...] = r_ref[...] + jnp.dot(a_ref[...], wb_ref[...], preferred_element_type=F32)


def _matmul_residual(a, w, layer, res, tm, tn):
    m, k = a.shape
    n = w.shape[2]
    return pl.pallas_call(
        _mm_res_kernel,
        grid=(n // tn, m // tm),
        in_specs=[pl.BlockSpec((tm, k), lambda j, i: (i, 0)),
                  pl.BlockSpec((1, k, tn), lambda j, i: (layer, 0, j)),
                  pl.BlockSpec((tm, tn), lambda j, i: (i, j))],
        out_specs=pl.BlockSpec((tm, tn), lambda j, i: (i, j)),
        out_shape=jax.ShapeDtypeStruct((m, n), F32),
        scratch_shapes=[pltpu.VMEM((k, tn), BF16)],
        compiler_params=_cparams(("parallel", "arbitrary")),
        name="out_proj",
    )(a, w, res)


def _attn_kernel(q_ref, k_ref, v_ref, tab_ref, qg_ref, kg_ref, o_ref,
                 qn_ref, kn_ref, vf_ref, acc_ref, lse_ref, *, s_len):
    grp = pl.program_id(1)
    tq = ATT_Q_TILE
    hd = ATT_HEAD_DIM
    half = hd // 2
    low = lax.broadcasted_iota(jnp.int32, (tq, hd), 1) < half
    row = lax.broadcasted_iota(jnp.int32, (hd, hd), 0)
    col = lax.broadcasted_iota(jnp.int32, (hd, hd), 1)
    ones = jnp.ones((hd, hd), BF16)
    rot = jnp.where(row == col + half, -1.0, jnp.where(col == row + half, 1.0, 0.0)).astype(BF16)

    def prep(i, carry):
        rows = pl.ds(pl.multiple_of(i * tq, tq), tq)
        tab = tab_ref[rows, :]
        swapped = pltpu.roll(tab, half, axis=1)
        c = jnp.where(low, tab, swapped)
        s = jnp.where(low, swapped, tab)
        for src, g_ref, dst in ((q_ref, qg_ref, qn_ref), (k_ref, kg_ref, kn_ref)):
            xf = src[rows, :].astype(F32)
            ssq = jnp.dot((xf * xf).astype(BF16), ones, preferred_element_type=F32)
            y = xf * lax.rsqrt(ssq * (1.0 / hd) + EPS) * g_ref[0]
            y_rot = jnp.dot(y.astype(BF16), rot, preferred_element_type=F32)
            dst[rows, :] = y * c + y_rot * s
        vf_ref[rows, :] = v_ref[rows, :].astype(F32)
        return carry

    lax.fori_loop(0, s_len // tq, prep, 0, unroll=4)

    def run_group(window, dilation, first):
        radius = window // (2 * dilation)
        sub_len = s_len // dilation
        win = tq + 2 * radius
        tiles_per_class = sub_len // tq
        rel = (lax.broadcasted_iota(jnp.int32, (tq, win), 1)
               - lax.broadcasted_iota(jnp.int32, (tq, win), 0))

        def rows_of(start, size):
            if dilation == 1:
                return pl.ds(start, size)
            return pl.ds(start, size, stride=dilation)

        def tile(idx, carry):
            r = idx // tiles_per_class
            q0 = (idx % tiles_per_class) * tq
            k0 = jnp.clip(q0 - radius, 0, sub_len - win)
            q_rows = rows_of(r + dilation * q0, tq)
            k_rows = rows_of(r + dilation * k0, win)
            q = qn_ref[q_rows, :].astype(BF16)
            k = kn_ref[k_rows, :].astype(BF16)
            v = vf_ref[k_rows, :].astype(BF16)
            s = lax.dot_general(q, k, (((1,), (1,)), ((), ())), preferred_element_type=F32)
            s = jnp.where(jnp.abs(rel + (k0 - q0)) <= radius, s, NEG)
            m = jnp.max(s, axis=-1, keepdims=True)
            p = jnp.exp(s - m)
            den = jnp.sum(p, axis=-1, keepdims=True)
            o = jnp.dot(p.astype(BF16), v, preferred_element_type=F32) / den
            lse = jnp.broadcast_to(m + jnp.log(den), (tq, ATT_HEAD_DIM))
            if first:
                acc_ref[q_rows, :] = o
                lse_ref[q_rows, :] = lse
            else:
                old_l = lse_ref[q_rows, :]
                top = jnp.maximum(old_l, lse)
                a = jnp.exp(old_l - top)
                b = jnp.exp(lse - top)
                tot = a + b
                acc_ref[q_rows, :] = (a * acc_ref[q_rows, :] + b * o) / tot
                lse_ref[q_rows, :] = top + jnp.log(tot)
            return carry

        lax.fori_loop(0, s_len // tq, tile, 0, unroll=ATT_TILE_UNROLL)

    for gi, (window, dilation) in enumerate(ATT_PATTERNS):
        @pl.when(grp == gi)
        def _(window=window, dilation=dilation, gi=gi):
            run_group(window, dilation, gi == 0)

    @pl.when(grp == N_ATT_GROUPS - 1)
    def _():
        o_ref[...] = acc_ref[...].astype(o_ref.dtype)


def _attention(proj, q_gain, k_gain):
    s_len = proj.shape[0]
    half = ATT_HEAD_DIM // 2
    inv_freq = ROPE_THETA ** (-jnp.arange(half, dtype=F32) / half)
    ang = jnp.arange(s_len).astype(F32)[:, None] * inv_freq[None, :]
    table = jnp.concatenate([jnp.cos(ang), jnp.sin(ang)], axis=1)
    qb, kb, vb = OFF_AQ // LANES, OFF_AK // LANES, OFF_AV // LANES
    blk = (s_len, ATT_HEAD_DIM)
    gain_spec = pl.BlockSpec((1, 1, ATT_HEAD_DIM), lambda h, g: (g, 0, 0))
    q_scaled = q_gain.reshape(N_ATT_GROUPS, 1, ATT_HEAD_DIM) * (ATT_HEAD_DIM ** -0.5)
    return pl.pallas_call(
        functools.partial(_attn_kernel, s_len=s_len),
        grid=(ATT_HEADS, N_ATT_GROUPS),
        in_specs=[pl.BlockSpec(blk, lambda h, g: (0, qb + g * ATT_HEADS + h)),
                  pl.BlockSpec(blk, lambda h, g: (0, kb + g * ATT_HEADS + h)),
                  pl.BlockSpec(blk, lambda h, g: (0, vb + g * ATT_HEADS + h)),
                  pl.BlockSpec(blk, lambda h, g: (0, 0)),
                  gain_spec, gain_spec],
        out_specs=pl.BlockSpec(blk, lambda h, g: (0, h)),
        out_shape=jax.ShapeDtypeStruct((s_len, ATT_GROUP_W), BF16),
        scratch_shapes=[pltpu.VMEM(blk, F32)] * 5,
        compiler_params=_cparams(("parallel", "arbitrary")),
        name="dilated_attn",
    )(proj, proj, proj, table, q_scaled, k_gain.reshape(N_ATT_GROUPS, 1, ATT_HEAD_DIM))


def _mlstm_kernel(q_ref, k_ref, v_ref, g_ref, b_ref, o_ref, c_ref, n_ref, m_ref):
    lc = ML_TILE
    dh = pl.program_id(0)
    step = pl.program_id(1)
    is_bw = dh >= ML_HEADS
    sgn = jnp.where(is_bw, -1, 1)

    @pl.when(step == 0)
    def _():
        c_ref[...] = jnp.zeros_like(c_ref)
        n_ref[...] = jnp.zeros_like(n_ref)
        m_ref[...] = jnp.zeros_like(m_ref)

    g = g_ref[0, 0] + b_ref[0]
    g = GATE_SOFTCAP * jnp.tanh(g / GATE_SOFTCAP)
    ig_row = g[0:1, :]
    f = g[1:2, :]
    lf_row = jnp.minimum(f, 0.0) - jnp.log1p(jnp.exp(-jnp.abs(f)))

    t_idx = lax.broadcasted_iota(jnp.int32, (lc, lc), 0)
    s_idx = lax.broadcasted_iota(jnp.int32, (lc, lc), 1)
    valid = (s_idx - t_idx) * sgn <= 0
    eye = s_idx == t_idx
    b_col = jnp.sum(jnp.where(valid, lf_row, 0.0), axis=1, keepdims=True)
    b_row = jnp.sum(jnp.where(eye, b_col, 0.0), axis=0, keepdims=True)
    ig_col = jnp.sum(jnp.where(eye, ig_row, 0.0), axis=1, keepdims=True)

    log_d = jnp.where(valid, b_col - b_row + ig_row, NEG)
    m_prev = m_ref[...]
    m_inter = b_col + m_prev
    m_t = jnp.maximum(m_inter, jnp.max(log_d, axis=1, keepdims=True))
    decay_q = jnp.exp(m_inter - m_t)
    d_mat = jnp.exp(log_d - m_t)

    q = q_ref[...]
    k = k_ref[...] * (ML_QK_DIM ** -0.5)
    v = v_ref[...]
    s = lax.dot_general(q, k, (((1,), (1,)), ((), ())), preferred_element_type=F32) * d_mat
    c_old = c_ref[...]
    num = jnp.dot(s.astype(BF16), v, preferred_element_type=F32)
    num = num + decay_q * jnp.dot(q, c_old.astype(BF16), preferred_element_type=F32)
    qn = jnp.sum(q.astype(F32) * n_ref[...], axis=1, keepdims=True)
    den = jnp.sum(s, axis=1, keepdims=True) + decay_q * qn
    o_ref[0] = num / jnp.maximum(jnp.abs(den), jnp.exp(-m_t))

    last = jnp.where(is_bw, 0, lc - 1)
    sel = lax.broadcasted_iota(jnp.int32, (lc, 1), 0) == last
    m_last = jnp.max(jnp.where(sel, m_t, -jnp.inf), axis=0, keepdims=True)
    b_last = jnp.sum(jnp.where(sel, b_col, 0.0), axis=0, keepdims=True)
    w_col = jnp.exp(b_last - b_col + ig_col - m_last)
    decay_c = jnp.exp(b_last + m_prev - m_last)
    kw = k.astype(F32) * w_col
    c_ref[...] = decay_c * c_old + lax.dot_general(
        kw.astype(BF16), v, (((0,), (0,)), ((), ())), preferred_element_type=F32)
    n_ref[...] = decay_c * n_ref[...] + jnp.sum(kw, axis=0, keepdims=True)
    m_ref[...] = m_last


def _mlstm(proj, gates, gate_bias):
    s_len = proj.shape[0]
    lc = ML_TILE
    nt = s_len // lc
    gt = gates.reshape(s_len, 2, 2, ML_HEADS).transpose(1, 3, 2, 0)
    gt = gt.reshape(2 * ML_HEADS, 2, nt, lc).transpose(0, 2, 1, 3)
    bias = gate_bias.reshape(2, 2, ML_HEADS).transpose(0, 2, 1).reshape(2 * ML_HEADS, 2, 1)

    def tile_of(dh, t):
        return jnp.where(dh >= ML_HEADS, nt - 1 - t, t)

    qb, kb = OFF_MQ // ML_QK_DIM, OFF_MK // ML_QK_DIM
    vb = OFF_MV // ML_V_DIM
    return pl.pallas_call(
        _mlstm_kernel,
        grid=(2 * ML_HEADS, nt),
        in_specs=[pl.BlockSpec((lc, ML_QK_DIM), lambda dh, t: (tile_of(dh, t), qb + dh % ML_HEADS)),
                  pl.BlockSpec((lc, ML_QK_DIM), lambda dh, t: (tile_of(dh, t), kb + dh % ML_HEADS)),
                  pl.BlockSpec((lc, ML_V_DIM), lambda dh, t: (tile_of(dh, t), vb + dh % ML_HEADS)),
                  pl.BlockSpec((1, 1, 2, lc), lambda dh, t: (dh, tile_of(dh, t), 0, 0)),
                  pl.BlockSpec((1, 2, 1), lambda dh, t: (dh, 0, 0))],
        out_specs=pl.BlockSpec((1, lc, ML_V_DIM),
                               lambda dh, t: (dh // ML_HEADS, tile_of(dh, t), dh % ML_HEADS)),
        out_shape=jax.ShapeDtypeStruct((2, s_len, ML_V_W), F32),
        scratch_shapes=[pltpu.VMEM((ML_QK_DIM, ML_V_DIM), F32),
                        pltpu.VMEM((1, ML_QK_DIM), F32),
                        pltpu.VMEM((1, 1), F32)],
        compiler_params=_cparams(("parallel", "arbitrary")),
        name="mlstm_scan",
    )(proj, proj, proj, gt, bias)


def _mlstm_out_kernel(h_ref, mo_ref, g_ref, o_ref):
    h = h_ref[0] + h_ref[1]
    y = h * lax.rsqrt(jnp.mean(h * h, axis=-1, keepdims=True) + EPS) * g_ref[0]
    o_ref[...] = (y * jax.nn.sigmoid(mo_ref[...].astype(F32))).astype(o_ref.dtype)


def _mlstm_out(hs, proj, head_gain, tm=512):
    s_len = proj.shape[0]
    ob = OFF_MO // ML_V_DIM
    return pl.pallas_call(
        _mlstm_out_kernel,
        grid=(s_len // tm, ML_HEADS),
        in_specs=[pl.BlockSpec((2, tm, ML_V_DIM), lambda i, h: (0, i, h)),
                  pl.BlockSpec((tm, ML_V_DIM), lambda i, h: (i, ob + h)),
                  pl.BlockSpec((1, 1, ML_V_DIM), lambda i, h: (h, 0, 0))],
        out_specs=pl.BlockSpec((tm, ML_V_DIM), lambda i, h: (i, h)),
        out_shape=jax.ShapeDtypeStruct((s_len, ML_V_W), BF16),
        compiler_params=_cparams(("parallel", "parallel")),
        name="mlstm_out",
    )(hs, proj, head_gain.reshape(ML_HEADS, 1, ML_V_DIM))


def _branch_kernel(att_ref, mem_ref, wa_ref, wm_ref, ga_ref, gm_ref, ba_ref, bm_ref, o_ref,
                   wab_ref, wmb_ref):
    @pl.when(pl.program_id(1) == 0)
    def _():
        wab_ref[...] = wa_ref[0].astype(BF16)
        wmb_ref[...] = wm_ref[0].astype(BF16)

    ya = jnp.dot(att_ref[...], wab_ref[...], preferred_element_type=F32)
    ym = jnp.dot(mem_ref[...], wmb_ref[...], preferred_element_type=F32)
    g_a = jax.nn.sigmoid(ga_ref[...].astype(F32) + ba_ref[0])
    g_m = jax.nn.sigmoid(gm_ref[...].astype(F32) + bm_ref[0])
    o_ref[...] = (g_a * ya + g_m * ym).astype(o_ref.dtype)


def _branch_mix(att, mem, w_a, w_m, layer, proj, gate_bias, tm=512, tn=512):
    m = att.shape[0]
    ka, km, d = w_a.shape[1], w_m.shape[1], w_a.shape[2]
    bias = gate_bias.reshape(-1, 1, d)
    gab = OFF_GA // tn
    gmb = (OFF_GA + d) // tn
    return pl.pallas_call(
        _branch_kernel,
        grid=(d // tn, m // tm),
        in_specs=[pl.BlockSpec((tm, ka), lambda j, i: (i, 0)),
                  pl.BlockSpec((tm, km), lambda j, i: (i, 0)),
                  pl.BlockSpec((1, ka, tn), lambda j, i: (layer, 0, j)),
                  pl.BlockSpec((1, km, tn), lambda j, i: (layer, 0, j)),
                  pl.BlockSpec((tm, tn), lambda j, i: (i, gab + j)),
                  pl.BlockSpec((tm, tn), lambda j, i: (i, gmb + j)),
                  pl.BlockSpec((1, 1, tn), lambda j, i: (2 * layer, 0, j)),
                  pl.BlockSpec((1, 1, tn), lambda j, i: (2 * layer + 1, 0, j))],
        out_specs=pl.BlockSpec((tm, tn), lambda j, i: (i, j)),
        out_shape=jax.ShapeDtypeStruct((m, d), BF16),
        scratch_shapes=[pltpu.VMEM((ka, tn), BF16), pltpu.VMEM((km, tn), BF16)],
        compiler_params=_cparams(("parallel", "arbitrary")),
        name="branch_mix",
    )(att, mem, w_a, w_m, proj, proj, bias, bias)


def _pack_bf16_pair(lo, hi):
    lo_bits = lax.bitcast_convert_type(lo.astype(BF16).astype(F32), jnp.uint32)
    hi_bits = lax.bitcast_convert_type(hi.astype(BF16).astype(F32), jnp.uint32)
    return (lo_bits >> 16) | (hi_bits & jnp.uint32(0xFFFF0000))


def _unpack_bf16_pair(word):
    lo = lax.bitcast_convert_type(word << 16, F32)
    hi = lax.bitcast_convert_type(word & jnp.uint32(0xFFFF0000), F32)
    return lo, hi


def _norm_router_kernel(x_ref, g_ref, wr_ref, br_ref, h_ref, lg_ref):
    x = x_ref[...]
    h = x * lax.rsqrt(jnp.mean(x * x, axis=-1, keepdims=True) + EPS) * g_ref[...]
    half = h.shape[1] // 2
    h_ref[...] = _pack_bf16_pair(h[:, :half], h[:, half:])
    lg_ref[...] = jnp.dot(h, wr_ref[...], preferred_element_type=F32,
                          precision=lax.Precision.HIGHEST) + br_ref[...]


def _norm_router(x, gain, w_r, b_r, tm=256):
    m, d = x.shape
    n = w_r.shape[1]
    return pl.pallas_call(
        _norm_router_kernel,
        grid=(m // tm,),
        in_specs=[pl.BlockSpec((tm, d), lambda i: (i, 0)),
                  pl.BlockSpec((1, d), lambda i: (0, 0)),
                  pl.BlockSpec((d, n), lambda i: (0, 0)),
                  pl.BlockSpec((1, n), lambda i: (0, 0))],
        out_specs=[pl.BlockSpec((tm, d // 2), lambda i: (i, 0)),
                   pl.BlockSpec((tm, n), lambda i: (i, 0))],
        out_shape=[jax.ShapeDtypeStruct((m, d // 2), jnp.uint32),
                   jax.ShapeDtypeStruct((m, n), F32)],
        compiler_params=_cparams(("parallel",)),
        name="norm_router",
    )(x, gain.reshape(1, d), w_r, b_r)


def _moe_kernel(blk_e_ref, cnt_ref, first_ref, tok_ref, h_hbm, wg_ref, wu_ref,
                wd_lo_ref, wd_hi_ref, y_ref, xg_ref, xb_ref, act_ref, sem, *, n_ff_tiles, n_blk):
    b = pl.program_id(0)
    j = pl.program_id(1)
    rows = MOE_ROWS
    tf = MOE_FF_TILE
    grp = DMA_ISSUE_UNROLL
    cnt = cnt_ref[b]
    used = cnt > 0
    up_phase = j < n_ff_tiles
    n_sub = (cnt + MOE_SUB_ROWS - 1) // MOE_SUB_ROWS

    def row_copy(i, tok):
        return pltpu.make_async_copy(h_hbm.at[pl.ds(tok, 1)], xg_ref.at[pl.ds(i, 1)], sem)

    def issue_rows(blk):
        first = first_ref[blk]
        last_entry = tok_ref.shape[0] - 1

        def body(gi, carry):
            for r in range(grp):
                i = gi * grp + r
                row_copy(i, tok_ref[jnp.minimum(first + i, last_entry)]).start()
            return carry
        lax.fori_loop(0, (cnt_ref[blk] + grp - 1) // grp, body, 0)

    def wait_rows(count):
        def body(gi, carry):
            for r in range(grp):
                row_copy(gi * grp + r, 0).wait()
            return carry
        lax.fori_loop(0, (count + grp - 1) // grp, body, 0)

    @pl.when(jnp.logical_and(used, j == 0))
    def _():
        @pl.when(b == 0)
        def _():
            xg_ref[...] = jnp.zeros_like(xg_ref)
            issue_rows(b)

        wait_rows(cnt)
        half = xg_ref.shape[1]
        lo, hi = _unpack_bf16_pair(xg_ref[...])
        xb_ref[:, 0:half] = lo.astype(BF16)
        xb_ref[:, half:2 * half] = hi.astype(BF16)
        @pl.when(b + 1 < n_blk)
        def _():
            issue_rows(b + 1)

    for m in range(MOE_SUB_ROWS, rows + 1, MOE_SUB_ROWS):
        live = jnp.logical_and(used, n_sub == m // MOE_SUB_ROWS)

        @pl.when(jnp.logical_and(live, up_phase))
        def _(m=m):
            x = xb_ref[0:m, :]
            g = jnp.dot(x, wg_ref[0, 0].astype(BF16), preferred_element_type=F32)
            u = jnp.dot(x, wu_ref[0, 0].astype(BF16), preferred_element_type=F32)
            act_ref[j, 0:m, :] = (g * jax.nn.sigmoid(g) * u).astype(BF16)

        @pl.when(jnp.logical_and(live, jnp.logical_not(up_phase)))
        def _(m=m):
            def down(wd_ref):
                acc = jnp.dot(act_ref[0, 0:m, :], wd_ref[0, 0, 0:tf, :].astype(BF16),
                              preferred_element_type=F32)
                for t in range(1, n_ff_tiles):
                    acc = acc + jnp.dot(act_ref[t, 0:m, :],
                                        wd_ref[0, 0, t * tf:(t + 1) * tf, :].astype(BF16),
                                        preferred_element_type=F32)
                return acc

            y_ref[0:m, :] = _pack_bf16_pair(down(wd_lo_ref), down(wd_hi_ref))
            if m < rows:
                y_ref[m:rows, :] = jnp.zeros((rows - m, y_ref.shape[1]), y_ref.dtype)

    @pl.when(jnp.logical_and(jnp.logical_not(used), jnp.logical_not(up_phase)))
    def _():
        y_ref[...] = jnp.zeros_like(y_ref)


def _moe_experts(h2p, blk_expert, blk_cnt, blk_first, tok_sorted, w_gate, w_up, w_down, layer):
    d = 2 * h2p.shape[1]
    d_ff = w_gate.shape[3]
    rows = MOE_ROWS
    n_blk = blk_expert.shape[0]
    tf = MOE_FF_TILE
    tn = MOE_OUT_TILE
    n_f = d_ff // tf
    n_o = d // 2 // tn

    def ff_tile(b, j, cnt):
        return jnp.where(cnt[b] > 0, jnp.minimum(j, n_f - 1), n_f - 1)

    def out_tile(b, j):
        return jnp.clip(j - n_f, 0, n_o - 1)

    def w_down_block(b, j, be, cnt, hi):
        stay = jnp.logical_or(j < n_f, cnt[b] == 0)
        expert = jnp.where(j < n_f, be[jnp.maximum(b - 1, 0)], be[b])
        return layer, expert, 0, hi * n_o + jnp.where(stay, n_o - 1, j - n_f)

    grid_spec = pltpu.PrefetchScalarGridSpec(
        num_scalar_prefetch=4,
        grid=(n_blk, n_f + n_o),
        in_specs=[pl.BlockSpec(memory_space=pl.ANY),
                  pl.BlockSpec((1, 1, d, tf),
                               lambda b, j, be, cnt, *_: (layer, be[b], 0, ff_tile(b, j, cnt))),
                  pl.BlockSpec((1, 1, d, tf),
                               lambda b, j, be, cnt, *_: (layer, be[b], 0, ff_tile(b, j, cnt))),
                  pl.BlockSpec((1, 1, d_ff, tn),
                               lambda b, j, be, cnt, *_: w_down_block(b, j, be, cnt, 0)),
                  pl.BlockSpec((1, 1, d_ff, tn),
                               lambda b, j, be, cnt, *_: w_down_block(b, j, be, cnt, 1))],
        out_specs=pl.BlockSpec((rows, tn), lambda b, j, *_: (b, out_tile(b, j))),
        scratch_shapes=[pltpu.VMEM((rows, d // 2), jnp.uint32),
                        pltpu.VMEM((rows, d), BF16),
                        pltpu.VMEM((n_f, rows, tf), BF16),
                        pltpu.SemaphoreType.DMA(())],
    )
    return pl.pallas_call(
        functools.partial(_moe_kernel, n_ff_tiles=n_f, n_blk=n_blk),
        grid_spec=grid_spec,
        out_shape=jax.ShapeDtypeStruct((n_blk * rows, d // 2), jnp.uint32),
        compiler_params=_cparams(("arbitrary", "arbitrary")),
        name="moe_experts",
    )(blk_expert, blk_cnt, blk_first, tok_sorted, h2p, w_gate, w_up, w_down, w_down)


def _combine_kernel(pos_ref, pos_next_ref, x_ref, w_ref, y_hbm, o_ref, buf_ref, sem, *, n_steps):
    rows = COMBINE_ROWS
    n_copy = TOP_K * rows
    step = pl.program_id(0)
    slot = step % 2

    def row_copy(s, i, src):
        return pltpu.make_async_copy(y_hbm.at[pl.ds(src, 1)], buf_ref.at[s, pl.ds(i, 1)], sem.at[s])

    def issue(s, idx_ref):
        def body(i, carry):
            row_copy(s, i, idx_ref[0, 0, i]).start()
            return carry
        lax.fori_loop(0, n_copy, body, 0, unroll=DMA_ISSUE_UNROLL)

    @pl.when(step == 0)
    def _():
        issue(0, pos_ref)

    @pl.when(step + 1 < n_steps)
    def _():
        issue(1 - slot, pos_next_ref)

    def drain(i, carry):
        row_copy(slot, i, 0).wait()
        return carry

    lax.fori_loop(0, n_copy, drain, 0, unroll=DMA_ISSUE_UNROLL)
    half = x_ref.shape[1] // 2
    lo0, hi0 = _unpack_bf16_pair(buf_ref[slot, pl.ds(0, rows), :])
    lo1, hi1 = _unpack_bf16_pair(buf_ref[slot, pl.ds(rows, rows), :])
    w0 = w_ref[:, 0:1]
    w1 = w_ref[:, 1:2]
    o_ref[:, 0:half] = x_ref[:, 0:half] + (w0 * lo0 + w1 * lo1)
    o_ref[:, half:2 * half] = x_ref[:, half:2 * half] + (w0 * hi0 + w1 * hi1)


def _moe_combine(x1, ys, pos, weights):
    n_tok, d = x1.shape
    rows = COMBINE_ROWS
    n_blk = n_tok // rows
    pos_b = pos.reshape(n_blk, rows, TOP_K).transpose(0, 2, 1).reshape(n_blk, 1, TOP_K * rows)
    idx_block = (1, 1, TOP_K * rows)
    return pl.pallas_call(
        functools.partial(_combine_kernel, n_steps=n_blk),
        grid=(n_blk,),
        in_specs=[pl.BlockSpec(idx_block, lambda i: (i, 0, 0), memory_space=pltpu.SMEM),
                  pl.BlockSpec(idx_block, lambda i: (jnp.minimum(i + 1, n_blk - 1), 0, 0),
                               memory_space=pltpu.SMEM),
                  pl.BlockSpec((rows, d), lambda i: (i, 0)),
                  pl.BlockSpec((rows, TOP_K), lambda i: (i, 0)),
                  pl.BlockSpec(memory_space=pl.ANY)],
        out_specs=pl.BlockSpec((rows, d), lambda i: (i, 0)),
        out_shape=jax.ShapeDtypeStruct((n_tok, d), F32),
        scratch_shapes=[pltpu.VMEM((2, TOP_K * rows, d // 2), jnp.uint32),
                        pltpu.SemaphoreType.DMA((2,))],
        compiler_params=_cparams(("arbitrary",)),
        name="moe_combine",
    )(pos_b, pos_b, x1, weights, ys)


def _route(logits, n_tok):
    glog = logits[:, :N_GROUPS]
    gprob = jax.nn.softmax(glog, axis=-1)
    g_sel = jnp.argmax(glog, axis=-1)
    p_g = jnp.take_along_axis(gprob, g_sel[:, None], axis=1)[:, 0]
    elog = logits[:, N_GROUPS:N_GROUPS + N_EXPERTS].reshape(n_tok, N_GROUPS, EXPERTS_PER_GROUP)
    elog = jnp.take_along_axis(elog, g_sel[:, None, None], axis=1)[:, 0]
    top_v, top_i = lax.top_k(elog, TOP_K)
    weights = p_g[:, None] * jax.nn.softmax(top_v, axis=-1)
    expert_ids = (g_sel[:, None] * EXPERTS_PER_GROUP + top_i).astype(jnp.int32)

    n_assign = n_tok * TOP_K
    rows = MOE_ROWS
    flat_e = expert_ids.reshape(n_assign)
    a_ids = jnp.arange(n_assign, dtype=jnp.int32)
    skey = jnp.sort(flat_e * n_assign + a_ids)
    order = skey % n_assign
    inv = jnp.argsort(order).astype(jnp.int32)
    experts = jnp.arange(N_EXPERTS, dtype=jnp.int32)
    counts = jnp.sum((flat_e[:, None] == experts[None, :]).astype(jnp.int32), axis=0)
    blocks = (counts + rows - 1) // rows
    pad_end = jnp.cumsum(blocks * rows)
    pad_start = pad_end - blocks * rows
    start = jnp.cumsum(counts) - counts
    n_slots = n_assign + N_EXPERTS * rows
    n_blk = n_slots // rows
    n_used = jnp.sum(blocks).astype(jnp.int32)
    blk_ids = jnp.arange(n_blk, dtype=jnp.int32)
    blk_expert = jnp.minimum(
        jnp.sum((pad_end[None, :] <= (blk_ids * rows)[:, None]).astype(jnp.int32), axis=1),
        N_EXPERTS - 1)
    blk_cnt = jnp.clip(counts[blk_expert] - (blk_ids * rows - pad_start[blk_expert]), 0, rows)
    blk_cnt = jnp.where(blk_ids < n_used, blk_cnt, 0).astype(jnp.int32)
    blk_first = (start[blk_expert] + blk_ids * rows - pad_start[blk_expert]).astype(jnp.int32)
    tok_sorted = (order // TOP_K).astype(jnp.int32)
    blk_expert = jnp.where(blk_ids < n_used, blk_expert,
                           blk_expert[jnp.maximum(n_used - 1, 0)]).astype(jnp.int32)
    shift = jnp.sum(jnp.where(flat_e[:, None] == experts[None, :], (pad_start - start)[None, :], 0),
                    axis=1)
    pos = (inv + shift).astype(jnp.int32).reshape(n_tok, TOP_K)
    return blk_expert, blk_cnt, blk_first, tok_sorted, pos, weights


def kernel(x, norm1_gain, w_in, attn_q_norm_gain, attn_k_norm_gain, mlstm_gate_bias,
           mlstm_head_norm_gain, branch_gate_bias, w_attn_branch, w_mlstm_branch, w_out,
           norm2_gain, w_router_group, b_router_group, w_router_expert, b_router_expert,
           w_expert_gate, w_expert_up, w_expert_down):
    bsz, s_len, d = x.shape
    assert bsz == 1
    depth = w_in.shape[0]
    xs = x.reshape(s_len, d)
    w_in_t = jnp.swapaxes(w_in, 1, 2)
    n_proj = OFF_GA + 2 * d
    tn = 512

    def proj_rows(j):
        return j * tn + jnp.where(j * tn >= OFF_GA, N_ML_GATES, 0)

    for l in range(depth):
        h = _rmsnorm(xs, norm1_gain[l])
        proj = _matmul_nt(h, w_in_t, l, proj_rows, n_proj, BF16, 1024, tn, "in_proj")
        gates = _matmul_nt(h, w_in_t, l, lambda j: OFF_MG + j * LANES, LANES, F32, 1024, LANES,
                           "gate_proj")[:, :N_ML_GATES]

        att = _attention(proj, attn_q_norm_gain[l], attn_k_norm_gain[l])

        hs = _mlstm(proj, gates, mlstm_gate_bias[l])
        mem = _mlstm_out(hs, proj, mlstm_head_norm_gain[l])

        merged = _branch_mix(att, mem, w_attn_branch, w_mlstm_branch, l, proj, branch_gate_bias)
        x1 = _matmul_residual(merged, w_out, l, xs, 1024, 512)

        w_r = jnp.concatenate([w_router_group[l], w_router_expert[l]], axis=1)
        b_r = jnp.concatenate([b_router_group[l], b_router_expert[l]])
        n_r = w_r.shape[1]
        w_r = jnp.pad(w_r, ((0, 0), (0, LANES - n_r)))
        b_r = jnp.pad(b_r, (0, LANES - n_r)).reshape(1, LANES)
        h2, logits = _norm_router(x1, norm2_gain[l], w_r, b_r)

        blk_expert, blk_cnt, blk_first, tok_sorted, pos, weights = _route(logits, s_len)
        ys = _moe_experts(h2, blk_expert, blk_cnt, blk_first, tok_sorted,
                          w_expert_gate, w_expert_up, w_expert_down, l)
        xs = _moe_combine(x1, ys, pos, weights)
    return xs.reshape(bsz, s_len, d)
```

```python
import functools

import jax
import jax.numpy as jnp
from jax import lax
from jax.experimental import pallas as pl
from jax.experimental.pallas import tpu as pltpu

F32 = jnp.float32
BF16 = jnp.bfloat16

EPS = 1e-6
NEG = -1e30
LANES = 128

ATT_PATTERNS = ((128, 1), (512, 4), (2048, 16))
N_ATT_GROUPS = 3
ATT_HEADS = 8
ATT_HEAD_DIM = 128
ATT_GROUP_W = ATT_HEADS * ATT_HEAD_DIM
ATT_W = N_ATT_GROUPS * ATT_GROUP_W
ROPE_THETA = 10000.0
ATT_Q_TILE = 128
ATT_TILE_UNROLL = 4

ML_HEADS = 8
ML_QK_DIM = 256
ML_V_DIM = 512
ML_QK_W = ML_HEADS * ML_QK_DIM
ML_V_W = ML_HEADS * ML_V_DIM
N_ML_GATES = 4 * ML_HEADS
GATE_SOFTCAP = 15.0
ML_TILE = 256
ML_HEAD_GROUP = 2

N_GROUPS = 8
EXPERTS_PER_GROUP = 8
N_EXPERTS = N_GROUPS * EXPERTS_PER_GROUP
TOP_K = 2
MOE_ROWS = 512
MOE_SUB_ROWS = 128
MOE_FF_TILE = 256
MOE_OUT_TILE = 1024
COMBINE_ROWS = 256
DMA_ISSUE_UNROLL = 8

OFF_AQ = 0
OFF_AK = OFF_AQ + ATT_W
OFF_AV = OFF_AK + ATT_W
OFF_MQ = OFF_AV + ATT_W
OFF_MK = OFF_MQ + ML_QK_W
OFF_MV = OFF_MK + ML_QK_W
OFF_MO = OFF_MV + ML_V_W
OFF_MG = OFF_MO + ML_V_W
OFF_GA = OFF_MG

VMEM_LIMIT = 56 * 1024 * 1024


def _cparams(sem, vmem=VMEM_LIMIT):
    return pltpu.CompilerParams(dimension_semantics=sem, vmem_limit_bytes=vmem)


def _rmsnorm_kernel(x_ref, g_ref, o_ref):
    x = x_ref[...]
    ms = jnp.mean(x * x, axis=-1, keepdims=True)
    o_ref[...] = (x * lax.rsqrt(ms + EPS) * g_ref[...]).astype(o_ref.dtype)


def _rmsnorm(x, gain, tm=256):
    m, d = x.shape
    return pl.pallas_call(
        _rmsnorm_kernel,
        grid=(m // tm,),
        in_specs=[pl.BlockSpec((tm, d), lambda i: (i, 0)),
                  pl.BlockSpec((1, d), lambda i: (0, 0))],
        out_specs=pl.BlockSpec((tm, d), lambda i: (i, 0)),
        out_shape=jax.ShapeDtypeStruct((m, d), BF16),
        compiler_params=_cparams(("parallel",)),
        name="rmsnorm",
    )(x, gain.reshape(1, d))


def _mm_kernel(a_ref, w_ref, o_ref, wb_ref):
    @pl.when(pl.program_id(1) == 0)
    def _():
        wb_ref[...] = w_ref[0].astype(BF16)

    o_ref[...] = jnp.dot(a_ref[...], wb_ref[...], preferred_element_type=F32).astype(o_ref.dtype)


def _matmul(a, w, layer, col0, n, out_dtype, tm, tn, name):
    m, k = a.shape
    cb = col0 // tn
    return pl.pallas_call(
        _mm_kernel,
        grid=(n // tn, m // tm),
        in_specs=[pl.BlockSpec((tm, k), lambda j, i: (i, 0)),
                  pl.BlockSpec((1, k, tn), lambda j, i: (layer, 0, cb + j))],
        out_specs=pl.BlockSpec((tm, tn), lambda j, i: (i, j)),
        out_shape=jax.ShapeDtypeStruct((m, n), out_dtype),
        scratch_shapes=[pltpu.VMEM((k, tn), BF16)],
        compiler_params=_cparams(("parallel", "arbitrary")),
        name=name,
    )(a, w)


def _mm_nt_kernel(a_ref, wt_ref, o_ref, wb_ref):
    @pl.when(pl.program_id(1) == 0)
    def _():
        wb_ref[...] = wt_ref[0].astype(BF16)

    o_ref[...] = lax.dot_general(a_ref[...], wb_ref[...], (((1,), (1,)), ((), ())),
                                 preferred_element_type=F32).astype(o_ref.dtype)


def _matmul_nt(a, w_t, layer, row_of_tile, n, out_dtype, tm, tn, name):
    m, k = a.shape
    return pl.pallas_call(
        _mm_nt_kernel,
        grid=(n // tn, m // tm),
        in_specs=[pl.BlockSpec((tm, k), lambda j, i: (i, 0)),
                  pl.BlockSpec((pl.Element(1), pl.Element(tn), pl.Element(k)),
                               lambda j, i: (layer, pl.multiple_of(row_of_tile(j), 8), 0))],
        out_specs=pl.BlockSpec((tm, tn), lambda j, i: (i, j)),
        out_shape=jax.ShapeDtypeStruct((m, n), out_dtype),
        scratch_shapes=[pltpu.VMEM((tn, k), BF16)],
        compiler_params=_cparams(("parallel", "arbitrary")),
        name=name,
    )(a, w_t)


def _mm_res_kernel(a_ref, w_ref, r_ref, o_ref, wb_ref):
    @pl.when(pl.program_id(1) == 0)
    def _():
        wb_ref[...] = w_ref[0].astype(BF16)

    o_ref[...] = r_ref[...] + jnp.dot(a_ref[...], wb_ref[...], preferred_element_type=F32)


def _matmul_residual(a, w, layer, res, tm, tn):
    m, k = a.shape
    n = w.shape[2]
    return pl.pallas_call(
        _mm_res_kernel,
        grid=(n // tn, m // tm),
        in_specs=[pl.BlockSpec((tm, k), lambda j, i: (i, 0)),
                  pl.BlockSpec((1, k, tn), lambda j, i: (layer, 0, j)),
                  pl.BlockSpec((tm, tn), lambda j, i: (i, j))],
        out_specs=pl.BlockSpec((tm, tn), lambda j, i: (i, j)),
        out_shape=jax.ShapeDtypeStruct((m, n), F32),
        scratch_shapes=[pltpu.VMEM((k, tn), BF16)],
        compiler_params=_cparams(("parallel", "arbitrary")),
        name="out_proj",
    )(a, w, res)


def _attn_kernel(q_ref, k_ref, v_ref, tab_ref, qg_ref, kg_ref, o_ref,
                 qn_ref, kn_ref, vf_ref, acc_ref, lse_ref, *, s_len):
    grp = pl.program_id(1)
    tq = ATT_Q_TILE
    hd = ATT_HEAD_DIM
    half = hd // 2
    low = lax.broadcasted_iota(jnp.int32, (tq, hd), 1) < half
    row = lax.broadcasted_iota(jnp.int32, (hd, hd), 0)
    col = lax.broadcasted_iota(jnp.int32, (hd, hd), 1)
    ones = jnp.ones((hd, hd), BF16)
    rot = jnp.where(row == col + half, -1.0, jnp.where(col == row + half, 1.0, 0.0)).astype(BF16)

    def prep(i, carry):
        rows = pl.ds(pl.multiple_of(i * tq, tq), tq)
        tab = tab_ref[rows, :]
        swapped = pltpu.roll(tab, half, axis=1)
        c = jnp.where(low, tab, swapped)
        s = jnp.where(low, swapped, tab)
        for src, g_ref, dst in ((q_ref, qg_ref, qn_ref), (k_ref, kg_ref, kn_ref)):
            xf = src[rows, :].astype(F32)
            ssq = jnp.dot((xf * xf).astype(BF16), ones, preferred_element_type=F32)
            y = xf * lax.rsqrt(ssq * (1.0 / hd) + EPS) * g_ref[0]
            y_rot = jnp.dot(y.astype(BF16), rot, preferred_element_type=F32)
            dst[rows, :] = y * c + y_rot * s
        vf_ref[rows, :] = v_ref[rows, :].astype(F32)
        return carry

    lax.fori_loop(0, s_len // tq, prep, 0, unroll=4)

    def run_group(window, dilation, first):
        radius = window // (2 * dilation)
        sub_len = s_len // dilation
        win = tq + 2 * radius
        tiles_per_class = sub_len // tq
        rel = (lax.broadcasted_iota(jnp.int32, (tq, win), 1)
               - lax.broadcasted_iota(jnp.int32, (tq, win), 0))

        def rows_of(start, size):
            if dilation == 1:
                return pl.ds(start, size)
            return pl.ds(start, size, stride=dilation)

        def tile(idx, carry):
            r = idx // tiles_per_class
            q0 = (idx % tiles_per_class) * tq
            k0 = jnp.clip(q0 - radius, 0, sub_len - win)
            q_rows = rows_of(r + dilation * q0, tq)
            k_rows = rows_of(r + dilation * k0, win)
            q = qn_ref[q_rows, :].astype(BF16)
            k = kn_ref[k_rows, :].astype(BF16)
            v = vf_ref[k_rows, :].astype(BF16)
            s = lax.dot_general(q, k, (((1,), (1,)), ((), ())), preferred_element_type=F32)
            s = jnp.where(jnp.abs(rel + (k0 - q0)) <= radius, s, NEG)
            m = jnp.max(s, axis=-1, keepdims=True)
            p = jnp.exp(s - m)
            den = jnp.sum(p, axis=-1, keepdims=True)
            o = jnp.dot(p.astype(BF16), v, preferred_element_type=F32) / den
            lse = jnp.broadcast_to(m + jnp.log(den), (tq, ATT_HEAD_DIM))
            if first:
                acc_ref[q_rows, :] = o
                lse_ref[q_rows, :] = lse
            else:
                old_l = lse_ref[q_rows, :]
                top = jnp.maximum(old_l, lse)
                a = jnp.exp(old_l - top)
                b = jnp.exp(lse - top)
                tot = a + b
                acc_ref[q_rows, :] = (a * acc_ref[q_rows, :] + b * o) / tot
                lse_ref[q_rows, :] = top + jnp.log(tot)
            return carry

        lax.fori_loop(0, s_len // tq, tile, 0, unroll=ATT_TILE_UNROLL)

    for gi, (window, dilation) in enumerate(ATT_PATTERNS):
        @pl.when(grp == gi)
        def _(window=window, dilation=dilation, gi=gi):
            run_group(window, dilation, gi == 0)

    @pl.when(grp == N_ATT_GROUPS - 1)
    def _():
        o_ref[...] = acc_ref[...].astype(o_ref.dtype)


def _attention(proj, q_gain, k_gain):
    s_len = proj.shape[0]
    half = ATT_HEAD_DIM // 2
    inv_freq = ROPE_THETA ** (-jnp.arange(half, dtype=F32) / half)
    ang = jnp.arange(s_len).astype(F32)[:, None] * inv_freq[None, :]
    table = jnp.concatenate([jnp.cos(ang), jnp.sin(ang)], axis=1)
    qb, kb, vb = OFF_AQ // LANES, OFF_AK // LANES, OFF_AV // LANES
    blk = (s_len, ATT_HEAD_DIM)
    gain_spec = pl.BlockSpec((1, 1, ATT_HEAD_DIM), lambda h, g: (g, 0, 0))
    q_scaled = q_gain.reshape(N_ATT_GROUPS, 1, ATT_HEAD_DIM) * (ATT_HEAD_DIM ** -0.5)
    return pl.pallas_call(
        functools.partial(_attn_kernel, s_len=s_len),
        grid=(ATT_HEADS, N_ATT_GROUPS),
        in_specs=[pl.BlockSpec(blk, lambda h, g: (0, qb + g * ATT_HEADS + h)),
                  pl.BlockSpec(blk, lambda h, g: (0, kb + g * ATT_HEADS + h)),
                  pl.BlockSpec(blk, lambda h, g: (0, vb + g * ATT_HEADS + h)),
                  pl.BlockSpec(blk, lambda h, g: (0, 0)),
                  gain_spec, gain_spec],
        out_specs=pl.BlockSpec(blk, lambda h, g: (0, h)),
        out_shape=jax.ShapeDtypeStruct((s_len, ATT_GROUP_W), BF16),
        scratch_shapes=[pltpu.VMEM(blk, F32)] * 5,
        compiler_params=_cparams(("parallel", "arbitrary")),
        name="dilated_attn",
    )(proj, proj, proj, table, q_scaled, k_gain.reshape(N_ATT_GROUPS, 1, ATT_HEAD_DIM))


def _tile_scan(x, pos, lc, combine, reverse, fill):
    n = x.shape[1]
    k = 1
    while k < lc:
        if reverse:
            shifted = pltpu.roll(x, n - k, axis=1)
            ok = pos < lc - k
        else:
            shifted = pltpu.roll(x, k, axis=1)
            ok = pos >= k
        x = combine(x, jnp.where(ok, shifted, fill))
        k *= 2
    return x


def _mlstm_gate_kernel(ig_ref, f_ref, bias_ref, o_ref, *, lc):
    n_rows, s_len = ig_ref.shape
    n_tiles = s_len // lc
    shape = (n_rows, s_len)
    bw = lax.broadcasted_iota(jnp.int32, shape, 0) >= ML_HEADS
    lane = lax.broadcasted_iota(jnp.int32, shape, 1)
    pos = lane % lc
    tile = lane // lc

    ig = GATE_SOFTCAP * jnp.tanh((ig_ref[...] + bias_ref[:, 0:1]) / GATE_SOFTCAP)
    f = GATE_SOFTCAP * jnp.tanh((f_ref[...] + bias_ref[:, 1:2]) / GATE_SOFTCAP)
    lf = jnp.minimum(f, 0.0) - jnp.log1p(jnp.exp(-jnp.abs(f)))

    pre = _tile_scan(lf, pos, lc, jnp.add, False, 0.0)
    suf = _tile_scan(lf, pos, lc, jnp.add, True, 0.0)
    b = jnp.where(bw, suf, pre)
    tot = pre + suf - lf
    a = ig - b
    pmax = _tile_scan(a, pos, lc, jnp.maximum, False, -jnp.inf)
    smax = _tile_scan(a, pos, lc, jnp.maximum, True, -jnp.inf)
    cm = jnp.where(bw, smax, pmax)
    cmax = jnp.maximum(pmax, smax)

    def tiles_scan(reverse):
        alpha, beta = tot, cmax + tot
        step = 1
        while step < n_tiles:
            sh = step * lc
            if reverse:
                pa, pb = pltpu.roll(alpha, s_len - sh, axis=1), pltpu.roll(beta, s_len - sh, axis=1)
                ok = tile < n_tiles - step
            else:
                pa, pb = pltpu.roll(alpha, sh, axis=1), pltpu.roll(beta, sh, axis=1)
                ok = tile >= step
            alpha, beta = (jnp.where(ok, pa + alpha, alpha),
                           jnp.where(ok, jnp.maximum(pb + alpha, beta), beta))
            step *= 2
        return jnp.maximum(alpha, beta)

    last_f = tiles_scan(False)
    last_r = tiles_scan(True)
    m_last = jnp.where(bw, last_r, last_f)
    prev_f = jnp.where(tile >= 1, pltpu.roll(last_f, lc, axis=1), 0.0)
    prev_r = jnp.where(tile < n_tiles - 1, pltpu.roll(last_r, s_len - lc, axis=1), 0.0)
    m_prev = jnp.where(bw, prev_r, prev_f)
    mm = jnp.maximum(m_prev, cm)

    o_ref[0] = a
    o_ref[1] = mm
    o_ref[2] = b + mm
    o_ref[3] = m_prev
    o_ref[4] = tot - m_last
    o_ref[5] = tot + m_prev - m_last
    o_ref[6] = jnp.zeros(shape, F32)
    o_ref[7] = jnp.zeros(shape, F32)


def _mlstm_scan_kernel(q_ref, k_ref, v_ref, g_ref, mask_ref, o_ref, c_ref):
    dk, dv = ML_QK_DIM, ML_V_DIM

    @pl.when(pl.program_id(1) == 0)
    def _():
        c_ref[...] = jnp.zeros_like(c_ref)

    lc = q_ref.shape[0]
    mask = mask_ref[0]
    ones = jnp.ones((lc, LANES), BF16)
    for hd in range(ML_HEAD_GROUP):
        rows = g_ref[hd]

        def column(r, rows=rows):
            return jnp.broadcast_to(rows[r:r + 1, :], (8, lc)).T[:, 0:1]

        a_row = rows[0:1, :]
        a_col, mm_col, m_col = column(0), column(1), column(2)
        m_prev = rows[3:4, 0:1]
        d_mat = jnp.exp(a_row - mm_col + mask)
        decay_q = jnp.exp(m_prev - mm_col)

        q = q_ref[:, hd * dk:(hd + 1) * dk]
        k = k_ref[:, hd * dk:(hd + 1) * dk] * (dk ** -0.5)
        v_ext = jnp.concatenate([v_ref[:, hd * dv:(hd + 1) * dv], ones], axis=1)
        s = lax.dot_general(q, k, (((1,), (1,)), ((), ())), preferred_element_type=F32) * d_mat
        c_old = c_ref[hd]
        acc = jnp.dot(s.astype(BF16), v_ext, preferred_element_type=F32)
        acc = acc + decay_q * jnp.dot(q, c_old.astype(BF16), preferred_element_type=F32)
        den = acc[:, dv:dv + LANES]
        inv = 1.0 / jnp.maximum(jnp.abs(den), jnp.exp(-m_col))
        for part in range(dv // LANES):
            cols = slice(part * LANES, (part + 1) * LANES)
            o_ref[0, :, hd * dv + part * LANES:hd * dv + (part + 1) * LANES] = (
                acc[:, cols] * inv).astype(o_ref.dtype)

        w_col = jnp.exp(a_col + rows[4:5, 0:1])
        decay_c = jnp.exp(rows[5:6, 0:1])
        kw = k.astype(F32) * w_col
        c_ref[hd] = decay_c * c_old + lax.dot_general(
            kw.astype(BF16), v_ext, (((0,), (0,)), ((), ())), preferred_element_type=F32)


def _mlstm(proj, gates, gate_bias):
    s_len = proj.shape[0]
    lc = ML_TILE
    nt = s_len // lc
    n_rows = 2 * ML_HEADS
    gt = gates.T.reshape(2, 2, ML_HEADS, s_len)
    bias = gate_bias.reshape(2, 2, ML_HEADS)
    gate_rows = pl.pallas_call(
        functools.partial(_mlstm_gate_kernel, lc=lc),
        out_shape=jax.ShapeDtypeStruct((8, n_rows, s_len), F32),
        compiler_params=_cparams(None),
        name="mlstm_gates",
    )(gt[:, 0].reshape(n_rows, s_len), gt[:, 1].reshape(n_rows, s_len),
      jnp.stack([bias[:, 0].reshape(n_rows), bias[:, 1].reshape(n_rows)], axis=1))
    gate_rows = gate_rows.transpose(1, 0, 2)

    t_idx = jnp.arange(lc)[:, None]
    s_idx = jnp.arange(lc)[None, :]
    mask = jnp.stack([jnp.where(s_idx <= t_idx, 0.0, NEG), jnp.where(s_idx >= t_idx, 0.0, NEG)]).astype(F32)

    hg = ML_HEAD_GROUP
    groups = ML_HEADS // hg

    def tile_of(dg, t):
        return jnp.where(dg >= groups, nt - 1 - t, t)

    assert OFF_MQ % (hg * ML_QK_DIM) == 0 and OFF_MK % (hg * ML_QK_DIM) == 0
    assert OFF_MV % (hg * ML_V_DIM) == 0
    qb, kb = OFF_MQ // (hg * ML_QK_DIM), OFF_MK // (hg * ML_QK_DIM)
    vb = OFF_MV // (hg * ML_V_DIM)
    return pl.pallas_call(
        _mlstm_scan_kernel,
        grid=(2 * groups, nt),
        in_specs=[pl.BlockSpec((lc, hg * ML_QK_DIM), lambda dg, t: (tile_of(dg, t), qb + dg % groups)),
                  pl.BlockSpec((lc, hg * ML_QK_DIM), lambda dg, t: (tile_of(dg, t), kb + dg % groups)),
                  pl.BlockSpec((lc, hg * ML_V_DIM), lambda dg, t: (tile_of(dg, t), vb + dg % groups)),
                  pl.BlockSpec((hg, 8, lc), lambda dg, t: (dg, 0, tile_of(dg, t))),
                  pl.BlockSpec((1, lc, lc), lambda dg, t: (dg // groups, 0, 0))],
        out_specs=pl.BlockSpec((1, lc, hg * ML_V_DIM),
                               lambda dg, t: (dg // groups, tile_of(dg, t), dg % groups)),
        out_shape=jax.ShapeDtypeStruct((2, s_len, ML_V_W), BF16),
        scratch_shapes=[pltpu.VMEM((hg, ML_QK_DIM, ML_V_DIM + LANES), F32)],
        compiler_params=_cparams(("parallel", "arbitrary")),
        name="mlstm_scan",
    )(proj, proj, proj, gate_rows, mask)


def _mlstm_out_kernel(h_ref, mo_ref, g_ref, o_ref):
    h = h_ref[0].astype(F32) + h_ref[1].astype(F32)
    y = h * lax.rsqrt(jnp.mean(h * h, axis=-1, keepdims=True) + EPS) * g_ref[0]
    o_ref[...] = (y * jax.nn.sigmoid(mo_ref[...].astype(F32))).astype(o_ref.dtype)


def _mlstm_out(hs, proj, head_gain, tm=512):
    s_len = proj.shape[0]
    ob = OFF_MO // ML_V_DIM
    return pl.pallas_call(
        _mlstm_out_kernel,
        grid=(s_len // tm, ML_HEADS),
        in_specs=[pl.BlockSpec((2, tm, ML_V_DIM), lambda i, h: (0, i, h)),
                  pl.BlockSpec((tm, ML_V_DIM), lambda i, h: (i, ob + h)),
                  pl.BlockSpec((1, 1, ML_V_DIM), lambda i, h: (h, 0, 0))],
        out_specs=pl.BlockSpec((tm, ML_V_DIM), lambda i, h: (i, h)),
        out_shape=jax.ShapeDtypeStruct((s_len, ML_V_W), BF16),
        compiler_params=_cparams(("parallel", "parallel")),
        name="mlstm_out",
    )(hs, proj, head_gain.reshape(ML_HEADS, 1, ML_V_DIM))


def _branch_kernel(att_ref, mem_ref, wa_ref, wm_ref, ga_ref, gm_ref, ba_ref, bm_ref, o_ref,
                   wab_ref, wmb_ref):
    @pl.when(pl.program_id(1) == 0)
    def _():
        wab_ref[...] = wa_ref[0].astype(BF16)
        wmb_ref[...] = wm_ref[0].astype(BF16)

    ya = jnp.dot(att_ref[...], wab_ref[...], preferred_element_type=F32)
    ym = jnp.dot(mem_ref[...], wmb_ref[...], preferred_element_type=F32)
    g_a = jax.nn.sigmoid(ga_ref[...].astype(F32) + ba_ref[0])
    g_m = jax.nn.sigmoid(gm_ref[...].astype(F32) + bm_ref[0])
    o_ref[...] = (g_a * ya + g_m * ym).astype(o_ref.dtype)


def _branch_mix(att, mem, w_a, w_m, layer, proj, gate_bias, tm=512, tn=512):
    m = att.shape[0]
    ka, km, d = w_a.shape[1], w_m.shape[1], w_a.shape[2]
    bias = gate_bias.reshape(-1, 1, d)
    gab = OFF_GA // tn
    gmb = (OFF_GA + d) // tn
    return pl.pallas_call(
        _branch_kernel,
        grid=(d // tn, m // tm),
        in_specs=[pl.BlockSpec((tm, ka), lambda j, i: (i, 0)),
                  pl.BlockSpec((tm, km), lambda j, i: (i, 0)),
                  pl.BlockSpec((1, ka, tn), lambda j, i: (layer, 0, j)),
                  pl.BlockSpec((1, km, tn), lambda j, i: (layer, 0, j)),
                  pl.BlockSpec((tm, tn), lambda j, i: (i, gab + j)),
                  pl.BlockSpec((tm, tn), lambda j, i: (i, gmb + j)),
                  pl.BlockSpec((1, 1, tn), lambda j, i: (2 * layer, 0, j)),
                  pl.BlockSpec((1, 1, tn), lambda j, i: (2 * layer + 1, 0, j))],
        out_specs=pl.BlockSpec((tm, tn), lambda j, i: (i, j)),
        out_shape=jax.ShapeDtypeStruct((m, d), BF16),
        scratch_shapes=[pltpu.VMEM((ka, tn), BF16), pltpu.VMEM((km, tn), BF16)],
        compiler_params=_cparams(("parallel", "arbitrary")),
        name="branch_mix",
    )(att, mem, w_a, w_m, proj, proj, bias, bias)


def _pack_bf16_pair(lo, hi):
    lo_bits = lax.bitcast_convert_type(lo.astype(BF16).astype(F32), jnp.uint32)
    hi_bits = lax.bitcast_convert_type(hi.astype(BF16).astype(F32), jnp.uint32)
    return (lo_bits >> 16) | (hi_bits & jnp.uint32(0xFFFF0000))


def _unpack_bf16_pair(word):
    lo = lax.bitcast_convert_type(word << 16, F32)
    hi = lax.bitcast_convert_type(word & jnp.uint32(0xFFFF0000), F32)
    return lo, hi


def _norm_router_kernel(x_ref, g_ref, wr_ref, br_ref, h_ref, lg_ref):
    x = x_ref[...]
    h = x * lax.rsqrt(jnp.mean(x * x, axis=-1, keepdims=True) + EPS) * g_ref[...]
    half = h.shape[1] // 2
    h_ref[...] = _pack_bf16_pair(h[:, :half], h[:, half:])
    lg_ref[...] = jnp.dot(h, wr_ref[...], preferred_element_type=F32,
                          precision=lax.Precision.HIGHEST) + br_ref[...]


def _norm_router(x, gain, w_r, b_r, tm=256):
    m, d = x.shape
    n = w_r.shape[1]
    return pl.pallas_call(
        _norm_router_kernel,
        grid=(m // tm,),
        in_specs=[pl.BlockSpec((tm, d), lambda i: (i, 0)),
                  pl.BlockSpec((1, d), lambda i: (0, 0)),
                  pl.BlockSpec((d, n), lambda i: (0, 0)),
                  pl.BlockSpec((1, n), lambda i: (0, 0))],
        out_specs=[pl.BlockSpec((tm, d // 2), lambda i: (i, 0)),
                   pl.BlockSpec((tm, n), lambda i: (i, 0))],
        out_shape=[jax.ShapeDtypeStruct((m, d // 2), jnp.uint32),
                   jax.ShapeDtypeStruct((m, n), F32)],
        compiler_params=_cparams(("parallel",)),
        name="norm_router",
    )(x, gain.reshape(1, d), w_r, b_r)


def _moe_kernel(blk_e_ref, cnt_ref, first_ref, tok_ref, h_hbm, wg_ref, wu_ref,
                wd_lo_ref, wd_hi_ref, y_ref, xg_ref, xb_ref, act_ref, sem, *, n_ff_tiles, n_blk):
    b = pl.program_id(0)
    j = pl.program_id(1)
    rows = MOE_ROWS
    tf = MOE_FF_TILE
    grp = DMA_ISSUE_UNROLL
    cnt = cnt_ref[b]
    used = cnt > 0
    up_phase = j < n_ff_tiles
    n_sub = (cnt + MOE_SUB_ROWS - 1) // MOE_SUB_ROWS

    def row_copy(i, tok):
        return pltpu.make_async_copy(h_hbm.at[pl.ds(tok, 1)], xg_ref.at[pl.ds(i, 1)], sem)

    def issue_rows(blk):
        first = first_ref[blk]
        last_entry = tok_ref.shape[0] - 1

        def body(gi, carry):
            for r in range(grp):
                i = gi * grp + r
                row_copy(i, tok_ref[jnp.minimum(first + i, last_entry)]).start()
            return carry
        lax.fori_loop(0, (cnt_ref[blk] + grp - 1) // grp, body, 0)

    def wait_rows(count):
        def body(gi, carry):
            for r in range(grp):
                row_copy(gi * grp + r, 0).wait()
            return carry
        lax.fori_loop(0, (count + grp - 1) // grp, body, 0)

    @pl.when(jnp.logical_and(used, j == 0))
    def _():
        @pl.when(b == 0)
        def _():
            xg_ref[...] = jnp.zeros_like(xg_ref)
            issue_rows(b)

        wait_rows(cnt)
        half = xg_ref.shape[1]
        lo, hi = _unpack_bf16_pair(xg_ref[...])
        xb_ref[:, 0:half] = lo.astype(BF16)
        xb_ref[:, half:2 * half] = hi.astype(BF16)
        @pl.when(b + 1 < n_blk)
        def _():
            issue_rows(b + 1)

    for m in range(MOE_SUB_ROWS, rows + 1, MOE_SUB_ROWS):
        live = jnp.logical_and(used, n_sub == m // MOE_SUB_ROWS)

        @pl.when(jnp.logical_and(live, up_phase))
        def _(m=m):
            x = xb_ref[0:m, :]
            g = jnp.dot(x, wg_ref[0, 0].astype(BF16), preferred_element_type=F32)
            u = jnp.dot(x, wu_ref[0, 0].astype(BF16), preferred_element_type=F32)
            act_ref[j, 0:m, :] = (g * jax.nn.sigmoid(g) * u).astype(BF16)

        @pl.when(jnp.logical_and(live, jnp.logical_not(up_phase)))
        def _(m=m):
            def down(wd_ref):
                acc = jnp.dot(act_ref[0, 0:m, :], wd_ref[0, 0, 0:tf, :].astype(BF16),
                              preferred_element_type=F32)
                for t in range(1, n_ff_tiles):
                    acc = acc + jnp.dot(act_ref[t, 0:m, :],
                                        wd_ref[0, 0, t * tf:(t + 1) * tf, :].astype(BF16),
                                        preferred_element_type=F32)
                return acc

            y_ref[0:m, :] = _pack_bf16_pair(down(wd_lo_ref), down(wd_hi_ref))
            if m < rows:
                y_ref[m:rows, :] = jnp.zeros((rows - m, y_ref.shape[1]), y_ref.dtype)

    @pl.when(jnp.logical_and(jnp.logical_not(used), jnp.logical_not(up_phase)))
    def _():
        y_ref[...] = jnp.zeros_like(y_ref)


def _moe_experts(h2p, blk_expert, blk_cnt, blk_first, tok_sorted, w_gate, w_up, w_down, layer):
    d = 2 * h2p.shape[1]
    d_ff = w_gate.shape[3]
    rows = MOE_ROWS
    n_blk = blk_expert.shape[0]
    tf = MOE_FF_TILE
    tn = MOE_OUT_TILE
    n_f = d_ff // tf
    n_o = d // 2 // tn

    def ff_tile(b, j, cnt):
        return jnp.where(cnt[b] > 0, jnp.minimum(j, n_f - 1), n_f - 1)

    def out_tile(b, j):
        return jnp.clip(j - n_f, 0, n_o - 1)

    def w_down_block(b, j, be, cnt, hi):
        stay = jnp.logical_or(j < n_f, cnt[b] == 0)
        expert = jnp.where(j < n_f, be[jnp.maximum(b - 1, 0)], be[b])
        return layer, expert, 0, hi * n_o + jnp.where(stay, n_o - 1, j - n_f)

    grid_spec = pltpu.PrefetchScalarGridSpec(
        num_scalar_prefetch=4,
        grid=(n_blk, n_f + n_o),
        in_specs=[pl.BlockSpec(memory_space=pl.ANY),
                  pl.BlockSpec((1, 1, d, tf),
                               lambda b, j, be, cnt, *_: (layer, be[b], 0, ff_tile(b, j, cnt))),
                  pl.BlockSpec((1, 1, d, tf),
                               lambda b, j, be, cnt, *_: (layer, be[b], 0, ff_tile(b, j, cnt))),
                  pl.BlockSpec((1, 1, d_ff, tn),
                               lambda b, j, be, cnt, *_: w_down_block(b, j, be, cnt, 0)),
                  pl.BlockSpec((1, 1, d_ff, tn),
                               lambda b, j, be, cnt, *_: w_down_block(b, j, be, cnt, 1))],
        out_specs=pl.BlockSpec((rows, tn), lambda b, j, *_: (b, out_tile(b, j))),
        scratch_shapes=[pltpu.VMEM((rows, d // 2), jnp.uint32),
                        pltpu.VMEM((rows, d), BF16),
                        pltpu.VMEM((n_f, rows, tf), BF16),
                        pltpu.SemaphoreType.DMA(())],
    )
    return pl.pallas_call(
        functools.partial(_moe_kernel, n_ff_tiles=n_f, n_blk=n_blk),
        grid_spec=grid_spec,
        out_shape=jax.ShapeDtypeStruct((n_blk * rows, d // 2), jnp.uint32),
        compiler_params=_cparams(("arbitrary", "arbitrary")),
        name="moe_experts",
    )(blk_expert, blk_cnt, blk_first, tok_sorted, h2p, w_gate, w_up, w_down, w_down)


def _combine_kernel(pos_ref, pos_next_ref, x_ref, w_ref, y_hbm, o_ref, buf_ref, sem, *, n_steps):
    rows = COMBINE_ROWS
    n_copy = TOP_K * rows
    step = pl.program_id(0)
    slot = step % 2

    def row_copy(s, i, src):
        return pltpu.make_async_copy(y_hbm.at[pl.ds(src, 1)], buf_ref.at[s, pl.ds(i, 1)], sem.at[s])

    def issue(s, idx_ref):
        def body(i, carry):
            row_copy(s, i, idx_ref[0, 0, i]).start()
            return carry
        lax.fori_loop(0, n_copy, body, 0, unroll=DMA_ISSUE_UNROLL)

    @pl.when(step == 0)
    def _():
        issue(0, pos_ref)

    @pl.when(step + 1 < n_steps)
    def _():
        issue(1 - slot, pos_next_ref)

    def drain(i, carry):
        row_copy(slot, i, 0).wait()
        return carry

    lax.fori_loop(0, n_copy, drain, 0, unroll=DMA_ISSUE_UNROLL)
    half = x_ref.shape[1] // 2
    lo0, hi0 = _unpack_bf16_pair(buf_ref[slot, pl.ds(0, rows), :])
    lo1, hi1 = _unpack_bf16_pair(buf_ref[slot, pl.ds(rows, rows), :])
    w0 = w_ref[:, 0:1]
    w1 = w_ref[:, 1:2]
    o_ref[:, 0:half] = x_ref[:, 0:half] + (w0 * lo0 + w1 * lo1)
    o_ref[:, half:2 * half] = x_ref[:, half:2 * half] + (w0 * hi0 + w1 * hi1)


def _moe_combine(x1, ys, pos, weights):
    n_tok, d = x1.shape
    rows = COMBINE_ROWS
    n_blk = n_tok // rows
    pos_b = pos.reshape(n_blk, rows, TOP_K).transpose(0, 2, 1).reshape(n_blk, 1, TOP_K * rows)
    idx_block = (1, 1, TOP_K * rows)
    return pl.pallas_call(
        functools.partial(_combine_kernel, n_steps=n_blk),
        grid=(n_blk,),
        in_specs=[pl.BlockSpec(idx_block, lambda i: (i, 0, 0), memory_space=pltpu.SMEM),
                  pl.BlockSpec(idx_block, lambda i: (jnp.minimum(i + 1, n_blk - 1), 0, 0),
                               memory_space=pltpu.SMEM),
                  pl.BlockSpec((rows, d), lambda i: (i, 0)),
                  pl.BlockSpec((rows, TOP_K), lambda i: (i, 0)),
                  pl.BlockSpec(memory_space=pl.ANY)],
        out_specs=pl.BlockSpec((rows, d), lambda i: (i, 0)),
        out_shape=jax.ShapeDtypeStruct((n_tok, d), F32),
        scratch_shapes=[pltpu.VMEM((2, TOP_K * rows, d // 2), jnp.uint32),
                        pltpu.SemaphoreType.DMA((2,))],
        compiler_params=_cparams(("arbitrary",)),
        name="moe_combine",
    )(pos_b, pos_b, x1, weights, ys)


def _route(logits, n_tok):
    glog = logits[:, :N_GROUPS]
    gprob = jax.nn.softmax(glog, axis=-1)
    g_sel = jnp.argmax(glog, axis=-1)
    p_g = jnp.take_along_axis(gprob, g_sel[:, None], axis=1)[:, 0]
    elog = logits[:, N_GROUPS:N_GROUPS + N_EXPERTS].reshape(n_tok, N_GROUPS, EXPERTS_PER_GROUP)
    elog = jnp.take_along_axis(elog, g_sel[:, None, None], axis=1)[:, 0]
    top_v, top_i = lax.top_k(elog, TOP_K)
    weights = p_g[:, None] * jax.nn.softmax(top_v, axis=-1)
    expert_ids = (g_sel[:, None] * EXPERTS_PER_GROUP + top_i).astype(jnp.int32)

    n_assign = n_tok * TOP_K
    rows = MOE_ROWS
    flat_e = expert_ids.reshape(n_assign)
    a_ids = jnp.arange(n_assign, dtype=jnp.int32)
    skey = jnp.sort(flat_e * n_assign + a_ids)
    order = skey % n_assign
    inv = jnp.argsort(order).astype(jnp.int32)
    experts = jnp.arange(N_EXPERTS, dtype=jnp.int32)
    counts = jnp.sum((flat_e[:, None] == experts[None, :]).astype(jnp.int32), axis=0)
    blocks = (counts + rows - 1) // rows
    pad_end = jnp.cumsum(blocks * rows)
    pad_start = pad_end - blocks * rows
    start = jnp.cumsum(counts) - counts
    n_slots = n_assign + N_EXPERTS * rows
    n_blk = n_slots // rows
    n_used = jnp.sum(blocks).astype(jnp.int32)
    blk_ids = jnp.arange(n_blk, dtype=jnp.int32)
    blk_expert = jnp.minimum(
        jnp.sum((pad_end[None, :] <= (blk_ids * rows)[:, None]).astype(jnp.int32), axis=1),
        N_EXPERTS - 1)
    blk_cnt = jnp.clip(counts[blk_expert] - (blk_ids * rows - pad_start[blk_expert]), 0, rows)
    blk_cnt = jnp.where(blk_ids < n_used, blk_cnt, 0).astype(jnp.int32)
    blk_first = (start[blk_expert] + blk_ids * rows - pad_start[blk_expert]).astype(jnp.int32)
    tok_sorted = (order // TOP_K).astype(jnp.int32)
    blk_expert = jnp.where(blk_ids < n_used, blk_expert,
                           blk_expert[jnp.maximum(n_used - 1, 0)]).astype(jnp.int32)
    shift = jnp.sum(jnp.where(flat_e[:, None] == experts[None, :], (pad_start - start)[None, :], 0),
                    axis=1)
    pos = (inv + shift).astype(jnp.int32).reshape(n_tok, TOP_K)
    return blk_expert, blk_cnt, blk_first, tok_sorted, pos, weights


def kernel(x, norm1_gain, w_in, attn_q_norm_gain, attn_k_norm_gain, mlstm_gate_bias,
           mlstm_head_norm_gain, branch_gate_bias, w_attn_branch, w_mlstm_branch, w_out,
           norm2_gain, w_router_group, b_router_group, w_router_expert, b_router_expert,
           w_expert_gate, w_expert_up, w_expert_down):
    bsz, s_len, d = x.shape
    assert bsz == 1
    depth = w_in.shape[0]
    xs = x.reshape(s_len, d)
    w_in_t = jnp.swapaxes(w_in, 1, 2)
    n_proj = OFF_GA + 2 * d
    tn = 512

    def proj_rows(j):
        return j * tn + jnp.where(j * tn >= OFF_GA, N_ML_GATES, 0)

    for l in range(depth):
        h = _rmsnorm(xs, norm1_gain[l])
        proj = _matmul_nt(h, w_in_t, l, proj_rows, n_proj, BF16, 1024, tn, "in_proj")
        gates = _matmul_nt(h, w_in_t, l, lambda j: OFF_MG + j * LANES, LANES, F32, 1024, LANES,
                           "gate_proj")[:, :N_ML_GATES]

        att = _attention(proj, attn_q_norm_gain[l], attn_k_norm_gain[l])

        hs = _mlstm(proj, gates, mlstm_gate_bias[l])
        mem = _mlstm_out(hs, proj, mlstm_head_norm_gain[l])

        merged = _branch_mix(att, mem, w_attn_branch, w_mlstm_branch, l, proj, branch_gate_bias)
        x1 = _matmul_residual(merged, w_out, l, xs, 1024, 512)

        w_r = jnp.concatenate([w_router_group[l], w_router_expert[l]], axis=1)
        b_r = jnp.concatenate([b_router_group[l], b_router_expert[l]])
        n_r = w_r.shape[1]
        w_r = jnp.pad(w_r, ((0, 0), (0, LANES - n_r)))
        b_r = jnp.pad(b_r, (0, LANES - n_r)).reshape(1, LANES)
        h2, logits = _norm_router(x1, norm2_gain[l], w_r, b_r)

        blk_expert, blk_cnt, blk_first, tok_sorted, pos, weights = _route(logits, s_len)
        ys = _moe_experts(h2, blk_expert, blk_cnt, blk_first, tok_sorted,
                          w_expert_gate, w_expert_up, w_expert_down, l)
        xs = _moe_combine(x1, ys, pos, weights)
    return xs.reshape(bsz, s_len, d)
```

```python
import functools

import jax
import jax.numpy as jnp
from jax import lax
from jax.experimental import pallas as pl
from jax.experimental.pallas import tpu as pltpu

F32 = jnp.float32
BF16 = jnp.bfloat16

EPS = 1e-6
NEG = -1e30
LANES = 128

ATT_PATTERNS = ((128, 1), (512, 4), (2048, 16))
N_ATT_GROUPS = 3
ATT_HEADS = 8
ATT_HEAD_DIM = 128
ATT_GROUP_W = ATT_HEADS * ATT_HEAD_DIM
ATT_W = N_ATT_GROUPS * ATT_GROUP_W
ROPE_THETA = 10000.0
ATT_Q_TILE = 128
ATT_TILE_UNROLL = 8

ML_HEADS = 8
ML_QK_DIM = 256
ML_V_DIM = 512
ML_QK_W = ML_HEADS * ML_QK_DIM
ML_V_W = ML_HEADS * ML_V_DIM
N_ML_GATES = 4 * ML_HEADS
GATE_SOFTCAP = 15.0
ML_TILE = 256
ML_HEAD_GROUP = 2
ML_OUT_HEADS = 2

N_GROUPS = 8
EXPERTS_PER_GROUP = 8
N_EXPERTS = N_GROUPS * EXPERTS_PER_GROUP
TOP_K = 2
MOE_ROWS = 512
MOE_SUB_ROWS = 128
MOE_FF_TILE = 256
MOE_OUT_TILE = 1024
COMBINE_ROWS = 256
DMA_ISSUE_UNROLL = 8

OFF_AQ = 0
OFF_AK = OFF_AQ + ATT_W
OFF_AV = OFF_AK + ATT_W
OFF_MQ = OFF_AV + ATT_W
OFF_MK = OFF_MQ + ML_QK_W
OFF_MV = OFF_MK + ML_QK_W
OFF_MO = OFF_MV + ML_V_W
OFF_MG = OFF_MO + ML_V_W
OFF_GA = OFF_MG

VMEM_LIMIT = 56 * 1024 * 1024


def _cparams(sem, vmem=VMEM_LIMIT):
    return pltpu.CompilerParams(dimension_semantics=sem, vmem_limit_bytes=vmem)


def _rmsnorm_kernel(x_ref, g_ref, o_ref):
    x = x_ref[...]
    ms = jnp.mean(x * x, axis=-1, keepdims=True)
    o_ref[...] = (x * lax.rsqrt(ms + EPS) * g_ref[...]).astype(o_ref.dtype)


def _rmsnorm(x, gain, tm=256):
    m, d = x.shape
    return pl.pallas_call(
        _rmsnorm_kernel,
        grid=(m // tm,),
        in_specs=[pl.BlockSpec((tm, d), lambda i: (i, 0)),
                  pl.BlockSpec((1, d), lambda i: (0, 0))],
        out_specs=pl.BlockSpec((tm, d), lambda i: (i, 0)),
        out_shape=jax.ShapeDtypeStruct((m, d), BF16),
        compiler_params=_cparams(("parallel",)),
        name="rmsnorm",
    )(x, gain.reshape(1, d))


def _mm_kernel(a_ref, w_ref, o_ref, wb_ref):
    @pl.when(pl.program_id(1) == 0)
    def _():
        wb_ref[...] = w_ref[0].astype(BF16)

    o_ref[...] = jnp.dot(a_ref[...], wb_ref[...], preferred_element_type=F32).astype(o_ref.dtype)


def _matmul(a, w, layer, col0, n, out_dtype, tm, tn, name):
    m, k = a.shape
    cb = col0 // tn
    return pl.pallas_call(
        _mm_kernel,
        grid=(n // tn, m // tm),
        in_specs=[pl.BlockSpec((tm, k), lambda j, i: (i, 0)),
                  pl.BlockSpec((1, k, tn), lambda j, i: (layer, 0, cb + j))],
        out_specs=pl.BlockSpec((tm, tn), lambda j, i: (i, j)),
        out_shape=jax.ShapeDtypeStruct((m, n), out_dtype),
        scratch_shapes=[pltpu.VMEM((k, tn), BF16)],
        compiler_params=_cparams(("parallel", "arbitrary")),
        name=name,
    )(a, w)


def _mm_nt_kernel(a_ref, wt_ref, o_ref, wb_ref):
    @pl.when(pl.program_id(1) == 0)
    def _():
        wb_ref[...] = wt_ref[0].astype(BF16)

    o_ref[...] = lax.dot_general(a_ref[...], wb_ref[...], (((1,), (1,)), ((), ())),
                                 preferred_element_type=F32).astype(o_ref.dtype)


def _matmul_nt(a, w_t, layer, row_of_tile, n, out_dtype, tm, tn, name):
    m, k = a.shape
    return pl.pallas_call(
        _mm_nt_kernel,
        grid=(n // tn, m // tm),
        in_specs=[pl.BlockSpec((tm, k), lambda j, i: (i, 0)),
                  pl.BlockSpec((pl.Element(1), pl.Element(tn), pl.Element(k)),
                               lambda j, i: (layer, pl.multiple_of(row_of_tile(j), 8), 0))],
        out_specs=pl.BlockSpec((tm, tn), lambda j, i: (i, j)),
        out_shape=jax.ShapeDtypeStruct((m, n), out_dtype),
        scratch_shapes=[pltpu.VMEM((tn, k), BF16)],
        compiler_params=_cparams(("parallel", "arbitrary")),
        name=name,
    )(a, w_t)


def _mm_res_kernel(a_ref, w_ref, r_ref, o_ref, wb_ref):
    @pl.when(pl.program_id(1) == 0)
    def _():
        wb_ref[...] = w_ref[0].astype(BF16)

    o_ref[...] = r_ref[...] + jnp.dot(a_ref[...], wb_ref[...], preferred_element_type=F32)


def _matmul_residual(a, w, layer, res, tm, tn):
    m, k = a.shape
    n = w.shape[2]
    return pl.pallas_call(
        _mm_res_kernel,
        grid=(n // tn, m // tm),
        in_specs=[pl.BlockSpec((tm, k), lambda j, i: (i, 0)),
                  pl.BlockSpec((1, k, tn), lambda j, i: (layer, 0, j)),
                  pl.BlockSpec((tm, tn), lambda j, i: (i, j))],
        out_specs=pl.BlockSpec((tm, tn), lambda j, i: (i, j)),
        out_shape=jax.ShapeDtypeStruct((m, n), F32),
        scratch_shapes=[pltpu.VMEM((k, tn), BF16)],
        compiler_params=_cparams(("parallel", "arbitrary")),
        name="out_proj",
    )(a, w, res)


def _attn_kernel(q_ref, k_ref, v_ref, tab_ref, qg_ref, kg_ref, o_ref,
                 qn_ref, kn_ref, vf_ref, acc_ref, lse_ref, band_ref, *, s_len):
    grp = pl.program_id(1)
    tq = ATT_Q_TILE
    hd = ATT_HEAD_DIM
    half = hd // 2
    low = lax.broadcasted_iota(jnp.int32, (tq, hd), 1) < half
    row = lax.broadcasted_iota(jnp.int32, (hd, hd), 0)
    col = lax.broadcasted_iota(jnp.int32, (hd, hd), 1)
    ones = jnp.ones((hd, hd), BF16)
    rot = jnp.where(row == col + half, -1.0, jnp.where(col == row + half, 1.0, 0.0)).astype(BF16)

    def prep(i, carry):
        rows = pl.ds(pl.multiple_of(i * tq, tq), tq)
        tab = tab_ref[rows, :]
        swapped = pltpu.roll(tab, half, axis=1)
        c = jnp.where(low, tab, swapped)
        s = jnp.where(low, swapped, tab)
        for src, g_ref, dst in ((q_ref, qg_ref, qn_ref), (k_ref, kg_ref, kn_ref)):
            xf = src[rows, :].astype(F32)
            ssq = jnp.dot((xf * xf).astype(BF16), ones, preferred_element_type=F32)
            y = xf * lax.rsqrt(ssq * (1.0 / hd) + EPS) * g_ref[0]
            y_rot = jnp.dot(y.astype(BF16), rot, preferred_element_type=F32)
            dst[rows, :] = y * c + y_rot * s
        vf_ref[rows, :] = v_ref[rows, :].astype(F32)
        return carry

    lax.fori_loop(0, s_len // tq, prep, 0, unroll=4)

    def run_group(window, dilation, first):
        radius = window // (2 * dilation)
        sub_len = s_len // dilation
        win = tq + 2 * radius
        tiles_per_class = sub_len // tq
        rel = (lax.broadcasted_iota(jnp.int32, (tq, win), 1)
               - lax.broadcasted_iota(jnp.int32, (tq, win), 0))
        for case in range(3):
            band_ref[case] = jnp.where(jnp.abs(rel - case * radius) <= radius, 0.0, NEG)

        def rows_of(start, size):
            if dilation == 1:
                return pl.ds(start, size)
            return pl.ds(start, size, stride=dilation)

        def tile(idx, carry):
            r = idx // tiles_per_class
            q0 = (idx % tiles_per_class) * tq
            k0 = jnp.clip(q0 - radius, 0, sub_len - win)
            q_rows = rows_of(r + dilation * q0, tq)
            k_rows = rows_of(r + dilation * k0, win)
            q = qn_ref[q_rows, :].astype(BF16)
            k = kn_ref[k_rows, :].astype(BF16)
            v = vf_ref[k_rows, :].astype(BF16)
            s = lax.dot_general(q, k, (((1,), (1,)), ((), ())), preferred_element_type=F32)
            s = s + band_ref[(q0 - k0) // radius]
            m = jnp.max(s, axis=-1, keepdims=True)
            p = jnp.exp(s - m)
            den = jnp.sum(p, axis=-1, keepdims=True)
            o = jnp.dot(p.astype(BF16), v, preferred_element_type=F32) / den
            lse = jnp.broadcast_to(m + jnp.log(den), (tq, ATT_HEAD_DIM))
            if first:
                acc_ref[q_rows, :] = o
                lse_ref[q_rows, :] = lse
            else:
                old_l = lse_ref[q_rows, :]
                top = jnp.maximum(old_l, lse)
                a = jnp.exp(old_l - top)
                b = jnp.exp(lse - top)
                tot = a + b
                acc_ref[q_rows, :] = (a * acc_ref[q_rows, :] + b * o) / tot
                lse_ref[q_rows, :] = top + jnp.log(tot)
            return carry

        lax.fori_loop(0, s_len // tq, tile, 0, unroll=ATT_TILE_UNROLL)

    for gi, (window, dilation) in enumerate(ATT_PATTERNS):
        @pl.when(grp == gi)
        def _(window=window, dilation=dilation, gi=gi):
            run_group(window, dilation, gi == 0)

    @pl.when(grp == N_ATT_GROUPS - 1)
    def _():
        o_ref[...] = acc_ref[...].astype(o_ref.dtype)


def _attention(proj, q_gain, k_gain):
    s_len = proj.shape[0]
    half = ATT_HEAD_DIM // 2
    inv_freq = ROPE_THETA ** (-jnp.arange(half, dtype=F32) / half)
    ang = jnp.arange(s_len).astype(F32)[:, None] * inv_freq[None, :]
    table = jnp.concatenate([jnp.cos(ang), jnp.sin(ang)], axis=1)
    qb, kb, vb = OFF_AQ // LANES, OFF_AK // LANES, OFF_AV // LANES
    blk = (s_len, ATT_HEAD_DIM)
    radii = {window // (2 * dilation) for window, dilation in ATT_PATTERNS}
    assert len(radii) == 1
    win = ATT_Q_TILE + 2 * radii.pop()
    gain_spec = pl.BlockSpec((1, 1, ATT_HEAD_DIM), lambda h, g: (g, 0, 0))
    q_scaled = q_gain.reshape(N_ATT_GROUPS, 1, ATT_HEAD_DIM) * (ATT_HEAD_DIM ** -0.5)
    return pl.pallas_call(
        functools.partial(_attn_kernel, s_len=s_len),
        grid=(ATT_HEADS, N_ATT_GROUPS),
        in_specs=[pl.BlockSpec(blk, lambda h, g: (0, qb + g * ATT_HEADS + h)),
                  pl.BlockSpec(blk, lambda h, g: (0, kb + g * ATT_HEADS + h)),
                  pl.BlockSpec(blk, lambda h, g: (0, vb + g * ATT_HEADS + h)),
                  pl.BlockSpec(blk, lambda h, g: (0, 0)),
                  gain_spec, gain_spec],
        out_specs=pl.BlockSpec(blk, lambda h, g: (0, h)),
        out_shape=jax.ShapeDtypeStruct((s_len, ATT_GROUP_W), BF16),
        scratch_shapes=[pltpu.VMEM(blk, F32)] * 5 + [pltpu.VMEM((3, ATT_Q_TILE, win), F32)],
        compiler_params=_cparams(("parallel", "arbitrary")),
        name="dilated_attn",
    )(proj, proj, proj, table, q_scaled, k_gain.reshape(N_ATT_GROUPS, 1, ATT_HEAD_DIM))


def _tile_scan(x, pos, lc, combine, reverse, fill):
    n = x.shape[1]
    k = 1
    while k < lc:
        if reverse:
            shifted = pltpu.roll(x, n - k, axis=1)
            ok = pos < lc - k
        else:
            shifted = pltpu.roll(x, k, axis=1)
            ok = pos >= k
        x = combine(x, jnp.where(ok, shifted, fill))
        k *= 2
    return x


def _mlstm_gate_kernel(ig_ref, f_ref, bias_ref, o_ref, *, lc):
    n_rows, s_len = ig_ref.shape
    n_tiles = s_len // lc
    shape = (n_rows, s_len)
    bw = lax.broadcasted_iota(jnp.int32, shape, 0) >= ML_HEADS
    lane = lax.broadcasted_iota(jnp.int32, shape, 1)
    pos = lane % lc
    tile = lane // lc

    ig = GATE_SOFTCAP * jnp.tanh((ig_ref[...] + bias_ref[:, 0:1]) / GATE_SOFTCAP)
    f = GATE_SOFTCAP * jnp.tanh((f_ref[...] + bias_ref[:, 1:2]) / GATE_SOFTCAP)
    lf = jnp.minimum(f, 0.0) - jnp.log1p(jnp.exp(-jnp.abs(f)))

    pre = _tile_scan(lf, pos, lc, jnp.add, False, 0.0)
    suf = _tile_scan(lf, pos, lc, jnp.add, True, 0.0)
    b = jnp.where(bw, suf, pre)
    tot = pre + suf - lf
    a = ig - b
    pmax = _tile_scan(a, pos, lc, jnp.maximum, False, -jnp.inf)
    smax = _tile_scan(a, pos, lc, jnp.maximum, True, -jnp.inf)
    cm = jnp.where(bw, smax, pmax)
    cmax = jnp.maximum(pmax, smax)

    def tiles_scan(reverse):
        alpha, beta = tot, cmax + tot
        step = 1
        while step < n_tiles:
            sh = step * lc
            if reverse:
                pa, pb = pltpu.roll(alpha, s_len - sh, axis=1), pltpu.roll(beta, s_len - sh, axis=1)
                ok = tile < n_tiles - step
            else:
                pa, pb = pltpu.roll(alpha, sh, axis=1), pltpu.roll(beta, sh, axis=1)
                ok = tile >= step
            alpha, beta = (jnp.where(ok, pa + alpha, alpha),
                           jnp.where(ok, jnp.maximum(pb + alpha, beta), beta))
            step *= 2
        return jnp.maximum(alpha, beta)

    last_f = tiles_scan(False)
    last_r = tiles_scan(True)
    m_last = jnp.where(bw, last_r, last_f)
    prev_f = jnp.where(tile >= 1, pltpu.roll(last_f, lc, axis=1), 0.0)
    prev_r = jnp.where(tile < n_tiles - 1, pltpu.roll(last_r, s_len - lc, axis=1), 0.0)
    m_prev = jnp.where(bw, prev_r, prev_f)
    mm = jnp.maximum(m_prev, cm)

    o_ref[0] = a
    o_ref[1] = mm
    o_ref[2] = b + mm
    o_ref[3] = m_prev
    o_ref[4] = tot - m_last
    o_ref[5] = tot + m_prev - m_last
    o_ref[6] = jnp.zeros(shape, F32)
    o_ref[7] = jnp.zeros(shape, F32)


def _mlstm_scan_kernel(q_ref, k_ref, v_ref, g_ref, mask_ref, o_ref, c_ref):
    dk, dv = ML_QK_DIM, ML_V_DIM

    @pl.when(pl.program_id(1) == 0)
    def _():
        c_ref[...] = jnp.zeros_like(c_ref)

    lc = q_ref.shape[0]
    mask = mask_ref[0]
    ones = jnp.ones((lc, LANES), BF16)
    for hd in range(ML_HEAD_GROUP):
        rows = g_ref[hd]

        def column(r, rows=rows):
            return jnp.broadcast_to(rows[r:r + 1, :], (8, lc)).T[:, 0:1]

        a_row = rows[0:1, :]
        a_col, mm_col, m_col = column(0), column(1), column(2)
        m_prev = rows[3:4, 0:1]
        d_mat = jnp.exp(a_row - mm_col + mask)
        decay_q = jnp.exp(m_prev - mm_col)

        q = q_ref[:, hd * dk:(hd + 1) * dk]
        k = k_ref[:, hd * dk:(hd + 1) * dk] * (dk ** -0.5)
        v_ext = jnp.concatenate([v_ref[:, hd * dv:(hd + 1) * dv], ones], axis=1)
        s = lax.dot_general(q, k, (((1,), (1,)), ((), ())), preferred_element_type=F32) * d_mat
        c_old = c_ref[hd]
        acc = jnp.dot(s.astype(BF16), v_ext, preferred_element_type=F32)
        acc = acc + decay_q * jnp.dot(q, c_old.astype(BF16), preferred_element_type=F32)
        den = acc[:, dv:dv + LANES]
        inv = 1.0 / jnp.maximum(jnp.abs(den), jnp.exp(-m_col))
        for part in range(dv // LANES):
            cols = slice(part * LANES, (part + 1) * LANES)
            o_ref[0, :, hd * dv + part * LANES:hd * dv + (part + 1) * LANES] = (
                acc[:, cols] * inv).astype(o_ref.dtype)

        w_col = jnp.exp(a_col + rows[4:5, 0:1])
        decay_c = jnp.exp(rows[5:6, 0:1])
        kw = k.astype(F32) * w_col
        c_ref[hd] = decay_c * c_old + lax.dot_general(
            kw.astype(BF16), v_ext, (((0,), (0,)), ((), ())), preferred_element_type=F32)


def _mlstm(proj, gates, gate_bias):
    s_len = proj.shape[0]
    lc = ML_TILE
    nt = s_len // lc
    n_rows = 2 * ML_HEADS
    gt = gates.T.reshape(2, 2, ML_HEADS, s_len)
    bias = gate_bias.reshape(2, 2, ML_HEADS)
    gate_rows = pl.pallas_call(
        functools.partial(_mlstm_gate_kernel, lc=lc),
        out_shape=jax.ShapeDtypeStruct((8, n_rows, s_len), F32),
        compiler_params=_cparams(None),
        name="mlstm_gates",
    )(gt[:, 0].reshape(n_rows, s_len), gt[:, 1].reshape(n_rows, s_len),
      jnp.stack([bias[:, 0].reshape(n_rows), bias[:, 1].reshape(n_rows)], axis=1))
    gate_rows = gate_rows.transpose(1, 0, 2)

    t_idx = jnp.arange(lc)[:, None]
    s_idx = jnp.arange(lc)[None, :]
    mask = jnp.stack([jnp.where(s_idx <= t_idx, 0.0, NEG), jnp.where(s_idx >= t_idx, 0.0, NEG)]).astype(F32)

    hg = ML_HEAD_GROUP
    groups = ML_HEADS // hg

    def tile_of(dg, t):
        return jnp.where(dg >= groups, nt - 1 - t, t)

    assert OFF_MQ % (hg * ML_QK_DIM) == 0 and OFF_MK % (hg * ML_QK_DIM) == 0
    assert OFF_MV % (hg * ML_V_DIM) == 0
    qb, kb = OFF_MQ // (hg * ML_QK_DIM), OFF_MK // (hg * ML_QK_DIM)
    vb = OFF_MV // (hg * ML_V_DIM)
    return pl.pallas_call(
        _mlstm_scan_kernel,
        grid=(2 * groups, nt),
        in_specs=[pl.BlockSpec((lc, hg * ML_QK_DIM), lambda dg, t: (tile_of(dg, t), qb + dg % groups)),
                  pl.BlockSpec((lc, hg * ML_QK_DIM), lambda dg, t: (tile_of(dg, t), kb + dg % groups)),
                  pl.BlockSpec((lc, hg * ML_V_DIM), lambda dg, t: (tile_of(dg, t), vb + dg % groups)),
                  pl.BlockSpec((hg, 8, lc), lambda dg, t: (dg, 0, tile_of(dg, t))),
                  pl.BlockSpec((1, lc, lc), lambda dg, t: (dg // groups, 0, 0))],
        out_specs=pl.BlockSpec((1, lc, hg * ML_V_DIM),
                               lambda dg, t: (dg // groups, tile_of(dg, t), dg % groups)),
        out_shape=jax.ShapeDtypeStruct((2, s_len, ML_V_W), BF16),
        scratch_shapes=[pltpu.VMEM((hg, ML_QK_DIM, ML_V_DIM + LANES), F32)],
        compiler_params=_cparams(("parallel", "arbitrary")),
        name="mlstm_scan",
    )(proj, proj, proj, gate_rows, mask)


def _mlstm_out_kernel(h_ref, mo_ref, g_ref, o_ref):
    for hd in range(ML_OUT_HEADS):
        cols = slice(hd * ML_V_DIM, (hd + 1) * ML_V_DIM)
        h = h_ref[0, :, cols].astype(F32) + h_ref[1, :, cols].astype(F32)
        y = h * lax.rsqrt(jnp.mean(h * h, axis=-1, keepdims=True) + EPS) * g_ref[0, :, cols]
        o_ref[:, cols] = (y * jax.nn.sigmoid(mo_ref[:, cols].astype(F32))).astype(o_ref.dtype)


def _mlstm_out(hs, proj, head_gain, tm=1024):
    s_len = proj.shape[0]
    width = ML_OUT_HEADS * ML_V_DIM
    assert OFF_MO % width == 0
    ob = OFF_MO // width
    return pl.pallas_call(
        _mlstm_out_kernel,
        grid=(s_len // tm, ML_V_W // width),
        in_specs=[pl.BlockSpec((2, tm, width), lambda i, h: (0, i, h)),
                  pl.BlockSpec((tm, width), lambda i, h: (i, ob + h)),
                  pl.BlockSpec((1, 1, width), lambda i, h: (h, 0, 0))],
        out_specs=pl.BlockSpec((tm, width), lambda i, h: (i, h)),
        out_shape=jax.ShapeDtypeStruct((s_len, ML_V_W), BF16),
        compiler_params=_cparams(("parallel", "parallel")),
        name="mlstm_out",
    )(hs, proj, head_gain.reshape(ML_V_W // width, 1, width))


def _branch_kernel(att_ref, mem_ref, wa_ref, wm_ref, ga_ref, gm_ref, ba_ref, bm_ref, o_ref,
                   wab_ref, wmb_ref):
    @pl.when(pl.program_id(1) == 0)
    def _():
        wab_ref[...] = wa_ref[0].astype(BF16)
        wmb_ref[...] = wm_ref[0].astype(BF16)

    ya = jnp.dot(att_ref[...], wab_ref[...], preferred_element_type=F32)
    ym = jnp.dot(mem_ref[...], wmb_ref[...], preferred_element_type=F32)
    g_a = jax.nn.sigmoid(ga_ref[...].astype(F32) + ba_ref[0])
    g_m = jax.nn.sigmoid(gm_ref[...].astype(F32) + bm_ref[0])
    o_ref[...] = (g_a * ya + g_m * ym).astype(o_ref.dtype)


def _branch_mix(att, mem, w_a, w_m, layer, proj, gate_bias, tm=512, tn=512):
    m = att.shape[0]
    ka, km, d = w_a.shape[1], w_m.shape[1], w_a.shape[2]
    bias = gate_bias.reshape(-1, 1, d)
    gab = OFF_GA // tn
    gmb = (OFF_GA + d) // tn
    return pl.pallas_call(
        _branch_kernel,
        grid=(d // tn, m // tm),
        in_specs=[pl.BlockSpec((tm, ka), lambda j, i: (i, 0)),
                  pl.BlockSpec((tm, km), lambda j, i: (i, 0)),
                  pl.BlockSpec((1, ka, tn), lambda j, i: (layer, 0, j)),
                  pl.BlockSpec((1, km, tn), lambda j, i: (layer, 0, j)),
                  pl.BlockSpec((tm, tn), lambda j, i: (i, gab + j)),
                  pl.BlockSpec((tm, tn), lambda j, i: (i, gmb + j)),
                  pl.BlockSpec((1, 1, tn), lambda j, i: (2 * layer, 0, j)),
                  pl.BlockSpec((1, 1, tn), lambda j, i: (2 * layer + 1, 0, j))],
        out_specs=pl.BlockSpec((tm, tn), lambda j, i: (i, j)),
        out_shape=jax.ShapeDtypeStruct((m, d), BF16),
        scratch_shapes=[pltpu.VMEM((ka, tn), BF16), pltpu.VMEM((km, tn), BF16)],
        compiler_params=_cparams(("parallel", "arbitrary")),
        name="branch_mix",
    )(att, mem, w_a, w_m, proj, proj, bias, bias)


def _pack_bf16_pair(lo, hi):
    lo_bits = lax.bitcast_convert_type(lo.astype(BF16).astype(F32), jnp.uint32)
    hi_bits = lax.bitcast_convert_type(hi.astype(BF16).astype(F32), jnp.uint32)
    return (lo_bits >> 16) | (hi_bits & jnp.uint32(0xFFFF0000))


def _unpack_bf16_pair(word):
    lo = lax.bitcast_convert_type(word << 16, F32)
    hi = lax.bitcast_convert_type(word & jnp.uint32(0xFFFF0000), F32)
    return lo, hi


def _norm_router_kernel(x_ref, g_ref, wr_ref, br_ref, h_ref, lg_ref):
    x = x_ref[...]
    h = x * lax.rsqrt(jnp.mean(x * x, axis=-1, keepdims=True) + EPS) * g_ref[...]
    half = h.shape[1] // 2
    h_ref[...] = _pack_bf16_pair(h[:, :half], h[:, half:])
    lg_ref[...] = jnp.dot(h, wr_ref[...], preferred_element_type=F32,
                          precision=lax.Precision.HIGHEST) + br_ref[...]


def _norm_router(x, gain, w_r, b_r, tm=256):
    m, d = x.shape
    n = w_r.shape[1]
    return pl.pallas_call(
        _norm_router_kernel,
        grid=(m // tm,),
        in_specs=[pl.BlockSpec((tm, d), lambda i: (i, 0)),
                  pl.BlockSpec((1, d), lambda i: (0, 0)),
                  pl.BlockSpec((d, n), lambda i: (0, 0)),
                  pl.BlockSpec((1, n), lambda i: (0, 0))],
        out_specs=[pl.BlockSpec((tm, d // 2), lambda i: (i, 0)),
                   pl.BlockSpec((tm, n), lambda i: (i, 0))],
        out_shape=[jax.ShapeDtypeStruct((m, d // 2), jnp.uint32),
                   jax.ShapeDtypeStruct((m, n), F32)],
        compiler_params=_cparams(("parallel",)),
        name="norm_router",
    )(x, gain.reshape(1, d), w_r, b_r)


def _moe_kernel(blk_e_ref, cnt_ref, first_ref, tok_ref, h_hbm, wg_ref, wu_ref,
                wd_lo_ref, wd_hi_ref, y_ref, xg_ref, xb_ref, act_ref, sem, *, n_ff_tiles, n_steps,
                n_blk):
    b = pl.program_id(0)
    j = pl.program_id(1)
    rows = MOE_ROWS
    tf = MOE_FF_TILE
    grp = DMA_ISSUE_UNROLL
    cnt = cnt_ref[b]
    used = cnt > 0
    up_phase = j < n_ff_tiles
    n_sub = (cnt + MOE_SUB_ROWS - 1) // MOE_SUB_ROWS

    def row_copy(i, tok):
        return pltpu.make_async_copy(h_hbm.at[pl.ds(tok, 1)], xg_ref.at[pl.ds(i, 1)], sem)

    def issue_rows(blk):
        first = first_ref[blk]
        last_entry = tok_ref.shape[0] - 1

        def body(gi, carry):
            for r in range(grp):
                i = gi * grp + r
                row_copy(i, tok_ref[jnp.minimum(first + i, last_entry)]).start()
            return carry
        lax.fori_loop(0, (cnt_ref[blk] + grp - 1) // grp, body, 0)

    def wait_rows(count):
        def body(gi, carry):
            for r in range(grp):
                row_copy(gi * grp + r, 0).wait()
            return carry
        lax.fori_loop(0, (count + grp - 1) // grp, body, 0)

    def land_rows(blk):
        wait_rows(cnt_ref[blk])
        half = xg_ref.shape[1]
        lo, hi = _unpack_bf16_pair(xg_ref[...])
        xb_ref[blk % 2, :, 0:half] = lo.astype(BF16)
        xb_ref[blk % 2, :, half:2 * half] = hi.astype(BF16)

    @pl.when(jnp.logical_and(b == 0, j == 0))
    def _():
        xg_ref[...] = jnp.zeros_like(xg_ref)
        issue_rows(0)
        land_rows(0)
        if n_blk > 1:
            issue_rows(1)

    @pl.when(jnp.logical_and(used, j == n_steps - 1))
    def _():
        @pl.when(b + 1 < n_blk)
        def _():
            @pl.when(cnt_ref[jnp.minimum(b + 1, n_blk - 1)] > 0)
            def _():
                land_rows(b + 1)

            @pl.when(b + 2 < n_blk)
            def _():
                issue_rows(jnp.minimum(b + 2, n_blk - 1))

    for m in range(MOE_SUB_ROWS, rows + 1, MOE_SUB_ROWS):
        live = jnp.logical_and(used, n_sub == m // MOE_SUB_ROWS)

        @pl.when(jnp.logical_and(live, up_phase))
        def _(m=m):
            x = xb_ref[b % 2, 0:m, :]
            g = jnp.dot(x, wg_ref[0, 0].astype(BF16), preferred_element_type=F32)
            u = jnp.dot(x, wu_ref[0, 0].astype(BF16), preferred_element_type=F32)
            act_ref[j, 0:m, :] = (g * jax.nn.sigmoid(g) * u).astype(BF16)

        @pl.when(jnp.logical_and(live, jnp.logical_not(up_phase)))
        def _(m=m):
            def down(wd_ref):
                acc = jnp.dot(act_ref[0, 0:m, :], wd_ref[0, 0, 0:tf, :].astype(BF16),
                              preferred_element_type=F32)
                for t in range(1, n_ff_tiles):
                    acc = acc + jnp.dot(act_ref[t, 0:m, :],
                                        wd_ref[0, 0, t * tf:(t + 1) * tf, :].astype(BF16),
                                        preferred_element_type=F32)
                return acc

            y_ref[0:m, :] = _pack_bf16_pair(down(wd_lo_ref), down(wd_hi_ref))
            if m < rows:
                y_ref[m:rows, :] = jnp.zeros((rows - m, y_ref.shape[1]), y_ref.dtype)

    @pl.when(jnp.logical_and(jnp.logical_not(used), jnp.logical_not(up_phase)))
    def _():
        y_ref[...] = jnp.zeros_like(y_ref)


def _moe_experts(h2p, blk_expert, blk_cnt, blk_first, tok_sorted, w_gate, w_up, w_down, layer):
    d = 2 * h2p.shape[1]
    d_ff = w_gate.shape[3]
    rows = MOE_ROWS
    n_blk = blk_expert.shape[0]
    tf = MOE_FF_TILE
    tn = MOE_OUT_TILE
    n_f = d_ff // tf
    n_o = d // 2 // tn

    def ff_tile(b, j, cnt):
        return jnp.where(cnt[b] > 0, jnp.minimum(j, n_f - 1), n_f - 1)

    def out_tile(b, j):
        return jnp.clip(j - n_f, 0, n_o - 1)

    def w_down_block(b, j, be, cnt, hi):
        stay = jnp.logical_or(j < n_f, cnt[b] == 0)
        expert = jnp.where(j < n_f, be[jnp.maximum(b - 1, 0)], be[b])
        return layer, expert, 0, hi * n_o + jnp.where(stay, n_o - 1, j - n_f)

    grid_spec = pltpu.PrefetchScalarGridSpec(
        num_scalar_prefetch=4,
        grid=(n_blk, n_f + n_o),
        in_specs=[pl.BlockSpec(memory_space=pl.ANY),
                  pl.BlockSpec((1, 1, d, tf),
                               lambda b, j, be, cnt, *_: (layer, be[b], 0, ff_tile(b, j, cnt))),
                  pl.BlockSpec((1, 1, d, tf),
                               lambda b, j, be, cnt, *_: (layer, be[b], 0, ff_tile(b, j, cnt))),
                  pl.BlockSpec((1, 1, d_ff, tn),
                               lambda b, j, be, cnt, *_: w_down_block(b, j, be, cnt, 0)),
                  pl.BlockSpec((1, 1, d_ff, tn),
                               lambda b, j, be, cnt, *_: w_down_block(b, j, be, cnt, 1))],
        out_specs=pl.BlockSpec((rows, tn), lambda b, j, *_: (b, out_tile(b, j))),
        scratch_shapes=[pltpu.VMEM((rows, d // 2), jnp.uint32),
                        pltpu.VMEM((2, rows, d), BF16),
                        pltpu.VMEM((n_f, rows, tf), BF16),
                        pltpu.SemaphoreType.DMA(())],
    )
    return pl.pallas_call(
        functools.partial(_moe_kernel, n_ff_tiles=n_f, n_steps=n_f + n_o, n_blk=n_blk),
        grid_spec=grid_spec,
        out_shape=jax.ShapeDtypeStruct((n_blk * rows, d // 2), jnp.uint32),
        compiler_params=_cparams(("arbitrary", "arbitrary")),
        name="moe_experts",
    )(blk_expert, blk_cnt, blk_first, tok_sorted, h2p, w_gate, w_up, w_down, w_down)


def _combine_kernel(pos_ref, pos_next_ref, x_ref, w_ref, y_hbm, o_ref, buf_ref, sem, *, n_steps):
    rows = COMBINE_ROWS
    n_copy = TOP_K * rows
    step = pl.program_id(0)
    slot = step % 2

    def row_copy(s, i, src):
        return pltpu.make_async_copy(y_hbm.at[pl.ds(src, 1)], buf_ref.at[s, pl.ds(i, 1)], sem.at[s])

    def issue(s, idx_ref):
        def body(i, carry):
            row_copy(s, i, idx_ref[0, 0, i]).start()
            return carry
        lax.fori_loop(0, n_copy, body, 0, unroll=DMA_ISSUE_UNROLL)

    @pl.when(step == 0)
    def _():
        issue(0, pos_ref)

    @pl.when(step + 1 < n_steps)
    def _():
        issue(1 - slot, pos_next_ref)

    def drain(i, carry):
        row_copy(slot, i, 0).wait()
        return carry

    lax.fori_loop(0, n_copy, drain, 0, unroll=DMA_ISSUE_UNROLL)
    half = x_ref.shape[1] // 2
    lo0, hi0 = _unpack_bf16_pair(buf_ref[slot, pl.ds(0, rows), :])
    lo1, hi1 = _unpack_bf16_pair(buf_ref[slot, pl.ds(rows, rows), :])
    w0 = w_ref[:, 0:1]
    w1 = w_ref[:, 1:2]
    o_ref[:, 0:half] = x_ref[:, 0:half] + (w0 * lo0 + w1 * lo1)
    o_ref[:, half:2 * half] = x_ref[:, half:2 * half] + (w0 * hi0 + w1 * hi1)


def _moe_combine(x1, ys, pos, weights):
    n_tok, d = x1.shape
    rows = COMBINE_ROWS
    n_blk = n_tok // rows
    pos_b = pos.reshape(n_blk, rows, TOP_K).transpose(0, 2, 1).reshape(n_blk, 1, TOP_K * rows)
    idx_block = (1, 1, TOP_K * rows)
    return pl.pallas_call(
        functools.partial(_combine_kernel, n_steps=n_blk),
        grid=(n_blk,),
        in_specs=[pl.BlockSpec(idx_block, lambda i: (i, 0, 0), memory_space=pltpu.SMEM),
                  pl.BlockSpec(idx_block, lambda i: (jnp.minimum(i + 1, n_blk - 1), 0, 0),
                               memory_space=pltpu.SMEM),
                  pl.BlockSpec((rows, d), lambda i: (i, 0)),
                  pl.BlockSpec((rows, TOP_K), lambda i: (i, 0)),
                  pl.BlockSpec(memory_space=pl.ANY)],
        out_specs=pl.BlockSpec((rows, d), lambda i: (i, 0)),
        out_shape=jax.ShapeDtypeStruct((n_tok, d), F32),
        scratch_shapes=[pltpu.VMEM((2, TOP_K * rows, d // 2), jnp.uint32),
                        pltpu.SemaphoreType.DMA((2,))],
        compiler_params=_cparams(("arbitrary",)),
        name="moe_combine",
    )(pos_b, pos_b, x1, weights, ys)


def _route(logits, n_tok):
    glog = logits[:, :N_GROUPS]
    gprob = jax.nn.softmax(glog, axis=-1)
    g_sel = jnp.argmax(glog, axis=-1)
    p_g = jnp.take_along_axis(gprob, g_sel[:, None], axis=1)[:, 0]
    elog = logits[:, N_GROUPS:N_GROUPS + N_EXPERTS].reshape(n_tok, N_GROUPS, EXPERTS_PER_GROUP)
    elog = jnp.take_along_axis(elog, g_sel[:, None, None], axis=1)[:, 0]
    top_v, top_i = lax.top_k(elog, TOP_K)
    weights = p_g[:, None] * jax.nn.softmax(top_v, axis=-1)
    expert_ids = (g_sel[:, None] * EXPERTS_PER_GROUP + top_i).astype(jnp.int32)

    n_assign = n_tok * TOP_K
    rows = MOE_ROWS
    flat_e = expert_ids.reshape(n_assign)
    a_ids = jnp.arange(n_assign, dtype=jnp.int32)
    skey = jnp.sort(flat_e * n_assign + a_ids)
    order = skey % n_assign
    inv = jnp.argsort(order).astype(jnp.int32)
    experts = jnp.arange(N_EXPERTS, dtype=jnp.int32)
    counts = jnp.sum((flat_e[:, None] == experts[None, :]).astype(jnp.int32), axis=0)
    blocks = (counts + rows - 1) // rows
    pad_end = jnp.cumsum(blocks * rows)
    pad_start = pad_end - blocks * rows
    start = jnp.cumsum(counts) - counts
    n_slots = n_assign + N_EXPERTS * rows
    n_blk = n_slots // rows
    n_used = jnp.sum(blocks).astype(jnp.int32)
    blk_ids = jnp.arange(n_blk, dtype=jnp.int32)
    blk_expert = jnp.minimum(
        jnp.sum((pad_end[None, :] <= (blk_ids * rows)[:, None]).astype(jnp.int32), axis=1),
        N_EXPERTS - 1)
    blk_cnt = jnp.clip(counts[blk_expert] - (blk_ids * rows - pad_start[blk_expert]), 0, rows)
    blk_cnt = jnp.where(blk_ids < n_used, blk_cnt, 0).astype(jnp.int32)
    blk_first = (start[blk_expert] + blk_ids * rows - pad_start[blk_expert]).astype(jnp.int32)
    tok_sorted = (order // TOP_K).astype(jnp.int32)
    blk_expert = jnp.where(blk_ids < n_used, blk_expert,
                           blk_expert[jnp.maximum(n_used - 1, 0)]).astype(jnp.int32)
    shift = jnp.sum(jnp.where(flat_e[:, None] == experts[None, :], (pad_start - start)[None, :], 0),
                    axis=1)
    pos = (inv + shift).astype(jnp.int32).reshape(n_tok, TOP_K)
    return blk_expert, blk_cnt, blk_first, tok_sorted, pos, weights


def kernel(x, norm1_gain, w_in, attn_q_norm_gain, attn_k_norm_gain, mlstm_gate_bias,
           mlstm_head_norm_gain, branch_gate_bias, w_attn_branch, w_mlstm_branch, w_out,
           norm2_gain, w_router_group, b_router_group, w_router_expert, b_router_expert,
           w_expert_gate, w_expert_up, w_expert_down):
    bsz, s_len, d = x.shape
    assert bsz == 1
    depth = w_in.shape[0]
    xs = x.reshape(s_len, d)
    w_in_t = jnp.swapaxes(w_in, 1, 2)
    n_proj = OFF_GA + 2 * d
    tn = 512

    def proj_rows(j):
        return j * tn + jnp.where(j * tn >= OFF_GA, N_ML_GATES, 0)

    for l in range(depth):
        h = _rmsnorm(xs, norm1_gain[l])
        proj = _matmul_nt(h, w_in_t, l, proj_rows, n_proj, BF16, 1024, tn, "in_proj")
        gates = _matmul_nt(h, w_in_t, l, lambda j: OFF_MG + j * LANES, LANES, F32, 1024, LANES,
                           "gate_proj")[:, :N_ML_GATES]

        att = _attention(proj, attn_q_norm_gain[l], attn_k_norm_gain[l])

        hs = _mlstm(proj, gates, mlstm_gate_bias[l])
        mem = _mlstm_out(hs, proj, mlstm_head_norm_gain[l])

        merged = _branch_mix(att, mem, w_attn_branch, w_mlstm_branch, l, proj, branch_gate_bias)
        x1 = _matmul_residual(merged, w_out, l, xs, 1024, 512)

        w_r = jnp.concatenate([w_router_group[l], w_router_expert[l]], axis=1)
        b_r = jnp.concatenate([b_router_group[l], b_router_expert[l]])
        n_r = w_r.shape[1]
        w_r = jnp.pad(w_r, ((0, 0), (0, LANES - n_r)))
        b_r = jnp.pad(b_r, (0, LANES - n_r)).reshape(1, LANES)
        h2, logits = _norm_router(x1, norm2_gain[l], w_r, b_r)

        blk_expert, blk_cnt, blk_first, tok_sorted, pos, weights = _route(logits, s_len)
        ys = _moe_experts(h2, blk_expert, blk_cnt, blk_first, tok_sorted,
                          w_expert_gate, w_expert_up, w_expert_down, l)
        xs = _moe_combine(x1, ys, pos, weights)
    return xs.reshape(bsz, s_len, d)
```

```python
import functools

import jax
import jax.numpy as jnp
from jax import lax
from jax.experimental import pallas as pl
from jax.experimental.pallas import tpu as pltpu

F32 = jnp.float32
BF16 = jnp.bfloat16

EPS = 1e-6
NEG = -1e30
LANES = 128

ATT_PATTERNS = ((128, 1), (512, 4), (2048, 16))
N_ATT_GROUPS = 3
ATT_HEADS = 8
ATT_HEAD_DIM = 128
ATT_GROUP_W = ATT_HEADS * ATT_HEAD_DIM
ATT_W = N_ATT_GROUPS * ATT_GROUP_W
ROPE_THETA = 10000.0
ATT_Q_TILE = 128
ATT_TILE_UNROLL = 8

ML_HEADS = 8
ML_QK_DIM = 256
ML_V_DIM = 512
ML_QK_W = ML_HEADS * ML_QK_DIM
ML_V_W = ML_HEADS * ML_V_DIM
N_ML_GATES = 4 * ML_HEADS
GATE_SOFTCAP = 15.0
ML_TILE = 256
ML_HEAD_GROUP = 2
ML_OUT_HEADS = 2

N_GROUPS = 8
EXPERTS_PER_GROUP = 8
N_EXPERTS = N_GROUPS * EXPERTS_PER_GROUP
TOP_K = 2
MOE_ROWS = 512
MOE_SUB_ROWS = 128
MOE_FF_TILE = 256
MOE_OUT_TILE = 1024
COMBINE_ROWS = 256
COMBINE_CHUNK = 8
DMA_ISSUE_UNROLL = 8

OFF_AQ = 0
OFF_AK = OFF_AQ + ATT_W
OFF_AV = OFF_AK + ATT_W
OFF_MQ = OFF_AV + ATT_W
OFF_MK = OFF_MQ + ML_QK_W
OFF_MV = OFF_MK + ML_QK_W
OFF_MO = OFF_MV + ML_V_W
OFF_MG = OFF_MO + ML_V_W
OFF_GA = OFF_MG

VMEM_LIMIT = 56 * 1024 * 1024


def _cparams(sem, vmem=VMEM_LIMIT):
    return pltpu.CompilerParams(dimension_semantics=sem, vmem_limit_bytes=vmem)


def _rmsnorm_kernel(x_ref, g_ref, o_ref):
    x = x_ref[...]
    ms = jnp.mean(x * x, axis=-1, keepdims=True)
    o_ref[...] = (x * lax.rsqrt(ms + EPS) * g_ref[...]).astype(o_ref.dtype)


def _rmsnorm(x, gain, tm=256):
    m, d = x.shape
    return pl.pallas_call(
        _rmsnorm_kernel,
        grid=(m // tm,),
        in_specs=[pl.BlockSpec((tm, d), lambda i: (i, 0)),
                  pl.BlockSpec((1, d), lambda i: (0, 0))],
        out_specs=pl.BlockSpec((tm, d), lambda i: (i, 0)),
        out_shape=jax.ShapeDtypeStruct((m, d), BF16),
        compiler_params=_cparams(("parallel",)),
        name="rmsnorm",
    )(x, gain.reshape(1, d))


def _mm_kernel(a_ref, w_ref, o_ref, wb_ref):
    @pl.when(pl.program_id(1) == 0)
    def _():
        wb_ref[...] = w_ref[0].astype(BF16)

    o_ref[...] = jnp.dot(a_ref[...], wb_ref[...], preferred_element_type=F32).astype(o_ref.dtype)


def _matmul(a, w, layer, col0, n, out_dtype, tm, tn, name):
    m, k = a.shape
    cb = col0 // tn
    return pl.pallas_call(
        _mm_kernel,
        grid=(n // tn, m // tm),
        in_specs=[pl.BlockSpec((tm, k), lambda j, i: (i, 0)),
                  pl.BlockSpec((1, k, tn), lambda j, i: (layer, 0, cb + j))],
        out_specs=pl.BlockSpec((tm, tn), lambda j, i: (i, j)),
        out_shape=jax.ShapeDtypeStruct((m, n), out_dtype),
        scratch_shapes=[pltpu.VMEM((k, tn), BF16)],
        compiler_params=_cparams(("parallel", "arbitrary")),
        name=name,
    )(a, w)


def _mm_nt_kernel(a_ref, wt_ref, o_ref, wb_ref):
    @pl.when(pl.program_id(1) == 0)
    def _():
        wb_ref[...] = wt_ref[0].astype(BF16)

    o_ref[...] = lax.dot_general(a_ref[...], wb_ref[...], (((1,), (1,)), ((), ())),
                                 preferred_element_type=F32).astype(o_ref.dtype)


def _matmul_nt(a, w_t, layer, row_of_tile, n, out_dtype, tm, tn, name):
    m, k = a.shape
    return pl.pallas_call(
        _mm_nt_kernel,
        grid=(n // tn, m // tm),
        in_specs=[pl.BlockSpec((tm, k), lambda j, i: (i, 0)),
                  pl.BlockSpec((pl.Element(1), pl.Element(tn), pl.Element(k)),
                               lambda j, i: (layer, pl.multiple_of(row_of_tile(j), 8), 0))],
        out_specs=pl.BlockSpec((tm, tn), lambda j, i: (i, j)),
        out_shape=jax.ShapeDtypeStruct((m, n), out_dtype),
        scratch_shapes=[pltpu.VMEM((tn, k), BF16)],
        compiler_params=_cparams(("parallel", "arbitrary")),
        name=name,
    )(a, w_t)


def _mm_nt_pair_kernel(a_ref, wt_ref, o_ref, wb_ref):
    c = pl.program_id(2)

    @pl.when(pl.program_id(1) == 0)
    def _():
        wb_ref[c] = wt_ref[0].astype(BF16)

    o_ref[...] = lax.dot_general(a_ref[...], wb_ref[c], (((1,), (1,)), ((), ())),
                                 preferred_element_type=F32).astype(o_ref.dtype)


def _matmul_nt_paired(a, w_t, layer, row_of_tile, n, out_dtype, tm, tn, name):
    m, k = a.shape
    assert n % (2 * tn) == 0

    def w_rows(jj, i, c):
        return pl.multiple_of(row_of_tile(2 * jj + jnp.where(i == 0, c, 1)), 8)

    return pl.pallas_call(
        _mm_nt_pair_kernel,
        grid=(n // (2 * tn), m // tm, 2),
        in_specs=[pl.BlockSpec((tm, k), lambda jj, i, c: (i, 0)),
                  pl.BlockSpec((pl.Element(1), pl.Element(tn), pl.Element(k)),
                               lambda jj, i, c: (layer, w_rows(jj, i, c), 0))],
        out_specs=pl.BlockSpec((tm, tn), lambda jj, i, c: (i, 2 * jj + c)),
        out_shape=jax.ShapeDtypeStruct((m, n), out_dtype),
        scratch_shapes=[pltpu.VMEM((2, tn, k), BF16)],
        compiler_params=_cparams(("arbitrary", "arbitrary", "arbitrary")),
        name=name,
    )(a, w_t)


def _mm_res_kernel(a_ref, w_ref, r_ref, o_ref, wb_ref):
    @pl.when(pl.program_id(1) == 0)
    def _():
        wb_ref[...] = w_ref[0].astype(BF16)

    o_ref[...] = r_ref[...] + jnp.dot(a_ref[...], wb_ref[...], preferred_element_type=F32)


def _matmul_residual(a, w, layer, res, tm, tn):
    m, k = a.shape
    n = w.shape[2]
    return pl.pallas_call(
        _mm_res_kernel,
        grid=(n // tn, m // tm),
        in_specs=[pl.BlockSpec((tm, k), lambda j, i: (i, 0)),
                  pl.BlockSpec((1, k, tn), lambda j, i: (layer, 0, j)),
                  pl.BlockSpec((tm, tn), lambda j, i: (i, j))],
        out_specs=pl.BlockSpec((tm, tn), lambda j, i: (i, j)),
        out_shape=jax.ShapeDtypeStruct((m, n), F32),
        scratch_shapes=[pltpu.VMEM((k, tn), BF16)],
        compiler_params=_cparams(("parallel", "arbitrary")),
        name="out_proj",
    )(a, w, res)


def _attn_kernel(q_ref, k_ref, v_ref, tab_ref, qg_ref, kg_ref, o_ref,
                 qn_ref, kn_ref, vf_ref, acc_ref, lse_ref, band_ref, *, s_len):
    grp = pl.program_id(1)
    tq = ATT_Q_TILE
    hd = ATT_HEAD_DIM
    half = hd // 2
    low = lax.broadcasted_iota(jnp.int32, (tq, hd), 1) < half
    row = lax.broadcasted_iota(jnp.int32, (hd, hd), 0)
    col = lax.broadcasted_iota(jnp.int32, (hd, hd), 1)
    ones = jnp.ones((hd, hd), BF16)
    rot = jnp.where(row == col + half, -1.0, jnp.where(col == row + half, 1.0, 0.0)).astype(BF16)

    def prep(i, carry):
        rows = pl.ds(pl.multiple_of(i * tq, tq), tq)
        tab = tab_ref[rows, :]
        swapped = pltpu.roll(tab, half, axis=1)
        c = jnp.where(low, tab, swapped)
        s = jnp.where(low, swapped, tab)
        for src, g_ref, dst in ((q_ref, qg_ref, qn_ref), (k_ref, kg_ref, kn_ref)):
            xf = src[rows, :].astype(F32)
            ssq = jnp.dot((xf * xf).astype(BF16), ones, preferred_element_type=F32)
            y = xf * lax.rsqrt(ssq * (1.0 / hd) + EPS) * g_ref[0]
            y_rot = jnp.dot(y.astype(BF16), rot, preferred_element_type=F32)
            dst[rows, :] = y * c + y_rot * s
        vf_ref[rows, :] = v_ref[rows, :].astype(F32)
        return carry

    lax.fori_loop(0, s_len // tq, prep, 0, unroll=4)

    def run_group(window, dilation, first):
        radius = window // (2 * dilation)
        sub_len = s_len // dilation
        win = tq + 2 * radius
        tiles_per_class = sub_len // tq
        rel = (lax.broadcasted_iota(jnp.int32, (tq, win), 1)
               - lax.broadcasted_iota(jnp.int32, (tq, win), 0))
        for case in range(3):
            band_ref[case] = jnp.where(jnp.abs(rel - case * radius) <= radius, 0.0, NEG)

        def rows_of(start, size):
            if dilation == 1:
                return pl.ds(start, size)
            return pl.ds(start, size, stride=dilation)

        def tile(idx, carry):
            r = idx // tiles_per_class
            q0 = (idx % tiles_per_class) * tq
            k0 = jnp.clip(q0 - radius, 0, sub_len - win)
            q_rows = rows_of(r + dilation * q0, tq)
            k_rows = rows_of(r + dilation * k0, win)
            q = qn_ref[q_rows, :].astype(BF16)
            k = kn_ref[k_rows, :].astype(BF16)
            v = vf_ref[k_rows, :].astype(BF16)
            s = lax.dot_general(q, k, (((1,), (1,)), ((), ())), preferred_element_type=F32)
            s = s + band_ref[(q0 - k0) // radius]
            m = jnp.max(s, axis=-1, keepdims=True)
            p = jnp.exp(s - m)
            den = jnp.sum(p, axis=-1, keepdims=True)
            o = jnp.dot(p.astype(BF16), v, preferred_element_type=F32) / den
            lse = jnp.broadcast_to(m + jnp.log(den), (tq, ATT_HEAD_DIM))
            if first:
                acc_ref[q_rows, :] = o
                lse_ref[q_rows, :] = lse
            else:
                old_l = lse_ref[q_rows, :]
                top = jnp.maximum(old_l, lse)
                a = jnp.exp(old_l - top)
                b = jnp.exp(lse - top)
                tot = a + b
                acc_ref[q_rows, :] = (a * acc_ref[q_rows, :] + b * o) / tot
                lse_ref[q_rows, :] = top + jnp.log(tot)
            return carry

        lax.fori_loop(0, s_len // tq, tile, 0, unroll=ATT_TILE_UNROLL)

    for gi, (window, dilation) in enumerate(ATT_PATTERNS):
        @pl.when(grp == gi)
        def _(window=window, dilation=dilation, gi=gi):
            run_group(window, dilation, gi == 0)

    @pl.when(grp == N_ATT_GROUPS - 1)
    def _():
        o_ref[...] = acc_ref[...].astype(o_ref.dtype)


def _attention(proj, q_gain, k_gain):
    s_len = proj.shape[0]
    half = ATT_HEAD_DIM // 2
    inv_freq = ROPE_THETA ** (-jnp.arange(half, dtype=F32) / half)
    ang = jnp.arange(s_len).astype(F32)[:, None] * inv_freq[None, :]
    table = jnp.concatenate([jnp.cos(ang), jnp.sin(ang)], axis=1)
    qb, kb, vb = OFF_AQ // LANES, OFF_AK // LANES, OFF_AV // LANES
    blk = (s_len, ATT_HEAD_DIM)
    radii = {window // (2 * dilation) for window, dilation in ATT_PATTERNS}
    assert len(radii) == 1
    win = ATT_Q_TILE + 2 * radii.pop()
    gain_spec = pl.BlockSpec((1, 1, ATT_HEAD_DIM), lambda h, g: (g, 0, 0))
    q_scaled = q_gain.reshape(N_ATT_GROUPS, 1, ATT_HEAD_DIM) * (ATT_HEAD_DIM ** -0.5)
    return pl.pallas_call(
        functools.partial(_attn_kernel, s_len=s_len),
        grid=(ATT_HEADS, N_ATT_GROUPS),
        in_specs=[pl.BlockSpec(blk, lambda h, g: (0, qb + g * ATT_HEADS + h)),
                  pl.BlockSpec(blk, lambda h, g: (0, kb + g * ATT_HEADS + h)),
                  pl.BlockSpec(blk, lambda h, g: (0, vb + g * ATT_HEADS + h)),
                  pl.BlockSpec(blk, lambda h, g: (0, 0)),
                  gain_spec, gain_spec],
        out_specs=pl.BlockSpec(blk, lambda h, g: (0, h)),
        out_shape=jax.ShapeDtypeStruct((s_len, ATT_GROUP_W), BF16),
        scratch_shapes=[pltpu.VMEM(blk, F32)] * 5 + [pltpu.VMEM((3, ATT_Q_TILE, win), F32)],
        compiler_params=_cparams(("parallel", "arbitrary")),
        name="dilated_attn",
    )(proj, proj, proj, table, q_scaled, k_gain.reshape(N_ATT_GROUPS, 1, ATT_HEAD_DIM))


def _tile_scan(x, pos, lc, combine, reverse, fill):
    n = x.shape[1]
    k = 1
    while k < lc:
        if reverse:
            shifted = pltpu.roll(x, n - k, axis=1)
            ok = pos < lc - k
        else:
            shifted = pltpu.roll(x, k, axis=1)
            ok = pos >= k
        x = combine(x, jnp.where(ok, shifted, fill))
        k *= 2
    return x


def _mlstm_gate_kernel(ig_ref, f_ref, bias_ref, o_ref, *, lc):
    n_rows, s_len = ig_ref.shape
    n_tiles = s_len // lc
    shape = (n_rows, s_len)
    bw = lax.broadcasted_iota(jnp.int32, shape, 0) >= ML_HEADS
    lane = lax.broadcasted_iota(jnp.int32, shape, 1)
    pos = lane % lc
    tile = lane // lc

    ig = GATE_SOFTCAP * jnp.tanh((ig_ref[...] + bias_ref[:, 0:1]) / GATE_SOFTCAP)
    f = GATE_SOFTCAP * jnp.tanh((f_ref[...] + bias_ref[:, 1:2]) / GATE_SOFTCAP)
    lf = jnp.minimum(f, 0.0) - jnp.log1p(jnp.exp(-jnp.abs(f)))

    pre = _tile_scan(lf, pos, lc, jnp.add, False, 0.0)
    suf = _tile_scan(lf, pos, lc, jnp.add, True, 0.0)
    b = jnp.where(bw, suf, pre)
    tot = pre + suf - lf
    a = ig - b
    pmax = _tile_scan(a, pos, lc, jnp.maximum, False, -jnp.inf)
    smax = _tile_scan(a, pos, lc, jnp.maximum, True, -jnp.inf)
    cm = jnp.where(bw, smax, pmax)
    cmax = jnp.maximum(pmax, smax)

    def tiles_scan(reverse):
        alpha, beta = tot, cmax + tot
        step = 1
        while step < n_tiles:
            sh = step * lc
            if reverse:
                pa, pb = pltpu.roll(alpha, s_len - sh, axis=1), pltpu.roll(beta, s_len - sh, axis=1)
                ok = tile < n_tiles - step
            else:
                pa, pb = pltpu.roll(alpha, sh, axis=1), pltpu.roll(beta, sh, axis=1)
                ok = tile >= step
            alpha, beta = (jnp.where(ok, pa + alpha, alpha),
                           jnp.where(ok, jnp.maximum(pb + alpha, beta), beta))
            step *= 2
        return jnp.maximum(alpha, beta)

    last_f = tiles_scan(False)
    last_r = tiles_scan(True)
    m_last = jnp.where(bw, last_r, last_f)
    prev_f = jnp.where(tile >= 1, pltpu.roll(last_f, lc, axis=1), 0.0)
    prev_r = jnp.where(tile < n_tiles - 1, pltpu.roll(last_r, s_len - lc, axis=1), 0.0)
    m_prev = jnp.where(bw, prev_r, prev_f)
    mm = jnp.maximum(m_prev, cm)

    o_ref[0] = a
    o_ref[1] = mm
    o_ref[2] = b + mm
    o_ref[3] = m_prev
    o_ref[4] = tot - m_last
    o_ref[5] = tot + m_prev - m_last
    o_ref[6] = jnp.zeros(shape, F32)
    o_ref[7] = jnp.zeros(shape, F32)


def _mlstm_scan_kernel(q_ref, k_ref, v_ref, g_ref, mask_ref, o_ref, c_ref):
    dk, dv = ML_QK_DIM, ML_V_DIM

    @pl.when(pl.program_id(1) == 0)
    def _():
        c_ref[...] = jnp.zeros_like(c_ref)

    lc = q_ref.shape[0]
    mask = mask_ref[0]
    ones = jnp.ones((lc, LANES), BF16)
    for hd in range(ML_HEAD_GROUP):
        rows = g_ref[hd]

        def column(r, rows=rows):
            return jnp.broadcast_to(rows[r:r + 1, :], (8, lc)).T[:, 0:1]

        a_row = rows[0:1, :]
        a_col, mm_col, m_col = column(0), column(1), column(2)
        m_prev = rows[3:4, 0:1]
        d_mat = jnp.exp(a_row - mm_col + mask)
        decay_q = jnp.exp(m_prev - mm_col)

        q = q_ref[:, hd * dk:(hd + 1) * dk]
        k = k_ref[:, hd * dk:(hd + 1) * dk] * (dk ** -0.5)
        v_ext = jnp.concatenate([v_ref[:, hd * dv:(hd + 1) * dv], ones], axis=1)
        s = lax.dot_general(q, k, (((1,), (1,)), ((), ())), preferred_element_type=F32) * d_mat
        c_old = c_ref[hd]
        acc = jnp.dot(s.astype(BF16), v_ext, preferred_element_type=F32)
        acc = acc + decay_q * jnp.dot(q, c_old.astype(BF16), preferred_element_type=F32)
        den = acc[:, dv:dv + LANES]
        inv = 1.0 / jnp.maximum(jnp.abs(den), jnp.exp(-m_col))
        for part in range(dv // LANES):
            cols = slice(part * LANES, (part + 1) * LANES)
            o_ref[0, :, hd * dv + part * LANES:hd * dv + (part + 1) * LANES] = (
                acc[:, cols] * inv).astype(o_ref.dtype)

        w_col = jnp.exp(a_col + rows[4:5, 0:1])
        decay_c = jnp.exp(rows[5:6, 0:1])
        kw = k.astype(F32) * w_col
        c_ref[hd] = decay_c * c_old + lax.dot_general(
            kw.astype(BF16), v_ext, (((0,), (0,)), ((), ())), preferred_element_type=F32)


def _mlstm(proj, gates, gate_bias):
    s_len = proj.shape[0]
    lc = ML_TILE
    nt = s_len // lc
    n_rows = 2 * ML_HEADS
    gt = gates.T.reshape(2, 2, ML_HEADS, s_len)
    bias = gate_bias.reshape(2, 2, ML_HEADS)
    gate_rows = pl.pallas_call(
        functools.partial(_mlstm_gate_kernel, lc=lc),
        out_shape=jax.ShapeDtypeStruct((8, n_rows, s_len), F32),
        compiler_params=_cparams(None),
        name="mlstm_gates",
    )(gt[:, 0].reshape(n_rows, s_len), gt[:, 1].reshape(n_rows, s_len),
      jnp.stack([bias[:, 0].reshape(n_rows), bias[:, 1].reshape(n_rows)], axis=1))
    gate_rows = gate_rows.transpose(1, 0, 2)

    t_idx = jnp.arange(lc)[:, None]
    s_idx = jnp.arange(lc)[None, :]
    mask = jnp.stack([jnp.where(s_idx <= t_idx, 0.0, NEG), jnp.where(s_idx >= t_idx, 0.0, NEG)]).astype(F32)

    hg = ML_HEAD_GROUP
    groups = ML_HEADS // hg

    def tile_of(dg, t):
        return jnp.where(dg >= groups, nt - 1 - t, t)

    assert OFF_MQ % (hg * ML_QK_DIM) == 0 and OFF_MK % (hg * ML_QK_DIM) == 0
    assert OFF_MV % (hg * ML_V_DIM) == 0
    qb, kb = OFF_MQ // (hg * ML_QK_DIM), OFF_MK // (hg * ML_QK_DIM)
    vb = OFF_MV // (hg * ML_V_DIM)
    return pl.pallas_call(
        _mlstm_scan_kernel,
        grid=(2 * groups, nt),
        in_specs=[pl.BlockSpec((lc, hg * ML_QK_DIM), lambda dg, t: (tile_of(dg, t), qb + dg % groups)),
                  pl.BlockSpec((lc, hg * ML_QK_DIM), lambda dg, t: (tile_of(dg, t), kb + dg % groups)),
                  pl.BlockSpec((lc, hg * ML_V_DIM), lambda dg, t: (tile_of(dg, t), vb + dg % groups)),
                  pl.BlockSpec((hg, 8, lc), lambda dg, t: (dg, 0, tile_of(dg, t))),
                  pl.BlockSpec((1, lc, lc), lambda dg, t: (dg // groups, 0, 0))],
        out_specs=pl.BlockSpec((1, lc, hg * ML_V_DIM),
                               lambda dg, t: (dg // groups, tile_of(dg, t), dg % groups)),
        out_shape=jax.ShapeDtypeStruct((2, s_len, ML_V_W), BF16),
        scratch_shapes=[pltpu.VMEM((hg, ML_QK_DIM, ML_V_DIM + LANES), F32)],
        compiler_params=_cparams(("parallel", "arbitrary")),
        name="mlstm_scan",
    )(proj, proj, proj, gate_rows, mask)


def _mlstm_out_kernel(h_ref, mo_ref, g_ref, o_ref):
    for hd in range(ML_OUT_HEADS):
        cols = slice(hd * ML_V_DIM, (hd + 1) * ML_V_DIM)
        h = h_ref[0, :, cols].astype(F32) + h_ref[1, :, cols].astype(F32)
        y = h * lax.rsqrt(jnp.mean(h * h, axis=-1, keepdims=True) + EPS) * g_ref[0, :, cols]
        o_ref[:, cols] = (y * jax.nn.sigmoid(mo_ref[:, cols].astype(F32))).astype(o_ref.dtype)


def _mlstm_out(hs, proj, head_gain, tm=1024):
    s_len = proj.shape[0]
    width = ML_OUT_HEADS * ML_V_DIM
    assert OFF_MO % width == 0
    ob = OFF_MO // width
    return pl.pallas_call(
        _mlstm_out_kernel,
        grid=(s_len // tm, ML_V_W // width),
        in_specs=[pl.BlockSpec((2, tm, width), lambda i, h: (0, i, h)),
                  pl.BlockSpec((tm, width), lambda i, h: (i, ob + h)),
                  pl.BlockSpec((1, 1, width), lambda i, h: (h, 0, 0))],
        out_specs=pl.BlockSpec((tm, width), lambda i, h: (i, h)),
        out_shape=jax.ShapeDtypeStruct((s_len, ML_V_W), BF16),
        compiler_params=_cparams(("parallel", "parallel")),
        name="mlstm_out",
    )(hs, proj, head_gain.reshape(ML_V_W // width, 1, width))


def _branch_kernel(att_ref, mem_ref, wa_ref, wm_ref, ga_ref, gm_ref, ba_ref, bm_ref, o_ref,
                   wab_ref, wmb_ref):
    @pl.when(pl.program_id(1) == 0)
    def _():
        wab_ref[...] = wa_ref[0].astype(BF16)
        wmb_ref[...] = wm_ref[0].astype(BF16)

    ya = jnp.dot(att_ref[...], wab_ref[...], preferred_element_type=F32)
    ym = jnp.dot(mem_ref[...], wmb_ref[...], preferred_element_type=F32)
    g_a = jax.nn.sigmoid(ga_ref[...].astype(F32) + ba_ref[0])
    g_m = jax.nn.sigmoid(gm_ref[...].astype(F32) + bm_ref[0])
    o_ref[...] = (g_a * ya + g_m * ym).astype(o_ref.dtype)


def _branch_mix(att, mem, w_a, w_m, layer, proj, gate_bias, tm=512, tn=512):
    m = att.shape[0]
    ka, km, d = w_a.shape[1], w_m.shape[1], w_a.shape[2]
    bias = gate_bias.reshape(-1, 1, d)
    gab = OFF_GA // tn
    gmb = (OFF_GA + d) // tn
    return pl.pallas_call(
        _branch_kernel,
        grid=(d // tn, m // tm),
        in_specs=[pl.BlockSpec((tm, ka), lambda j, i: (i, 0)),
                  pl.BlockSpec((tm, km), lambda j, i: (i, 0)),
                  pl.BlockSpec((1, ka, tn), lambda j, i: (layer, 0, j)),
                  pl.BlockSpec((1, km, tn), lambda j, i: (layer, 0, j)),
                  pl.BlockSpec((tm, tn), lambda j, i: (i, gab + j)),
                  pl.BlockSpec((tm, tn), lambda j, i: (i, gmb + j)),
                  pl.BlockSpec((1, 1, tn), lambda j, i: (2 * layer, 0, j)),
                  pl.BlockSpec((1, 1, tn), lambda j, i: (2 * layer + 1, 0, j))],
        out_specs=pl.BlockSpec((tm, tn), lambda j, i: (i, j)),
        out_shape=jax.ShapeDtypeStruct((m, d), BF16),
        scratch_shapes=[pltpu.VMEM((ka, tn), BF16), pltpu.VMEM((km, tn), BF16)],
        compiler_params=_cparams(("parallel", "arbitrary")),
        name="branch_mix",
    )(att, mem, w_a, w_m, proj, proj, bias, bias)


def _pack_bf16_pair(lo, hi):
    lo_bits = lax.bitcast_convert_type(lo.astype(BF16).astype(F32), jnp.uint32)
    hi_bits = lax.bitcast_convert_type(hi.astype(BF16).astype(F32), jnp.uint32)
    return (lo_bits >> 16) | (hi_bits & jnp.uint32(0xFFFF0000))


def _unpack_bf16_pair(word):
    lo = lax.bitcast_convert_type(word << 16, F32)
    hi = lax.bitcast_convert_type(word & jnp.uint32(0xFFFF0000), F32)
    return lo, hi


def _norm_router_kernel(x_ref, g_ref, whi_ref, wlo_ref, br_ref, h_ref, lg_ref):
    x = x_ref[...]
    h = x * lax.rsqrt(jnp.mean(x * x, axis=-1, keepdims=True) + EPS) * g_ref[...]
    half = h.shape[1] // 2
    h_ref[...] = _pack_bf16_pair(h[:, :half], h[:, half:])
    h_hi = h.astype(BF16)
    h_lo = (h - h_hi.astype(F32)).astype(BF16)
    w_hi = whi_ref[...]
    logits = jnp.dot(h_hi, w_hi, preferred_element_type=F32)
    logits = logits + jnp.dot(h_lo, w_hi, preferred_element_type=F32)
    logits = logits + jnp.dot(h_hi, wlo_ref[...], preferred_element_type=F32)
    lg_ref[...] = logits + br_ref[...]


def _norm_router(x, gain, w_r, b_r, tm=256):
    m, d = x.shape
    n = w_r.shape[1]
    w_hi = w_r.astype(BF16)
    w_lo = (w_r - w_hi.astype(F32)).astype(BF16)
    return pl.pallas_call(
        _norm_router_kernel,
        grid=(m // tm,),
        in_specs=[pl.BlockSpec((tm, d), lambda i: (i, 0)),
                  pl.BlockSpec((1, d), lambda i: (0, 0)),
                  pl.BlockSpec((d, n), lambda i: (0, 0)),
                  pl.BlockSpec((d, n), lambda i: (0, 0)),
                  pl.BlockSpec((1, n), lambda i: (0, 0))],
        out_specs=[pl.BlockSpec((tm, d // 2), lambda i: (i, 0)),
                   pl.BlockSpec((tm, n), lambda i: (i, 0))],
        out_shape=[jax.ShapeDtypeStruct((m, d // 2), jnp.uint32),
                   jax.ShapeDtypeStruct((m, n), F32)],
        compiler_params=_cparams(("parallel",)),
        name="norm_router",
    )(x, gain.reshape(1, d), w_hi, w_lo, b_r)


def _moe_kernel(blk_e_ref, cnt_ref, first_ref, tok_ref, h_hbm, wg_ref, wu_ref,
                wd_lo_ref, wd_hi_ref, y_ref, xg_ref, xb_ref, act_ref, sem, *, n_ff_tiles, n_steps,
                n_blk):
    b = pl.program_id(0)
    j = pl.program_id(1)
    rows = MOE_ROWS
    tf = MOE_FF_TILE
    grp = DMA_ISSUE_UNROLL
    cnt = cnt_ref[b]
    used = cnt > 0
    up_phase = j < n_ff_tiles
    n_sub = (cnt + MOE_SUB_ROWS - 1) // MOE_SUB_ROWS

    def row_copy(i, tok):
        return pltpu.make_async_copy(h_hbm.at[pl.ds(tok, 1)], xg_ref.at[pl.ds(i, 1)], sem)

    def issue_rows(blk):
        first = first_ref[blk]
        last_entry = tok_ref.shape[0] - 1

        def body(gi, carry):
            for r in range(grp):
                i = gi * grp + r
                row_copy(i, tok_ref[jnp.minimum(first + i, last_entry)]).start()
            return carry
        lax.fori_loop(0, (cnt_ref[blk] + grp - 1) // grp, body, 0)

    def wait_rows(count):
        def body(gi, carry):
            for r in range(grp):
                row_copy(gi * grp + r, 0).wait()
            return carry
        lax.fori_loop(0, (count + grp - 1) // grp, body, 0)

    def land_rows(blk):
        wait_rows(cnt_ref[blk])
        half = xg_ref.shape[1]
        lo, hi = _unpack_bf16_pair(xg_ref[...])
        xb_ref[blk % 2, :, 0:half] = lo.astype(BF16)
        xb_ref[blk % 2, :, half:2 * half] = hi.astype(BF16)

    @pl.when(jnp.logical_and(b == 0, j == 0))
    def _():
        xg_ref[...] = jnp.zeros_like(xg_ref)
        issue_rows(0)
        land_rows(0)
        if n_blk > 1:
            issue_rows(1)

    @pl.when(jnp.logical_and(used, j == n_steps - 1))
    def _():
        @pl.when(b + 1 < n_blk)
        def _():
            @pl.when(cnt_ref[jnp.minimum(b + 1, n_blk - 1)] > 0)
            def _():
                land_rows(b + 1)

            @pl.when(b + 2 < n_blk)
            def _():
                issue_rows(jnp.minimum(b + 2, n_blk - 1))

    for m in range(MOE_SUB_ROWS, rows + 1, MOE_SUB_ROWS):
        live = jnp.logical_and(used, n_sub == m // MOE_SUB_ROWS)

        @pl.when(jnp.logical_and(live, up_phase))
        def _(m=m):
            x = xb_ref[b % 2, 0:m, :]
            g = jnp.dot(x, wg_ref[0, 0].astype(BF16), preferred_element_type=F32)
            u = jnp.dot(x, wu_ref[0, 0].astype(BF16), preferred_element_type=F32)
            act_ref[j, 0:m, :] = (g * jax.nn.sigmoid(g) * u).astype(BF16)

        @pl.when(jnp.logical_and(live, jnp.logical_not(up_phase)))
        def _(m=m):
            def down(wd_ref):
                acc = jnp.dot(act_ref[0, 0:m, :], wd_ref[0, 0, 0:tf, :].astype(BF16),
                              preferred_element_type=F32)
                for t in range(1, n_ff_tiles):
                    acc = acc + jnp.dot(act_ref[t, 0:m, :],
                                        wd_ref[0, 0, t * tf:(t + 1) * tf, :].astype(BF16),
                                        preferred_element_type=F32)
                return acc

            y_ref[0:m, :] = _pack_bf16_pair(down(wd_lo_ref), down(wd_hi_ref))
            if m < rows:
                y_ref[m:rows, :] = jnp.zeros((rows - m, y_ref.shape[1]), y_ref.dtype)

    @pl.when(jnp.logical_and(jnp.logical_not(used), jnp.logical_not(up_phase)))
    def _():
        y_ref[...] = jnp.zeros_like(y_ref)


def _moe_experts(h2p, blk_expert, blk_cnt, blk_first, tok_sorted, w_gate, w_up, w_down, layer):
    d = 2 * h2p.shape[1]
    d_ff = w_gate.shape[3]
    rows = MOE_ROWS
    n_blk = blk_expert.shape[0]
    tf = MOE_FF_TILE
    tn = MOE_OUT_TILE
    n_f = d_ff // tf
    n_o = d // 2 // tn

    def ff_tile(b, j, cnt):
        return jnp.where(cnt[b] > 0, jnp.minimum(j, n_f - 1), n_f - 1)

    def out_tile(b, j):
        return jnp.clip(j - n_f, 0, n_o - 1)

    def w_down_block(b, j, be, cnt, hi):
        stay = jnp.logical_or(j < n_f, cnt[b] == 0)
        expert = jnp.where(j < n_f, be[jnp.maximum(b - 1, 0)], be[b])
        return layer, expert, 0, hi * n_o + jnp.where(stay, n_o - 1, j - n_f)

    grid_spec = pltpu.PrefetchScalarGridSpec(
        num_scalar_prefetch=4,
        grid=(n_blk, n_f + n_o),
        in_specs=[pl.BlockSpec(memory_space=pl.ANY),
                  pl.BlockSpec((1, 1, d, tf),
                               lambda b, j, be, cnt, *_: (layer, be[b], 0, ff_tile(b, j, cnt))),
                  pl.BlockSpec((1, 1, d, tf),
                               lambda b, j, be, cnt, *_: (layer, be[b], 0, ff_tile(b, j, cnt))),
                  pl.BlockSpec((1, 1, d_ff, tn),
                               lambda b, j, be, cnt, *_: w_down_block(b, j, be, cnt, 0)),
                  pl.BlockSpec((1, 1, d_ff, tn),
                               lambda b, j, be, cnt, *_: w_down_block(b, j, be, cnt, 1))],
        out_specs=pl.BlockSpec((rows, tn), lambda b, j, *_: (b, out_tile(b, j))),
        scratch_shapes=[pltpu.VMEM((rows, d // 2), jnp.uint32),
                        pltpu.VMEM((2, rows, d), BF16),
                        pltpu.VMEM((n_f, rows, tf), BF16),
                        pltpu.SemaphoreType.DMA(())],
    )
    return pl.pallas_call(
        functools.partial(_moe_kernel, n_ff_tiles=n_f, n_steps=n_f + n_o, n_blk=n_blk),
        grid_spec=grid_spec,
        out_shape=jax.ShapeDtypeStruct((n_blk * rows, d // 2), jnp.uint32),
        compiler_params=_cparams(("arbitrary", "arbitrary")),
        name="moe_experts",
    )(blk_expert, blk_cnt, blk_first, tok_sorted, h2p, w_gate, w_up, w_down, w_down)


def _combine_kernel(pos_ref, pos_next_ref, x_ref, w_ref, y_hbm, o_ref, buf_ref, sem, *, n_steps):
    rows = COMBINE_ROWS
    n_copy = TOP_K * rows
    step = pl.program_id(0)
    slot = step % 2

    def row_copy(s, i, src):
        return pltpu.make_async_copy(y_hbm.at[pl.ds(src, 1)], buf_ref.at[s, pl.ds(i, 1)], sem.at[s])

    def issue(s, idx_ref):
        def body(i, carry):
            row_copy(s, i, idx_ref[0, 0, i]).start()
            return carry
        lax.fori_loop(0, n_copy, body, 0, unroll=DMA_ISSUE_UNROLL)

    @pl.when(step == 0)
    def _():
        issue(0, pos_ref)

    def drain(i, carry):
        row_copy(slot, i, 0).wait()
        return carry

    lax.fori_loop(0, n_copy, drain, 0, unroll=DMA_ISSUE_UNROLL)

    half = x_ref.shape[1] // 2
    chunk = COMBINE_CHUNK
    n_chunks = rows // chunk
    per_chunk = n_copy // n_chunks

    def chunk_body(prefetch):
        def body(c, carry):
            if prefetch:
                for r in range(per_chunk):
                    i = c * per_chunk + r
                    row_copy(1 - slot, i, pos_next_ref[0, 0, i]).start()
            r0 = pl.multiple_of(c * chunk, chunk)
            lo0, hi0 = _unpack_bf16_pair(buf_ref[slot, pl.ds(r0, chunk), :])
            lo1, hi1 = _unpack_bf16_pair(buf_ref[slot, pl.ds(rows + r0, chunk), :])
            w = w_ref[pl.ds(r0, chunk), :]
            w0 = w[:, 0:1]
            w1 = w[:, 1:2]
            o_ref[pl.ds(r0, chunk), 0:half] = x_ref[pl.ds(r0, chunk), 0:half] + (w0 * lo0 + w1 * lo1)
            o_ref[pl.ds(r0, chunk), half:2 * half] = (
                x_ref[pl.ds(r0, chunk), half:2 * half] + (w0 * hi0 + w1 * hi1))
            return carry
        return body

    @pl.when(step + 1 < n_steps)
    def _():
        lax.fori_loop(0, n_chunks, chunk_body(True), 0)

    @pl.when(step + 1 >= n_steps)
    def _():
        lax.fori_loop(0, n_chunks, chunk_body(False), 0)


def _moe_combine(x1, ys, pos, weights):
    n_tok, d = x1.shape
    rows = COMBINE_ROWS
    n_blk = n_tok // rows
    pos_b = pos.reshape(n_blk, rows, TOP_K).transpose(0, 2, 1).reshape(n_blk, 1, TOP_K * rows)
    idx_block = (1, 1, TOP_K * rows)
    return pl.pallas_call(
        functools.partial(_combine_kernel, n_steps=n_blk),
        grid=(n_blk,),
        in_specs=[pl.BlockSpec(idx_block, lambda i: (i, 0, 0), memory_space=pltpu.SMEM),
                  pl.BlockSpec(idx_block, lambda i: (jnp.minimum(i + 1, n_blk - 1), 0, 0),
                               memory_space=pltpu.SMEM),
                  pl.BlockSpec((rows, d), lambda i: (i, 0)),
                  pl.BlockSpec((rows, TOP_K), lambda i: (i, 0)),
                  pl.BlockSpec(memory_space=pl.ANY)],
        out_specs=pl.BlockSpec((rows, d), lambda i: (i, 0)),
        out_shape=jax.ShapeDtypeStruct((n_tok, d), F32),
        scratch_shapes=[pltpu.VMEM((2, TOP_K * rows, d // 2), jnp.uint32),
                        pltpu.SemaphoreType.DMA((2,))],
        compiler_params=_cparams(("arbitrary",)),
        name="moe_combine",
    )(pos_b, pos_b, x1, weights, ys)


def _route(logits, n_tok):
    glog = logits[:, :N_GROUPS]
    gprob = jax.nn.softmax(glog, axis=-1)
    g_sel = jnp.argmax(glog, axis=-1)
    p_g = jnp.take_along_axis(gprob, g_sel[:, None], axis=1)[:, 0]
    elog = logits[:, N_GROUPS:N_GROUPS + N_EXPERTS].reshape(n_tok, N_GROUPS, EXPERTS_PER_GROUP)
    elog = jnp.take_along_axis(elog, g_sel[:, None, None], axis=1)[:, 0]
    top_v, top_i = lax.top_k(elog, TOP_K)
    weights = p_g[:, None] * jax.nn.softmax(top_v, axis=-1)
    expert_ids = (g_sel[:, None] * EXPERTS_PER_GROUP + top_i).astype(jnp.int32)

    n_assign = n_tok * TOP_K
    rows = MOE_ROWS
    flat_e = expert_ids.reshape(n_assign)
    a_ids = jnp.arange(n_assign, dtype=jnp.int32)
    skey = jnp.sort(flat_e * n_assign + a_ids)
    order = skey % n_assign
    inv = jnp.argsort(order).astype(jnp.int32)
    experts = jnp.arange(N_EXPERTS, dtype=jnp.int32)
    counts = jnp.sum((flat_e[:, None] == experts[None, :]).astype(jnp.int32), axis=0)
    blocks = (counts + rows - 1) // rows
    pad_end = jnp.cumsum(blocks * rows)
    pad_start = pad_end - blocks * rows
    start = jnp.cumsum(counts) - counts
    n_slots = n_assign + N_EXPERTS * rows
    n_blk = n_slots // rows
    n_used = jnp.sum(blocks).astype(jnp.int32)
    blk_ids = jnp.arange(n_blk, dtype=jnp.int32)
    blk_expert = jnp.minimum(
        jnp.sum((pad_end[None, :] <= (blk_ids * rows)[:, None]).astype(jnp.int32), axis=1),
        N_EXPERTS - 1)
    blk_cnt = jnp.clip(counts[blk_expert] - (blk_ids * rows - pad_start[blk_expert]), 0, rows)
    blk_cnt = jnp.where(blk_ids < n_used, blk_cnt, 0).astype(jnp.int32)
    blk_first = (start[blk_expert] + blk_ids * rows - pad_start[blk_expert]).astype(jnp.int32)
    tok_sorted = (order // TOP_K).astype(jnp.int32)
    blk_expert = jnp.where(blk_ids < n_used, blk_expert,
                           blk_expert[jnp.maximum(n_used - 1, 0)]).astype(jnp.int32)
    shift = jnp.sum(jnp.where(flat_e[:, None] == experts[None, :], (pad_start - start)[None, :], 0),
                    axis=1)
    pos = (inv + shift).astype(jnp.int32).reshape(n_tok, TOP_K)
    return blk_expert, blk_cnt, blk_first, tok_sorted, pos, weights


def kernel(x, norm1_gain, w_in, attn_q_norm_gain, attn_k_norm_gain, mlstm_gate_bias,
           mlstm_head_norm_gain, branch_gate_bias, w_attn_branch, w_mlstm_branch, w_out,
           norm2_gain, w_router_group, b_router_group, w_router_expert, b_router_expert,
           w_expert_gate, w_expert_up, w_expert_down):
    bsz, s_len, d = x.shape
    assert bsz == 1
    depth = w_in.shape[0]
    xs = x.reshape(s_len, d)
    w_in_t = jnp.swapaxes(w_in, 1, 2)
    n_proj = OFF_GA + 2 * d
    tn = 512

    def proj_rows(j):
        return j * tn + jnp.where(j * tn >= OFF_GA, N_ML_GATES, 0)

    for l in range(depth):
        h = _rmsnorm(xs, norm1_gain[l])
        proj = _matmul_nt_paired(h, w_in_t, l, proj_rows, n_proj, BF16, 1024, tn, "in_proj")
        gates = _matmul_nt(h, w_in_t, l, lambda j: OFF_MG + j * LANES, LANES, F32, 1024, LANES,
                           "gate_proj")[:, :N_ML_GATES]

        att = _attention(proj, attn_q_norm_gain[l], attn_k_norm_gain[l])

        hs = _mlstm(proj, gates, mlstm_gate_bias[l])
        mem = _mlstm_out(hs, proj, mlstm_head_norm_gain[l])

        merged = _branch_mix(att, mem, w_attn_branch, w_mlstm_branch, l, proj, branch_gate_bias)
        x1 = _matmul_residual(merged, w_out, l, xs, 1024, 512)

        w_r = jnp.concatenate([w_router_group[l], w_router_expert[l]], axis=1)
        b_r = jnp.concatenate([b_router_group[l], b_router_expert[l]])
        n_r = w_r.shape[1]
        w_r = jnp.pad(w_r, ((0, 0), (0, LANES - n_r)))
        b_r = jnp.pad(b_r, (0, LANES - n_r)).reshape(1, LANES)
        h2, logits = _norm_router(x1, norm2_gain[l], w_r, b_r)

        blk_expert, blk_cnt, blk_first, tok_sorted, pos, weights = _route(logits, s_len)
        ys = _moe_experts(h2, blk_expert, blk_cnt, blk_first, tok_sorted,
                          w_expert_gate, w_expert_up, w_expert_down, l)
        xs = _moe_combine(x1, ys, pos, weights)
    return xs.reshape(bsz, s_len, d)
```

```python
import functools

import jax
import jax.numpy as jnp
from jax import lax
from jax.experimental import pallas as pl
from jax.experimental.pallas import tpu as pltpu

F32 = jnp.float32
BF16 = jnp.bfloat16

EPS = 1e-6
NEG = -1e30
LANES = 128

ATT_PATTERNS = ((128, 1), (512, 4), (2048, 16))
N_ATT_GROUPS = 3
ATT_HEADS = 8
ATT_HEAD_DIM = 128
ATT_GROUP_W = ATT_HEADS * ATT_HEAD_DIM
ATT_W = N_ATT_GROUPS * ATT_GROUP_W
ROPE_THETA = 10000.0
ATT_Q_TILE = 128
ATT_TILE_UNROLL = 16

ML_HEADS = 8
ML_QK_DIM = 256
ML_V_DIM = 512
ML_QK_W = ML_HEADS * ML_QK_DIM
ML_V_W = ML_HEADS * ML_V_DIM
N_ML_GATES = 4 * ML_HEADS
GATE_SOFTCAP = 15.0
ML_TILE = 256
ML_HEAD_GROUP = 4
ML_V_WINDOWS = 2
ML_OUT_HEADS = 2

N_GROUPS = 8
EXPERTS_PER_GROUP = 8
N_EXPERTS = N_GROUPS * EXPERTS_PER_GROUP
TOP_K = 2
MOE_ROWS = 512
MOE_SUB_ROWS = 128
MOE_FF_TILE = 256
MOE_OUT_TILE = 1024
COMBINE_ROWS = 256
COMBINE_CHUNK = 8
DMA_ISSUE_UNROLL = 8

OFF_AQ = 0
OFF_AK = OFF_AQ + ATT_W
OFF_AV = OFF_AK + ATT_W
OFF_MQ = OFF_AV + ATT_W
OFF_MK = OFF_MQ + ML_QK_W
OFF_MV = OFF_MK + ML_QK_W
OFF_MO = OFF_MV + ML_V_W
OFF_MG = OFF_MO + ML_V_W
OFF_GA = OFF_MG

VMEM_LIMIT = 56 * 1024 * 1024


def _cparams(sem, vmem=VMEM_LIMIT):
    return pltpu.CompilerParams(dimension_semantics=sem, vmem_limit_bytes=vmem)


def _rmsnorm_kernel(x_ref, g_ref, o_ref):
    x = x_ref[...]
    ms = jnp.mean(x * x, axis=-1, keepdims=True)
    o_ref[...] = (x * lax.rsqrt(ms + EPS) * g_ref[...]).astype(o_ref.dtype)


def _rmsnorm(x, gain, tm=256):
    m, d = x.shape
    return pl.pallas_call(
        _rmsnorm_kernel,
        grid=(m // tm,),
        in_specs=[pl.BlockSpec((tm, d), lambda i: (i, 0)),
                  pl.BlockSpec((1, d), lambda i: (0, 0))],
        out_specs=pl.BlockSpec((tm, d), lambda i: (i, 0)),
        out_shape=jax.ShapeDtypeStruct((m, d), BF16),
        compiler_params=_cparams(("parallel",)),
        name="rmsnorm",
    )(x, gain.reshape(1, d))


def _mm_kernel(a_ref, w_ref, o_ref, wb_ref):
    @pl.when(pl.program_id(1) == 0)
    def _():
        wb_ref[...] = w_ref[0].astype(BF16)

    o_ref[...] = jnp.dot(a_ref[...], wb_ref[...], preferred_element_type=F32).astype(o_ref.dtype)


def _matmul(a, w, layer, col0, n, out_dtype, tm, tn, name):
    m, k = a.shape
    cb = col0 // tn
    return pl.pallas_call(
        _mm_kernel,
        grid=(n // tn, m // tm),
        in_specs=[pl.BlockSpec((tm, k), lambda j, i: (i, 0)),
                  pl.BlockSpec((1, k, tn), lambda j, i: (layer, 0, cb + j))],
        out_specs=pl.BlockSpec((tm, tn), lambda j, i: (i, j)),
        out_shape=jax.ShapeDtypeStruct((m, n), out_dtype),
        scratch_shapes=[pltpu.VMEM((k, tn), BF16)],
        compiler_params=_cparams(("parallel", "arbitrary")),
        name=name,
    )(a, w)


def _mm_nt_kernel(a_ref, wt_ref, o_ref, wb_ref):
    @pl.when(pl.program_id(1) == 0)
    def _():
        wb_ref[...] = wt_ref[0].astype(BF16)

    o_ref[...] = lax.dot_general(a_ref[...], wb_ref[...], (((1,), (1,)), ((), ())),
                                 preferred_element_type=F32).astype(o_ref.dtype)


def _matmul_nt(a, w_t, layer, row_of_tile, n, out_dtype, tm, tn, name):
    m, k = a.shape
    return pl.pallas_call(
        _mm_nt_kernel,
        grid=(n // tn, m // tm),
        in_specs=[pl.BlockSpec((tm, k), lambda j, i: (i, 0)),
                  pl.BlockSpec((pl.Element(1), pl.Element(tn), pl.Element(k)),
                               lambda j, i: (layer, pl.multiple_of(row_of_tile(j), 8), 0))],
        out_specs=pl.BlockSpec((tm, tn), lambda j, i: (i, j)),
        out_shape=jax.ShapeDtypeStruct((m, n), out_dtype),
        scratch_shapes=[pltpu.VMEM((tn, k), BF16)],
        compiler_params=_cparams(("parallel", "arbitrary")),
        name=name,
    )(a, w_t)


def _mm_nt_pair_kernel(a_ref, wt_ref, o_ref, wb_ref):
    c = pl.program_id(2)

    @pl.when(pl.program_id(1) == 0)
    def _():
        wb_ref[c] = wt_ref[0].astype(BF16)

    o_ref[...] = lax.dot_general(a_ref[...], wb_ref[c], (((1,), (1,)), ((), ())),
                                 preferred_element_type=F32).astype(o_ref.dtype)


def _matmul_nt_paired(a, w_t, layer, row_of_tile, n, out_dtype, tm, tn, name):
    m, k = a.shape
    assert n % (2 * tn) == 0

    def w_rows(jj, i, c):
        return pl.multiple_of(row_of_tile(2 * jj + jnp.where(i == 0, c, 1)), 8)

    return pl.pallas_call(
        _mm_nt_pair_kernel,
        grid=(n // (2 * tn), m // tm, 2),
        in_specs=[pl.BlockSpec((tm, k), lambda jj, i, c: (i, 0)),
                  pl.BlockSpec((pl.Element(1), pl.Element(tn), pl.Element(k)),
                               lambda jj, i, c: (layer, w_rows(jj, i, c), 0))],
        out_specs=pl.BlockSpec((tm, tn), lambda jj, i, c: (i, 2 * jj + c)),
        out_shape=jax.ShapeDtypeStruct((m, n), out_dtype),
        scratch_shapes=[pltpu.VMEM((2, tn, k), BF16)],
        compiler_params=_cparams(("arbitrary", "arbitrary", "arbitrary")),
        name=name,
    )(a, w_t)


def _mm_res_kernel(a_ref, w_ref, r_ref, o_ref, wb_ref):
    @pl.when(pl.program_id(1) == 0)
    def _():
        wb_ref[...] = w_ref[0].astype(BF16)

    o_ref[...] = r_ref[...] + jnp.dot(a_ref[...], wb_ref[...], preferred_element_type=F32)


def _matmul_residual(a, w, layer, res, tm, tn):
    m, k = a.shape
    n = w.shape[2]
    return pl.pallas_call(
        _mm_res_kernel,
        grid=(n // tn, m // tm),
        in_specs=[pl.BlockSpec((tm, k), lambda j, i: (i, 0)),
                  pl.BlockSpec((1, k, tn), lambda j, i: (layer, 0, j)),
                  pl.BlockSpec((tm, tn), lambda j, i: (i, j))],
        out_specs=pl.BlockSpec((tm, tn), lambda j, i: (i, j)),
        out_shape=jax.ShapeDtypeStruct((m, n), F32),
        scratch_shapes=[pltpu.VMEM((k, tn), BF16)],
        compiler_params=_cparams(("parallel", "arbitrary")),
        name="out_proj",
    )(a, w, res)


def _attn_kernel(q_ref, k_ref, v_ref, tab_ref, qg_ref, kg_ref, o_ref,
                 qn_ref, kn_ref, vf_ref, acc_ref, lse_ref, band_ref, *, s_len):
    grp = pl.program_id(1)
    tq = ATT_Q_TILE
    hd = ATT_HEAD_DIM
    half = hd // 2
    low = lax.broadcasted_iota(jnp.int32, (tq, hd), 1) < half
    row = lax.broadcasted_iota(jnp.int32, (hd, hd), 0)
    col = lax.broadcasted_iota(jnp.int32, (hd, hd), 1)
    ones = jnp.ones((hd, hd), BF16)
    rot = jnp.where(row == col + half, -1.0, jnp.where(col == row + half, 1.0, 0.0)).astype(BF16)

    def prep(i, carry):
        rows = pl.ds(pl.multiple_of(i * tq, tq), tq)
        tab = tab_ref[rows, :]
        swapped = pltpu.roll(tab, half, axis=1)
        c = jnp.where(low, tab, swapped)
        s = jnp.where(low, swapped, tab)
        for src, g_ref, dst in ((q_ref, qg_ref, qn_ref), (k_ref, kg_ref, kn_ref)):
            xf = src[rows, :].astype(F32)
            ssq = jnp.dot((xf * xf).astype(BF16), ones, preferred_element_type=F32)
            y = xf * lax.rsqrt(ssq * (1.0 / hd) + EPS) * g_ref[0]
            y_rot = jnp.dot(y.astype(BF16), rot, preferred_element_type=F32)
            dst[rows, :] = y * c + y_rot * s
        vf_ref[rows, :] = v_ref[rows, :].astype(F32)
        return carry

    lax.fori_loop(0, s_len // tq, prep, 0, unroll=4)

    def run_group(window, dilation, first):
        radius = window // (2 * dilation)
        sub_len = s_len // dilation
        win = tq + 2 * radius
        tiles_per_class = sub_len // tq
        rel = (lax.broadcasted_iota(jnp.int32, (tq, win), 1)
               - lax.broadcasted_iota(jnp.int32, (tq, win), 0))
        for case in range(3):
            band_ref[case] = jnp.where(jnp.abs(rel - case * radius) <= radius, 0.0, NEG)

        def rows_of(start, size):
            if dilation == 1:
                return pl.ds(start, size)
            return pl.ds(start, size, stride=dilation)

        def tile(idx, carry):
            r = idx // tiles_per_class
            q0 = (idx % tiles_per_class) * tq
            k0 = jnp.clip(q0 - radius, 0, sub_len - win)
            q_rows = rows_of(r + dilation * q0, tq)
            k_rows = rows_of(r + dilation * k0, win)
            q = qn_ref[q_rows, :].astype(BF16)
            k = kn_ref[k_rows, :].astype(BF16)
            v = vf_ref[k_rows, :].astype(BF16)
            s = lax.dot_general(q, k, (((1,), (1,)), ((), ())), preferred_element_type=F32)
            s = s + band_ref[(q0 - k0) // radius]
            m = jnp.max(s, axis=-1, keepdims=True)
            p = jnp.exp(s - m)
            den = jnp.sum(p, axis=-1, keepdims=True)
            o = jnp.dot(p.astype(BF16), v, preferred_element_type=F32) / den
            lse = jnp.broadcast_to(m + jnp.log(den), (tq, ATT_HEAD_DIM))
            if first:
                acc_ref[q_rows, :] = o
                lse_ref[q_rows, :] = lse
            else:
                old_l = lse_ref[q_rows, :]
                top = jnp.maximum(old_l, lse)
                a = jnp.exp(old_l - top)
                b = jnp.exp(lse - top)
                tot = a + b
                acc_ref[q_rows, :] = (a * acc_ref[q_rows, :] + b * o) / tot
                lse_ref[q_rows, :] = top + jnp.log(tot)
            return carry

        lax.fori_loop(0, s_len // tq, tile, 0, unroll=ATT_TILE_UNROLL)

    for gi, (window, dilation) in enumerate(ATT_PATTERNS):
        @pl.when(grp == gi)
        def _(window=window, dilation=dilation, gi=gi):
            run_group(window, dilation, gi == 0)

    @pl.when(grp == N_ATT_GROUPS - 1)
    def _():
        o_ref[...] = acc_ref[...].astype(o_ref.dtype)


def _attention(proj, q_gain, k_gain):
    s_len = proj.shape[0]
    half = ATT_HEAD_DIM // 2
    inv_freq = ROPE_THETA ** (-jnp.arange(half, dtype=F32) / half)
    ang = jnp.arange(s_len).astype(F32)[:, None] * inv_freq[None, :]
    table = jnp.concatenate([jnp.cos(ang), jnp.sin(ang)], axis=1)
    qb, kb, vb = OFF_AQ // LANES, OFF_AK // LANES, OFF_AV // LANES
    blk = (s_len, ATT_HEAD_DIM)
    radii = {window // (2 * dilation) for window, dilation in ATT_PATTERNS}
    assert len(radii) == 1
    win = ATT_Q_TILE + 2 * radii.pop()
    gain_spec = pl.BlockSpec((1, 1, ATT_HEAD_DIM), lambda h, g: (g, 0, 0))
    q_scaled = q_gain.reshape(N_ATT_GROUPS, 1, ATT_HEAD_DIM) * (ATT_HEAD_DIM ** -0.5)
    return pl.pallas_call(
        functools.partial(_attn_kernel, s_len=s_len),
        grid=(ATT_HEADS, N_ATT_GROUPS),
        in_specs=[pl.BlockSpec(blk, lambda h, g: (0, qb + g * ATT_HEADS + h)),
                  pl.BlockSpec(blk, lambda h, g: (0, kb + g * ATT_HEADS + h)),
                  pl.BlockSpec(blk, lambda h, g: (0, vb + g * ATT_HEADS + h)),
                  pl.BlockSpec(blk, lambda h, g: (0, 0)),
                  gain_spec, gain_spec],
        out_specs=pl.BlockSpec(blk, lambda h, g: (0, h)),
        out_shape=jax.ShapeDtypeStruct((s_len, ATT_GROUP_W), BF16),
        scratch_shapes=[pltpu.VMEM(blk, F32)] * 5 + [pltpu.VMEM((3, ATT_Q_TILE, win), F32)],
        compiler_params=_cparams(("parallel", "arbitrary")),
        name="dilated_attn",
    )(proj, proj, proj, table, q_scaled, k_gain.reshape(N_ATT_GROUPS, 1, ATT_HEAD_DIM))


def _tile_scan(x, pos, lc, combine, reverse, fill):
    n = x.shape[1]
    k = 1
    while k < lc:
        if reverse:
            shifted = pltpu.roll(x, n - k, axis=1)
            ok = pos < lc - k
        else:
            shifted = pltpu.roll(x, k, axis=1)
            ok = pos >= k
        x = combine(x, jnp.where(ok, shifted, fill))
        k *= 2
    return x


def _mlstm_gate_kernel(ig_ref, f_ref, bias_ref, o_ref, *, lc):
    n_rows, s_len = ig_ref.shape
    n_tiles = s_len // lc
    shape = (n_rows, s_len)
    bw = lax.broadcasted_iota(jnp.int32, shape, 0) >= ML_HEADS
    lane = lax.broadcasted_iota(jnp.int32, shape, 1)
    pos = lane % lc
    tile = lane // lc

    ig = GATE_SOFTCAP * jnp.tanh((ig_ref[...] + bias_ref[:, 0:1]) / GATE_SOFTCAP)
    f = GATE_SOFTCAP * jnp.tanh((f_ref[...] + bias_ref[:, 1:2]) / GATE_SOFTCAP)
    lf = jnp.minimum(f, 0.0) - jnp.log1p(jnp.exp(-jnp.abs(f)))

    pre = _tile_scan(lf, pos, lc, jnp.add, False, 0.0)
    suf = _tile_scan(lf, pos, lc, jnp.add, True, 0.0)
    b = jnp.where(bw, suf, pre)
    tot = pre + suf - lf
    a = ig - b
    pmax = _tile_scan(a, pos, lc, jnp.maximum, False, -jnp.inf)
    smax = _tile_scan(a, pos, lc, jnp.maximum, True, -jnp.inf)
    cm = jnp.where(bw, smax, pmax)
    cmax = jnp.maximum(pmax, smax)

    def tiles_scan(reverse):
        alpha, beta = tot, cmax + tot
        step = 1
        while step < n_tiles:
            sh = step * lc
            if reverse:
                pa, pb = pltpu.roll(alpha, s_len - sh, axis=1), pltpu.roll(beta, s_len - sh, axis=1)
                ok = tile < n_tiles - step
            else:
                pa, pb = pltpu.roll(alpha, sh, axis=1), pltpu.roll(beta, sh, axis=1)
                ok = tile >= step
            alpha, beta = (jnp.where(ok, pa + alpha, alpha),
                           jnp.where(ok, jnp.maximum(pb + alpha, beta), beta))
            step *= 2
        return jnp.maximum(alpha, beta)

    last_f = tiles_scan(False)
    last_r = tiles_scan(True)
    m_last = jnp.where(bw, last_r, last_f)
    prev_f = jnp.where(tile >= 1, pltpu.roll(last_f, lc, axis=1), 0.0)
    prev_r = jnp.where(tile < n_tiles - 1, pltpu.roll(last_r, s_len - lc, axis=1), 0.0)
    m_prev = jnp.where(bw, prev_r, prev_f)
    mm = jnp.maximum(m_prev, cm)

    o_ref[0] = a
    o_ref[1] = mm
    o_ref[2] = b + mm
    o_ref[3] = m_prev
    o_ref[4] = tot - m_last
    o_ref[5] = tot + m_prev - m_last
    o_ref[6] = jnp.zeros(shape, F32)
    o_ref[7] = jnp.zeros(shape, F32)


def _mlstm_scan_kernel(q_ref, k_ref, *rest, n_v):
    v_refs = rest[:n_v]
    g_ref, mask_ref, o_ref, c_ref = rest[n_v:]
    dk, dv = ML_QK_DIM, ML_V_DIM

    @pl.when(pl.program_id(1) == 0)
    def _():
        c_ref[...] = jnp.zeros_like(c_ref)

    lc = q_ref.shape[0]
    mask = mask_ref[0]
    ones = jnp.ones((lc, LANES), BF16)
    for hd in range(ML_HEAD_GROUP):
        rows = g_ref[hd]

        def column(r, rows=rows):
            return jnp.broadcast_to(rows[r:r + 1, :], (8, lc)).T[:, 0:1]

        a_row = rows[0:1, :]
        a_col, mm_col, m_col = column(0), column(1), column(2)
        m_prev = rows[3:4, 0:1]
        d_mat = jnp.exp(a_row - mm_col + mask)
        decay_q = jnp.exp(m_prev - mm_col)

        q = q_ref[:, hd * dk:(hd + 1) * dk]
        k = k_ref[:, hd * dk:(hd + 1) * dk] * (dk ** -0.5)
        per_window = ML_HEAD_GROUP // n_v
        v_cols = slice((hd % per_window) * dv, (hd % per_window + 1) * dv)
        v_ext = jnp.concatenate([v_refs[hd // per_window][:, v_cols], ones], axis=1)
        s = lax.dot_general(q, k, (((1,), (1,)), ((), ())), preferred_element_type=F32) * d_mat
        c_old = c_ref[hd]
        acc = jnp.dot(s.astype(BF16), v_ext, preferred_element_type=F32)
        acc = acc + decay_q * jnp.dot(q, c_old.astype(BF16), preferred_element_type=F32)
        den = acc[:, dv:dv + LANES]
        inv = 1.0 / jnp.maximum(jnp.abs(den), jnp.exp(-m_col))
        for part in range(dv // LANES):
            cols = slice(part * LANES, (part + 1) * LANES)
            o_ref[0, :, hd * dv + part * LANES:hd * dv + (part + 1) * LANES] = (
                acc[:, cols] * inv).astype(o_ref.dtype)

        w_col = jnp.exp(a_col + rows[4:5, 0:1])
        decay_c = jnp.exp(rows[5:6, 0:1])
        kw = k.astype(F32) * w_col
        c_ref[hd] = decay_c * c_old + lax.dot_general(
            kw.astype(BF16), v_ext, (((0,), (0,)), ((), ())), preferred_element_type=F32)


def _mlstm(proj, gates, gate_bias):
    s_len = proj.shape[0]
    lc = ML_TILE
    nt = s_len // lc
    n_rows = 2 * ML_HEADS
    gt = gates.T.reshape(2, 2, ML_HEADS, s_len)
    bias = gate_bias.reshape(2, 2, ML_HEADS)
    gate_rows = pl.pallas_call(
        functools.partial(_mlstm_gate_kernel, lc=lc),
        out_shape=jax.ShapeDtypeStruct((8, n_rows, s_len), F32),
        compiler_params=_cparams(None),
        name="mlstm_gates",
    )(gt[:, 0].reshape(n_rows, s_len), gt[:, 1].reshape(n_rows, s_len),
      jnp.stack([bias[:, 0].reshape(n_rows), bias[:, 1].reshape(n_rows)], axis=1))
    gate_rows = gate_rows.transpose(1, 0, 2)

    t_idx = jnp.arange(lc)[:, None]
    s_idx = jnp.arange(lc)[None, :]
    mask = jnp.stack([jnp.where(s_idx <= t_idx, 0.0, NEG), jnp.where(s_idx >= t_idx, 0.0, NEG)]).astype(F32)

    hg = ML_HEAD_GROUP
    groups = ML_HEADS // hg

    def tile_of(dg, t):
        return jnp.where(dg >= groups, nt - 1 - t, t)

    parts = ML_V_WINDOWS
    v_width = hg // parts * ML_V_DIM
    assert OFF_MQ % (hg * ML_QK_DIM) == 0 and OFF_MK % (hg * ML_QK_DIM) == 0
    assert OFF_MV % v_width == 0 and hg % parts == 0
    qb, kb = OFF_MQ // (hg * ML_QK_DIM), OFF_MK // (hg * ML_QK_DIM)
    vb = OFF_MV // v_width

    def v_spec(part):
        return pl.BlockSpec((lc, v_width),
                            lambda dg, t: (tile_of(dg, t), vb + parts * (dg % groups) + part))

    return pl.pallas_call(
        functools.partial(_mlstm_scan_kernel, n_v=parts),
        grid=(2 * groups, nt),
        in_specs=[pl.BlockSpec((lc, hg * ML_QK_DIM), lambda dg, t: (tile_of(dg, t), qb + dg % groups)),
                  pl.BlockSpec((lc, hg * ML_QK_DIM), lambda dg, t: (tile_of(dg, t), kb + dg % groups))]
                 + [v_spec(part) for part in range(parts)]
                 + [pl.BlockSpec((hg, 8, lc), lambda dg, t: (dg, 0, tile_of(dg, t))),
                    pl.BlockSpec((1, lc, lc), lambda dg, t: (dg // groups, 0, 0))],
        out_specs=pl.BlockSpec((1, lc, hg * ML_V_DIM),
                               lambda dg, t: (dg // groups, tile_of(dg, t), dg % groups)),
        out_shape=jax.ShapeDtypeStruct((2, s_len, ML_V_W), BF16),
        scratch_shapes=[pltpu.VMEM((hg, ML_QK_DIM, ML_V_DIM + LANES), F32)],
        compiler_params=_cparams(("parallel", "arbitrary")),
        name="mlstm_scan",
    )(proj, proj, *([proj] * parts), gate_rows, mask)


def _mlstm_out_kernel(h_ref, mo_ref, g_ref, o_ref):
    for hd in range(ML_OUT_HEADS):
        cols = slice(hd * ML_V_DIM, (hd + 1) * ML_V_DIM)
        h = h_ref[0, :, cols].astype(F32) + h_ref[1, :, cols].astype(F32)
        y = h * lax.rsqrt(jnp.mean(h * h, axis=-1, keepdims=True) + EPS) * g_ref[0, :, cols]
        o_ref[:, cols] = (y * jax.nn.sigmoid(mo_ref[:, cols].astype(F32))).astype(o_ref.dtype)


def _mlstm_out(hs, proj, head_gain, tm=1024):
    s_len = proj.shape[0]
    width = ML_OUT_HEADS * ML_V_DIM
    assert OFF_MO % width == 0
    ob = OFF_MO // width
    return pl.pallas_call(
        _mlstm_out_kernel,
        grid=(s_len // tm, ML_V_W // width),
        in_specs=[pl.BlockSpec((2, tm, width), lambda i, h: (0, i, h)),
                  pl.BlockSpec((tm, width), lambda i, h: (i, ob + h)),
                  pl.BlockSpec((1, 1, width), lambda i, h: (h, 0, 0))],
        out_specs=pl.BlockSpec((tm, width), lambda i, h: (i, h)),
        out_shape=jax.ShapeDtypeStruct((s_len, ML_V_W), BF16),
        compiler_params=_cparams(("parallel", "parallel")),
        name="mlstm_out",
    )(hs, proj, head_gain.reshape(ML_V_W // width, 1, width))


def _branch_kernel(att_ref, mem_ref, wa_ref, wm_ref, ga_ref, gm_ref, ba_ref, bm_ref, o_ref,
                   wab_ref, wmb_ref):
    @pl.when(pl.program_id(1) == 0)
    def _():
        wab_ref[...] = wa_ref[0].astype(BF16)
        wmb_ref[...] = wm_ref[0].astype(BF16)

    ya = jnp.dot(att_ref[...], wab_ref[...], preferred_element_type=F32)
    ym = jnp.dot(mem_ref[...], wmb_ref[...], preferred_element_type=F32)
    g_a = jax.nn.sigmoid(ga_ref[...].astype(F32) + ba_ref[0])
    g_m = jax.nn.sigmoid(gm_ref[...].astype(F32) + bm_ref[0])
    o_ref[...] = (g_a * ya + g_m * ym).astype(o_ref.dtype)


def _branch_mix(att, mem, w_a, w_m, layer, proj, gate_bias, tm=512, tn=512):
    m = att.shape[0]
    ka, km, d = w_a.shape[1], w_m.shape[1], w_a.shape[2]
    bias = gate_bias.reshape(-1, 1, d)
    gab = OFF_GA // tn
    gmb = (OFF_GA + d) // tn
    return pl.pallas_call(
        _branch_kernel,
        grid=(d // tn, m // tm),
        in_specs=[pl.BlockSpec((tm, ka), lambda j, i: (i, 0)),
                  pl.BlockSpec((tm, km), lambda j, i: (i, 0)),
                  pl.BlockSpec((1, ka, tn), lambda j, i: (layer, 0, j)),
                  pl.BlockSpec((1, km, tn), lambda j, i: (layer, 0, j)),
                  pl.BlockSpec((tm, tn), lambda j, i: (i, gab + j)),
                  pl.BlockSpec((tm, tn), lambda j, i: (i, gmb + j)),
                  pl.BlockSpec((1, 1, tn), lambda j, i: (2 * layer, 0, j)),
                  pl.BlockSpec((1, 1, tn), lambda j, i: (2 * layer + 1, 0, j))],
        out_specs=pl.BlockSpec((tm, tn), lambda j, i: (i, j)),
        out_shape=jax.ShapeDtypeStruct((m, d), BF16),
        scratch_shapes=[pltpu.VMEM((ka, tn), BF16), pltpu.VMEM((km, tn), BF16)],
        compiler_params=_cparams(("parallel", "arbitrary")),
        name="branch_mix",
    )(att, mem, w_a, w_m, proj, proj, bias, bias)


def _pack_bf16_pair(lo, hi):
    lo_bits = lax.bitcast_convert_type(lo.astype(BF16).astype(F32), jnp.uint32)
    hi_bits = lax.bitcast_convert_type(hi.astype(BF16).astype(F32), jnp.uint32)
    return (lo_bits >> 16) | (hi_bits & jnp.uint32(0xFFFF0000))


def _unpack_bf16_pair(word):
    lo = lax.bitcast_convert_type(word << 16, F32)
    hi = lax.bitcast_convert_type(word & jnp.uint32(0xFFFF0000), F32)
    return lo, hi


def _norm_router_kernel(x_ref, g_ref, whi_ref, wlo_ref, br_ref, h_ref, lg_ref):
    x = x_ref[...]
    h = x * lax.rsqrt(jnp.mean(x * x, axis=-1, keepdims=True) + EPS) * g_ref[...]
    half = h.shape[1] // 2
    h_ref[...] = _pack_bf16_pair(h[:, :half], h[:, half:])
    h_hi = h.astype(BF16)
    h_lo = (h - h_hi.astype(F32)).astype(BF16)
    w_hi = whi_ref[...]
    logits = jnp.dot(h_hi, w_hi, preferred_element_type=F32)
    logits = logits + jnp.dot(h_lo, w_hi, preferred_element_type=F32)
    logits = logits + jnp.dot(h_hi, wlo_ref[...], preferred_element_type=F32)
    lg_ref[...] = logits + br_ref[...]


def _norm_router(x, gain, w_r, b_r, tm=256):
    m, d = x.shape
    n = w_r.shape[1]
    w_hi = w_r.astype(BF16)
    w_lo = (w_r - w_hi.astype(F32)).astype(BF16)
    return pl.pallas_call(
        _norm_router_kernel,
        grid=(m // tm,),
        in_specs=[pl.BlockSpec((tm, d), lambda i: (i, 0)),
                  pl.BlockSpec((1, d), lambda i: (0, 0)),
                  pl.BlockSpec((d, n), lambda i: (0, 0)),
                  pl.BlockSpec((d, n), lambda i: (0, 0)),
                  pl.BlockSpec((1, n), lambda i: (0, 0))],
        out_specs=[pl.BlockSpec((tm, d // 2), lambda i: (i, 0)),
                   pl.BlockSpec((tm, n), lambda i: (i, 0))],
        out_shape=[jax.ShapeDtypeStruct((m, d // 2), jnp.uint32),
                   jax.ShapeDtypeStruct((m, n), F32)],
        compiler_params=_cparams(("parallel",)),
        name="norm_router",
    )(x, gain.reshape(1, d), w_hi, w_lo, b_r)


def _moe_kernel(blk_e_ref, cnt_ref, first_ref, tok_ref, h_hbm, wg_ref, wu_ref,
                wd_lo_ref, wd_hi_ref, y_ref, xg_ref, xb_ref, act_ref, sem, *, n_ff_tiles, n_steps,
                n_blk):
    b = pl.program_id(0)
    j = pl.program_id(1)
    rows = MOE_ROWS
    tf = MOE_FF_TILE
    grp = DMA_ISSUE_UNROLL
    cnt = cnt_ref[b]
    used = cnt > 0
    up_phase = j < n_ff_tiles
    n_sub = (cnt + MOE_SUB_ROWS - 1) // MOE_SUB_ROWS

    def row_copy(i, tok):
        return pltpu.make_async_copy(h_hbm.at[pl.ds(tok, 1)], xg_ref.at[pl.ds(i, 1)], sem)

    def issue_rows(blk):
        first = first_ref[blk]
        last_entry = tok_ref.shape[0] - 1

        def body(gi, carry):
            for r in range(grp):
                i = gi * grp + r
                row_copy(i, tok_ref[jnp.minimum(first + i, last_entry)]).start()
            return carry
        lax.fori_loop(0, (cnt_ref[blk] + grp - 1) // grp, body, 0)

    def wait_rows(count):
        def body(gi, carry):
            for r in range(grp):
                row_copy(gi * grp + r, 0).wait()
            return carry
        lax.fori_loop(0, (count + grp - 1) // grp, body, 0)

    def land_rows(blk):
        wait_rows(cnt_ref[blk])
        half = xg_ref.shape[1]
        lo, hi = _unpack_bf16_pair(xg_ref[...])
        xb_ref[blk % 2, :, 0:half] = lo.astype(BF16)
        xb_ref[blk % 2, :, half:2 * half] = hi.astype(BF16)

    @pl.when(jnp.logical_and(b == 0, j == 0))
    def _():
        xg_ref[...] = jnp.zeros_like(xg_ref)
        issue_rows(0)
        land_rows(0)
        if n_blk > 1:
            issue_rows(1)

    @pl.when(jnp.logical_and(used, j == n_steps - 1))
    def _():
        @pl.when(b + 1 < n_blk)
        def _():
            @pl.when(cnt_ref[jnp.minimum(b + 1, n_blk - 1)] > 0)
            def _():
                land_rows(b + 1)

            @pl.when(b + 2 < n_blk)
            def _():
                issue_rows(jnp.minimum(b + 2, n_blk - 1))

    for m in range(MOE_SUB_ROWS, rows + 1, MOE_SUB_ROWS):
        live = jnp.logical_and(used, n_sub == m // MOE_SUB_ROWS)

        @pl.when(jnp.logical_and(live, up_phase))
        def _(m=m):
            x = xb_ref[b % 2, 0:m, :]
            g = jnp.dot(x, wg_ref[0, 0].astype(BF16), preferred_element_type=F32)
            u = jnp.dot(x, wu_ref[0, 0].astype(BF16), preferred_element_type=F32)
            act_ref[j, 0:m, :] = (g * jax.nn.sigmoid(g) * u).astype(BF16)

        @pl.when(jnp.logical_and(live, jnp.logical_not(up_phase)))
        def _(m=m):
            def down(wd_ref):
                acc = jnp.dot(act_ref[0, 0:m, :], wd_ref[0, 0, 0:tf, :].astype(BF16),
                              preferred_element_type=F32)
                for t in range(1, n_ff_tiles):
                    acc = acc + jnp.dot(act_ref[t, 0:m, :],
                                        wd_ref[0, 0, t * tf:(t + 1) * tf, :].astype(BF16),
                                        preferred_element_type=F32)
                return acc

            y_ref[0:m, :] = _pack_bf16_pair(down(wd_lo_ref), down(wd_hi_ref))
            if m < rows:
                y_ref[m:rows, :] = jnp.zeros((rows - m, y_ref.shape[1]), y_ref.dtype)

    @pl.when(jnp.logical_and(jnp.logical_not(used), jnp.logical_not(up_phase)))
    def _():
        y_ref[...] = jnp.zeros_like(y_ref)


def _moe_experts(h2p, blk_expert, blk_cnt, blk_first, tok_sorted, w_gate, w_up, w_down, layer):
    d = 2 * h2p.shape[1]
    d_ff = w_gate.shape[3]
    rows = MOE_ROWS
    n_blk = blk_expert.shape[0]
    tf = MOE_FF_TILE
    tn = MOE_OUT_TILE
    n_f = d_ff // tf
    n_o = d // 2 // tn

    def ff_tile(b, j, cnt):
        return jnp.where(cnt[b] > 0, jnp.minimum(j, n_f - 1), n_f - 1)

    def out_tile(b, j):
        return jnp.clip(j - n_f, 0, n_o - 1)

    def w_down_block(b, j, be, cnt, hi):
        stay = jnp.logical_or(j < n_f, cnt[b] == 0)
        expert = jnp.where(j < n_f, be[jnp.maximum(b - 1, 0)], be[b])
        return layer, expert, 0, hi * n_o + jnp.where(stay, n_o - 1, j - n_f)

    grid_spec = pltpu.PrefetchScalarGridSpec(
        num_scalar_prefetch=4,
        grid=(n_blk, n_f + n_o),
        in_specs=[pl.BlockSpec(memory_space=pl.ANY),
                  pl.BlockSpec((1, 1, d, tf),
                               lambda b, j, be, cnt, *_: (layer, be[b], 0, ff_tile(b, j, cnt))),
                  pl.BlockSpec((1, 1, d, tf),
                               lambda b, j, be, cnt, *_: (layer, be[b], 0, ff_tile(b, j, cnt))),
                  pl.BlockSpec((1, 1, d_ff, tn),
                               lambda b, j, be, cnt, *_: w_down_block(b, j, be, cnt, 0)),
                  pl.BlockSpec((1, 1, d_ff, tn),
                               lambda b, j, be, cnt, *_: w_down_block(b, j, be, cnt, 1))],
        out_specs=pl.BlockSpec((rows, tn), lambda b, j, *_: (b, out_tile(b, j))),
        scratch_shapes=[pltpu.VMEM((rows, d // 2), jnp.uint32),
                        pltpu.VMEM((2, rows, d), BF16),
                        pltpu.VMEM((n_f, rows, tf), BF16),
                        pltpu.SemaphoreType.DMA(())],
    )
    return pl.pallas_call(
        functools.partial(_moe_kernel, n_ff_tiles=n_f, n_steps=n_f + n_o, n_blk=n_blk),
        grid_spec=grid_spec,
        out_shape=jax.ShapeDtypeStruct((n_blk * rows, d // 2), jnp.uint32),
        compiler_params=_cparams(("arbitrary", "arbitrary")),
        name="moe_experts",
    )(blk_expert, blk_cnt, blk_first, tok_sorted, h2p, w_gate, w_up, w_down, w_down)


def _combine_kernel(pos_ref, pos_next_ref, x_ref, w_ref, y_hbm, o_ref, buf_ref, sem, *, n_steps):
    rows = COMBINE_ROWS
    n_copy = TOP_K * rows
    step = pl.program_id(0)
    slot = step % 2

    def row_copy(s, i, src):
        return pltpu.make_async_copy(y_hbm.at[pl.ds(src, 1)], buf_ref.at[s, pl.ds(i, 1)], sem.at[s])

    def issue(s, idx_ref):
        def body(i, carry):
            row_copy(s, i, idx_ref[0, 0, i]).start()
            return carry
        lax.fori_loop(0, n_copy, body, 0, unroll=DMA_ISSUE_UNROLL)

    @pl.when(step == 0)
    def _():
        issue(0, pos_ref)

    def drain(i, carry):
        row_copy(slot, i, 0).wait()
        return carry

    lax.fori_loop(0, n_copy, drain, 0, unroll=DMA_ISSUE_UNROLL)

    half = x_ref.shape[1] // 2
    chunk = COMBINE_CHUNK
    n_chunks = rows // chunk
    per_chunk = n_copy // n_chunks

    def chunk_body(prefetch):
        def body(c, carry):
            if prefetch:
                for r in range(per_chunk):
                    i = c * per_chunk + r
                    row_copy(1 - slot, i, pos_next_ref[0, 0, i]).start()
            r0 = pl.multiple_of(c * chunk, chunk)
            lo0, hi0 = _unpack_bf16_pair(buf_ref[slot, pl.ds(r0, chunk), :])
            lo1, hi1 = _unpack_bf16_pair(buf_ref[slot, pl.ds(rows + r0, chunk), :])
            w = w_ref[pl.ds(r0, chunk), :]
            w0 = w[:, 0:1]
            w1 = w[:, 1:2]
            o_ref[pl.ds(r0, chunk), 0:half] = x_ref[pl.ds(r0, chunk), 0:half] + (w0 * lo0 + w1 * lo1)
            o_ref[pl.ds(r0, chunk), half:2 * half] = (
                x_ref[pl.ds(r0, chunk), half:2 * half] + (w0 * hi0 + w1 * hi1))
            return carry
        return body

    @pl.when(step + 1 < n_steps)
    def _():
        lax.fori_loop(0, n_chunks, chunk_body(True), 0)

    @pl.when(step + 1 >= n_steps)
    def _():
        lax.fori_loop(0, n_chunks, chunk_body(False), 0)


def _moe_combine(x1, ys, pos, weights):
    n_tok, d = x1.shape
    rows = COMBINE_ROWS
    n_blk = n_tok // rows
    pos_b = pos.reshape(n_blk, rows, TOP_K).transpose(0, 2, 1).reshape(n_blk, 1, TOP_K * rows)
    idx_block = (1, 1, TOP_K * rows)
    return pl.pallas_call(
        functools.partial(_combine_kernel, n_steps=n_blk),
        grid=(n_blk,),
        in_specs=[pl.BlockSpec(idx_block, lambda i: (i, 0, 0), memory_space=pltpu.SMEM),
                  pl.BlockSpec(idx_block, lambda i: (jnp.minimum(i + 1, n_blk - 1), 0, 0),
                               memory_space=pltpu.SMEM),
                  pl.BlockSpec((rows, d), lambda i: (i, 0)),
                  pl.BlockSpec((rows, TOP_K), lambda i: (i, 0)),
                  pl.BlockSpec(memory_space=pl.ANY)],
        out_specs=pl.BlockSpec((rows, d), lambda i: (i, 0)),
        out_shape=jax.ShapeDtypeStruct((n_tok, d), F32),
        scratch_shapes=[pltpu.VMEM((2, TOP_K * rows, d // 2), jnp.uint32),
                        pltpu.SemaphoreType.DMA((2,))],
        compiler_params=_cparams(("arbitrary",)),
        name="moe_combine",
    )(pos_b, pos_b, x1, weights, ys)


def _route(logits, n_tok):
    glog = logits[:, :N_GROUPS]
    gprob = jax.nn.softmax(glog, axis=-1)
    g_sel = jnp.argmax(glog, axis=-1)
    p_g = jnp.take_along_axis(gprob, g_sel[:, None], axis=1)[:, 0]
    elog = logits[:, N_GROUPS:N_GROUPS + N_EXPERTS].reshape(n_tok, N_GROUPS, EXPERTS_PER_GROUP)
    elog = jnp.take_along_axis(elog, g_sel[:, None, None], axis=1)[:, 0]
    top_v, top_i = lax.top_k(elog, TOP_K)
    weights = p_g[:, None] * jax.nn.softmax(top_v, axis=-1)
    expert_ids = (g_sel[:, None] * EXPERTS_PER_GROUP + top_i).astype(jnp.int32)

    n_assign = n_tok * TOP_K
    rows = MOE_ROWS
    flat_e = expert_ids.reshape(n_assign)
    a_ids = jnp.arange(n_assign, dtype=jnp.int32)
    skey = jnp.sort(flat_e * n_assign + a_ids)
    order = skey % n_assign
    inv = jnp.argsort(order).astype(jnp.int32)
    experts = jnp.arange(N_EXPERTS, dtype=jnp.int32)
    counts = jnp.sum((flat_e[:, None] == experts[None, :]).astype(jnp.int32), axis=0)
    blocks = (counts + rows - 1) // rows
    pad_end = jnp.cumsum(blocks * rows)
    pad_start = pad_end - blocks * rows
    start = jnp.cumsum(counts) - counts
    n_slots = n_assign + N_EXPERTS * rows
    n_blk = n_slots // rows
    n_used = jnp.sum(blocks).astype(jnp.int32)
    blk_ids = jnp.arange(n_blk, dtype=jnp.int32)
    blk_expert = jnp.minimum(
        jnp.sum((pad_end[None, :] <= (blk_ids * rows)[:, None]).astype(jnp.int32), axis=1),
        N_EXPERTS - 1)
    blk_cnt = jnp.clip(counts[blk_expert] - (blk_ids * rows - pad_start[blk_expert]), 0, rows)
    blk_cnt = jnp.where(blk_ids < n_used, blk_cnt, 0).astype(jnp.int32)
    blk_first = (start[blk_expert] + blk_ids * rows - pad_start[blk_expert]).astype(jnp.int32)
    tok_sorted = (order // TOP_K).astype(jnp.int32)
    blk_expert = jnp.where(blk_ids < n_used, blk_expert,
                           blk_expert[jnp.maximum(n_used - 1, 0)]).astype(jnp.int32)
    shift = jnp.sum(jnp.where(flat_e[:, None] == experts[None, :], (pad_start - start)[None, :], 0),
                    axis=1)
    pos = (inv + shift).astype(jnp.int32).reshape(n_tok, TOP_K)
    return blk_expert, blk_cnt, blk_first, tok_sorted, pos, weights


def kernel(x, norm1_gain, w_in, attn_q_norm_gain, attn_k_norm_gain, mlstm_gate_bias,
           mlstm_head_norm_gain, branch_gate_bias, w_attn_branch, w_mlstm_branch, w_out,
           norm2_gain, w_router_group, b_router_group, w_router_expert, b_router_expert,
           w_expert_gate, w_expert_up, w_expert_down):
    bsz, s_len, d = x.shape
    assert bsz == 1
    depth = w_in.shape[0]
    xs = x.reshape(s_len, d)
    w_in_t = jnp.swapaxes(w_in, 1, 2)
    n_proj = OFF_GA + 2 * d
    tn = 512

    def proj_rows(j):
        return j * tn + jnp.where(j * tn >= OFF_GA, N_ML_GATES, 0)

    for l in range(depth):
        h = _rmsnorm(xs, norm1_gain[l])
        proj = _matmul_nt_paired(h, w_in_t, l, proj_rows, n_proj, BF16, 1024, tn, "in_proj")
        gates = _matmul_nt(h, w_in_t, l, lambda j: OFF_MG + j * LANES, LANES, F32, 1024, LANES,
                           "gate_proj")[:, :N_ML_GATES]

        att = _attention(proj, attn_q_norm_gain[l], attn_k_norm_gain[l])

        hs = _mlstm(proj, gates, mlstm_gate_bias[l])
        mem = _mlstm_out(hs, proj, mlstm_head_norm_gain[l])

        merged = _branch_mix(att, mem, w_attn_branch, w_mlstm_branch, l, proj, branch_gate_bias)
        x1 = _matmul_residual(merged, w_out, l, xs, 1024, 512)

        w_r = jnp.concatenate([w_router_group[l], w_router_expert[l]], axis=1)
        b_r = jnp.concatenate([b_router_group[l], b_router_expert[l]])
        n_r = w_r.shape[1]
        w_r = jnp.pad(w_r, ((0, 0), (0, LANES - n_r)))
        b_r = jnp.pad(b_r, (0, LANES - n_r)).reshape(1, LANES)
        h2, logits = _norm_router(x1, norm2_gain[l], w_r, b_r)

        blk_expert, blk_cnt, blk_first, tok_sorted, pos, weights = _route(logits, s_len)
        ys = _moe_experts(h2, blk_expert, blk_cnt, blk_first, tok_sorted,
                          w_expert_gate, w_expert_up, w_expert_down, l)
        xs = _moe_combine(x1, ys, pos, weights)
    return xs.reshape(bsz, s_len, d)
```

```python
import functools

import jax
import jax.numpy as jnp
from jax import lax
from jax.experimental import pallas as pl
from jax.experimental.pallas import tpu as pltpu

F32 = jnp.float32
BF16 = jnp.bfloat16

EPS = 1e-6
NEG = -1e30
LANES = 128

ATT_PATTERNS = ((128, 1), (512, 4), (2048, 16))
N_ATT_GROUPS = 3
ATT_HEADS = 8
ATT_HEAD_DIM = 128
ATT_GROUP_W = ATT_HEADS * ATT_HEAD_DIM
ATT_W = N_ATT_GROUPS * ATT_GROUP_W
ROPE_THETA = 10000.0
ATT_Q_TILE = 128
ATT_TILE_UNROLL = 16

ML_HEADS = 8
ML_QK_DIM = 256
ML_V_DIM = 512
ML_QK_W = ML_HEADS * ML_QK_DIM
ML_V_W = ML_HEADS * ML_V_DIM
N_ML_GATES = 4 * ML_HEADS
GATE_SOFTCAP = 15.0
ML_TILE = 256
ML_HEAD_GROUP = 4
ML_V_WINDOWS = 2
ML_OUT_HEADS = 2

N_GROUPS = 8
EXPERTS_PER_GROUP = 8
N_EXPERTS = N_GROUPS * EXPERTS_PER_GROUP
TOP_K = 2
MOE_ROWS = 512
MOE_SUB_ROWS = 128
MOE_FF_TILE = 256
MOE_OUT_TILE = 1024
COMBINE_ROWS = 256
COMBINE_CHUNK = 8
DMA_ISSUE_UNROLL = 8

OFF_AQ = 0
OFF_AK = OFF_AQ + ATT_W
OFF_AV = OFF_AK + ATT_W
OFF_MQ = OFF_AV + ATT_W
OFF_MK = OFF_MQ + ML_QK_W
OFF_MV = OFF_MK + ML_QK_W
OFF_MO = OFF_MV + ML_V_W
OFF_MG = OFF_MO + ML_V_W
OFF_GA = OFF_MG

VMEM_LIMIT = 56 * 1024 * 1024


def _cparams(sem, vmem=VMEM_LIMIT):
    return pltpu.CompilerParams(dimension_semantics=sem, vmem_limit_bytes=vmem)


def _rmsnorm_kernel(x_ref, g_ref, o_ref):
    x = x_ref[...]
    ms = jnp.mean(x * x, axis=-1, keepdims=True)
    o_ref[...] = (x * lax.rsqrt(ms + EPS) * g_ref[...]).astype(o_ref.dtype)


def _rmsnorm(x, gain, tm=256):
    m, d = x.shape
    return pl.pallas_call(
        _rmsnorm_kernel,
        grid=(m // tm,),
        in_specs=[pl.BlockSpec((tm, d), lambda i: (i, 0)),
                  pl.BlockSpec((1, d), lambda i: (0, 0))],
        out_specs=pl.BlockSpec((tm, d), lambda i: (i, 0)),
        out_shape=jax.ShapeDtypeStruct((m, d), BF16),
        compiler_params=_cparams(("parallel",)),
        name="rmsnorm",
    )(x, gain.reshape(1, d))


def _mm_nt_kernel(a_ref, wt_ref, o_ref, wb_ref):
    @pl.when(pl.program_id(1) == 0)
    def _():
        wb_ref[...] = wt_ref[0].astype(BF16)

    o_ref[...] = lax.dot_general(a_ref[...], wb_ref[...], (((1,), (1,)), ((), ())),
                                 preferred_element_type=F32).astype(o_ref.dtype)


def _matmul_nt(a, w_t, layer, row_of_tile, n, out_dtype, tm, tn, name):
    m, k = a.shape
    return pl.pallas_call(
        _mm_nt_kernel,
        grid=(n // tn, m // tm),
        in_specs=[pl.BlockSpec((tm, k), lambda j, i: (i, 0)),
                  pl.BlockSpec((pl.Element(1), pl.Element(tn), pl.Element(k)),
                               lambda j, i: (layer, pl.multiple_of(row_of_tile(j), 8), 0))],
        out_specs=pl.BlockSpec((tm, tn), lambda j, i: (i, j)),
        out_shape=jax.ShapeDtypeStruct((m, n), out_dtype),
        scratch_shapes=[pltpu.VMEM((tn, k), BF16)],
        compiler_params=_cparams(("parallel", "arbitrary")),
        name=name,
    )(a, w_t)


def _mm_nt_pair_kernel(a_ref, wt_ref, o_ref, wb_ref):
    c = pl.program_id(2)

    @pl.when(pl.program_id(1) == 0)
    def _():
        wb_ref[c] = wt_ref[0].astype(BF16)

    o_ref[...] = lax.dot_general(a_ref[...], wb_ref[c], (((1,), (1,)), ((), ())),
                                 preferred_element_type=F32).astype(o_ref.dtype)


def _matmul_nt_paired(a, w_t, layer, row_of_tile, n, out_dtype, tm, tn, name):
    m, k = a.shape
    assert n % (2 * tn) == 0

    def w_rows(jj, i, c):
        return pl.multiple_of(row_of_tile(2 * jj + jnp.where(i == 0, c, 1)), 8)

    return pl.pallas_call(
        _mm_nt_pair_kernel,
        grid=(n // (2 * tn), m // tm, 2),
        in_specs=[pl.BlockSpec((tm, k), lambda jj, i, c: (i, 0)),
                  pl.BlockSpec((pl.Element(1), pl.Element(tn), pl.Element(k)),
                               lambda jj, i, c: (layer, w_rows(jj, i, c), 0))],
        out_specs=pl.BlockSpec((tm, tn), lambda jj, i, c: (i, 2 * jj + c)),
        out_shape=jax.ShapeDtypeStruct((m, n), out_dtype),
        scratch_shapes=[pltpu.VMEM((2, tn, k), BF16)],
        compiler_params=_cparams(("arbitrary", "arbitrary", "arbitrary")),
        name=name,
    )(a, w_t)


def _mm_res_kernel(a_ref, w_ref, r_ref, o_ref, wb_ref):
    @pl.when(pl.program_id(1) == 0)
    def _():
        wb_ref[...] = w_ref[0].astype(BF16)

    o_ref[...] = r_ref[...] + jnp.dot(a_ref[...], wb_ref[...], preferred_element_type=F32)


def _matmul_residual(a, w, layer, res, tm, tn):
    m, k = a.shape
    n = w.shape[2]
    return pl.pallas_call(
        _mm_res_kernel,
        grid=(n // tn, m // tm),
        in_specs=[pl.BlockSpec((tm, k), lambda j, i: (i, 0)),
                  pl.BlockSpec((1, k, tn), lambda j, i: (layer, 0, j)),
                  pl.BlockSpec((tm, tn), lambda j, i: (i, j))],
        out_specs=pl.BlockSpec((tm, tn), lambda j, i: (i, j)),
        out_shape=jax.ShapeDtypeStruct((m, n), F32),
        scratch_shapes=[pltpu.VMEM((k, tn), BF16)],
        compiler_params=_cparams(("parallel", "arbitrary")),
        name="out_proj",
    )(a, w, res)


def _attn_kernel(q_ref, k_ref, v_ref, tab_ref, qg_ref, kg_ref, o_ref,
                 qn_ref, kn_ref, vf_ref, acc_ref, lse_ref, band_ref, *, s_len):
    grp = pl.program_id(1)
    tq = ATT_Q_TILE
    hd = ATT_HEAD_DIM
    half = hd // 2
    low = lax.broadcasted_iota(jnp.int32, (tq, hd), 1) < half
    row = lax.broadcasted_iota(jnp.int32, (hd, hd), 0)
    col = lax.broadcasted_iota(jnp.int32, (hd, hd), 1)
    mean_w = jnp.full((hd, hd), 1.0 / hd, BF16)
    rot = jnp.where(row == col + half, -1.0, jnp.where(col == row + half, 1.0, 0.0)).astype(BF16)

    def prep(i, carry):
        rows = pl.ds(pl.multiple_of(i * tq, tq), tq)
        tab = tab_ref[rows, :]
        swapped = pltpu.roll(tab, half, axis=1)
        c = jnp.where(low, tab, swapped)
        s = jnp.where(low, swapped, tab)
        for src, g_ref, dst in ((q_ref, qg_ref, qn_ref), (k_ref, kg_ref, kn_ref)):
            xf = src[rows, :].astype(F32)
            ms = jnp.dot((xf * xf).astype(BF16), mean_w, preferred_element_type=F32)
            y = xf * lax.rsqrt(ms + EPS) * g_ref[0]
            y_rot = jnp.dot(y.astype(BF16), rot, preferred_element_type=F32)
            dst[rows, :] = y * c + y_rot * s
        vf_ref[rows, :] = v_ref[rows, :].astype(F32)
        return carry

    lax.fori_loop(0, s_len // tq, prep, 0, unroll=4)

    def run_group(window, dilation, first):
        radius = window // (2 * dilation)
        sub_len = s_len // dilation
        win = tq + 2 * radius
        tiles_per_class = sub_len // tq
        rel = (lax.broadcasted_iota(jnp.int32, (tq, win), 1)
               - lax.broadcasted_iota(jnp.int32, (tq, win), 0))
        for case in range(3):
            band_ref[case] = jnp.where(jnp.abs(rel - case * radius) <= radius, 0.0, NEG)

        def rows_of(start, size):
            if dilation == 1:
                return pl.ds(start, size)
            return pl.ds(start, size, stride=dilation)

        def tile(idx, carry):
            r = idx // tiles_per_class
            q0 = (idx % tiles_per_class) * tq
            k0 = jnp.clip(q0 - radius, 0, sub_len - win)
            q_rows = rows_of(r + dilation * q0, tq)
            k_rows = rows_of(r + dilation * k0, win)
            q = qn_ref[q_rows, :].astype(BF16)
            k = kn_ref[k_rows, :].astype(BF16)
            v = vf_ref[k_rows, :].astype(BF16)
            s = lax.dot_general(q, k, (((1,), (1,)), ((), ())), preferred_element_type=F32)
            s = s + band_ref[(q0 - k0) // radius]
            m = jnp.max(s, axis=-1, keepdims=True)
            p = jnp.exp(s - m)
            den = jnp.sum(p, axis=-1, keepdims=True)
            o = jnp.dot(p.astype(BF16), v, preferred_element_type=F32) / den
            lse = jnp.broadcast_to(m + jnp.log(den), (tq, ATT_HEAD_DIM))
            if first:
                acc_ref[q_rows, :] = o
                lse_ref[q_rows, :] = lse
            else:
                old_l = lse_ref[q_rows, :]
                top = jnp.maximum(old_l, lse)
                a = jnp.exp(old_l - top)
                b = jnp.exp(lse - top)
                tot = a + b
                acc_ref[q_rows, :] = (a * acc_ref[q_rows, :] + b * o) / tot
                lse_ref[q_rows, :] = top + jnp.log(tot)
            return carry

        lax.fori_loop(0, s_len // tq, tile, 0, unroll=ATT_TILE_UNROLL)

    for gi, (window, dilation) in enumerate(ATT_PATTERNS):
        @pl.when(grp == gi)
        def _(window=window, dilation=dilation, gi=gi):
            run_group(window, dilation, gi == 0)

    @pl.when(grp == N_ATT_GROUPS - 1)
    def _():
        o_ref[...] = acc_ref[...].astype(o_ref.dtype)


def _attention(proj, q_gain, k_gain):
    s_len = proj.shape[0]
    half = ATT_HEAD_DIM // 2
    inv_freq = ROPE_THETA ** (-jnp.arange(half, dtype=F32) / half)
    ang = jnp.arange(s_len).astype(F32)[:, None] * inv_freq[None, :]
    table = jnp.concatenate([jnp.cos(ang), jnp.sin(ang)], axis=1)
    qb, kb, vb = OFF_AQ // LANES, OFF_AK // LANES, OFF_AV // LANES
    blk = (s_len, ATT_HEAD_DIM)
    radii = {window // (2 * dilation) for window, dilation in ATT_PATTERNS}
    assert len(radii) == 1
    win = ATT_Q_TILE + 2 * radii.pop()
    gain_spec = pl.BlockSpec((1, 1, ATT_HEAD_DIM), lambda h, g: (g, 0, 0))
    q_scaled = q_gain.reshape(N_ATT_GROUPS, 1, ATT_HEAD_DIM) * (ATT_HEAD_DIM ** -0.5)
    return pl.pallas_call(
        functools.partial(_attn_kernel, s_len=s_len),
        grid=(ATT_HEADS, N_ATT_GROUPS),
        in_specs=[pl.BlockSpec(blk, lambda h, g: (0, qb + g * ATT_HEADS + h)),
                  pl.BlockSpec(blk, lambda h, g: (0, kb + g * ATT_HEADS + h)),
                  pl.BlockSpec(blk, lambda h, g: (0, vb + g * ATT_HEADS + h)),
                  pl.BlockSpec(blk, lambda h, g: (0, 0)),
                  gain_spec, gain_spec],
        out_specs=pl.BlockSpec(blk, lambda h, g: (0, h)),
        out_shape=jax.ShapeDtypeStruct((s_len, ATT_GROUP_W), BF16),
        scratch_shapes=[pltpu.VMEM(blk, F32)] * 5 + [pltpu.VMEM((3, ATT_Q_TILE, win), F32)],
        compiler_params=_cparams(("parallel", "arbitrary")),
        name="dilated_attn",
    )(proj, proj, proj, table, q_scaled, k_gain.reshape(N_ATT_GROUPS, 1, ATT_HEAD_DIM))


def _tile_scan(x, pos, lc, combine, reverse, fill):
    n = x.shape[1]
    k = 1
    while k < lc:
        if reverse:
            shifted = pltpu.roll(x, n - k, axis=1)
            ok = pos < lc - k
        else:
            shifted = pltpu.roll(x, k, axis=1)
            ok = pos >= k
        x = combine(x, jnp.where(ok, shifted, fill))
        k *= 2
    return x


def _mlstm_gate_kernel(ig_ref, f_ref, bias_ref, o_ref, *, lc):
    n_rows, s_len = ig_ref.shape
    n_tiles = s_len // lc
    shape = (n_rows, s_len)
    bw = lax.broadcasted_iota(jnp.int32, shape, 0) >= ML_HEADS
    lane = lax.broadcasted_iota(jnp.int32, shape, 1)
    pos = lane % lc
    tile = lane // lc

    ig = GATE_SOFTCAP * jnp.tanh((ig_ref[...] + bias_ref[:, 0:1]) / GATE_SOFTCAP)
    f = GATE_SOFTCAP * jnp.tanh((f_ref[...] + bias_ref[:, 1:2]) / GATE_SOFTCAP)
    lf = jnp.minimum(f, 0.0) - jnp.log1p(jnp.exp(-jnp.abs(f)))

    pre = _tile_scan(lf, pos, lc, jnp.add, False, 0.0)
    suf = _tile_scan(lf, pos, lc, jnp.add, True, 0.0)
    b = jnp.where(bw, suf, pre)
    tot = pre + suf - lf
    a = ig - b
    pmax = _tile_scan(a, pos, lc, jnp.maximum, False, -jnp.inf)
    smax = _tile_scan(a, pos, lc, jnp.maximum, True, -jnp.inf)
    cm = jnp.where(bw, smax, pmax)
    cmax = jnp.maximum(pmax, smax)

    def tiles_scan(reverse):
        alpha, beta = tot, cmax + tot
        step = 1
        while step < n_tiles:
            sh = step * lc
            if reverse:
                pa, pb = pltpu.roll(alpha, s_len - sh, axis=1), pltpu.roll(beta, s_len - sh, axis=1)
                ok = tile < n_tiles - step
            else:
                pa, pb = pltpu.roll(alpha, sh, axis=1), pltpu.roll(beta, sh, axis=1)
                ok = tile >= step
            alpha, beta = (jnp.where(ok, pa + alpha, alpha),
                           jnp.where(ok, jnp.maximum(pb + alpha, beta), beta))
            step *= 2
        return jnp.maximum(alpha, beta)

    last_f = tiles_scan(False)
    last_r = tiles_scan(True)
    m_last = jnp.where(bw, last_r, last_f)
    prev_f = jnp.where(tile >= 1, pltpu.roll(last_f, lc, axis=1), 0.0)
    prev_r = jnp.where(tile < n_tiles - 1, pltpu.roll(last_r, s_len - lc, axis=1), 0.0)
    m_prev = jnp.where(bw, prev_r, prev_f)
    mm = jnp.maximum(m_prev, cm)

    o_ref[0] = a
    o_ref[1] = mm
    o_ref[2] = b + mm
    o_ref[3] = m_prev
    o_ref[4] = tot - m_last
    o_ref[5] = tot + m_prev - m_last
    o_ref[6] = jnp.zeros(shape, F32)
    o_ref[7] = jnp.zeros(shape, F32)


def _mlstm_scan_kernel(q_ref, k_ref, *rest, n_v):
    v_refs = rest[:n_v]
    g_ref, mask_ref, o_ref, c_ref = rest[n_v:]
    dk, dv = ML_QK_DIM, ML_V_DIM

    @pl.when(pl.program_id(1) == 0)
    def _():
        c_ref[...] = jnp.zeros_like(c_ref)

    lc = q_ref.shape[0]
    mask = mask_ref[0]
    ones = jnp.ones((lc, LANES), BF16)
    for hd in range(ML_HEAD_GROUP):
        rows = g_ref[hd]

        def column(r, rows=rows):
            return jnp.broadcast_to(rows[r:r + 1, :], (8, lc)).T[:, 0:1]

        a_row = rows[0:1, :]
        a_col, mm_col, m_col = column(0), column(1), column(2)
        m_prev = rows[3:4, 0:1]
        d_mat = jnp.exp(a_row - mm_col + mask)
        decay_q = jnp.exp(m_prev - mm_col)

        q = q_ref[:, hd * dk:(hd + 1) * dk]
        k = k_ref[:, hd * dk:(hd + 1) * dk] * (dk ** -0.5)
        per_window = ML_HEAD_GROUP // n_v
        v_cols = slice((hd % per_window) * dv, (hd % per_window + 1) * dv)
        v_ext = jnp.concatenate([v_refs[hd // per_window][:, v_cols], ones], axis=1)
        s = lax.dot_general(q, k, (((1,), (1,)), ((), ())), preferred_element_type=F32) * d_mat
        c_old = c_ref[hd]
        acc = jnp.dot(s.astype(BF16), v_ext, preferred_element_type=F32)
        acc = acc + decay_q * jnp.dot(q, c_old.astype(BF16), preferred_element_type=F32)
        den = acc[:, dv:dv + LANES]
        inv = 1.0 / jnp.maximum(jnp.abs(den), jnp.exp(-m_col))
        for part in range(dv // LANES):
            cols = slice(part * LANES, (part + 1) * LANES)
            o_ref[0, :, hd * dv + part * LANES:hd * dv + (part + 1) * LANES] = (
                acc[:, cols] * inv).astype(o_ref.dtype)

        w_col = jnp.exp(a_col + rows[4:5, 0:1])
        decay_c = jnp.exp(rows[5:6, 0:1])
        kw = k.astype(F32) * w_col
        c_ref[hd] = decay_c * c_old + lax.dot_general(
            kw.astype(BF16), v_ext, (((0,), (0,)), ((), ())), preferred_element_type=F32)


def _mlstm(proj, gates, gate_bias):
    s_len = proj.shape[0]
    lc = ML_TILE
    nt = s_len // lc
    n_rows = 2 * ML_HEADS
    gt = gates.T.reshape(2, 2, ML_HEADS, s_len)
    bias = gate_bias.reshape(2, 2, ML_HEADS)
    gate_rows = pl.pallas_call(
        functools.partial(_mlstm_gate_kernel, lc=lc),
        out_shape=jax.ShapeDtypeStruct((8, n_rows, s_len), F32),
        compiler_params=_cparams(None),
        name="mlstm_gates",
    )(gt[:, 0].reshape(n_rows, s_len), gt[:, 1].reshape(n_rows, s_len),
      jnp.stack([bias[:, 0].reshape(n_rows), bias[:, 1].reshape(n_rows)], axis=1))
    gate_rows = gate_rows.transpose(1, 0, 2)

    t_idx = jnp.arange(lc)[:, None]
    s_idx = jnp.arange(lc)[None, :]
    mask = jnp.stack([jnp.where(s_idx <= t_idx, 0.0, NEG), jnp.where(s_idx >= t_idx, 0.0, NEG)]).astype(F32)

    hg = ML_HEAD_GROUP
    groups = ML_HEADS // hg

    def tile_of(dg, t):
        return jnp.where(dg >= groups, nt - 1 - t, t)

    parts = ML_V_WINDOWS
    v_width = hg // parts * ML_V_DIM
    assert OFF_MQ % (hg * ML_QK_DIM) == 0 and OFF_MK % (hg * ML_QK_DIM) == 0
    assert OFF_MV % v_width == 0 and hg % parts == 0
    qb, kb = OFF_MQ // (hg * ML_QK_DIM), OFF_MK // (hg * ML_QK_DIM)
    vb = OFF_MV // v_width

    def v_spec(part):
        return pl.BlockSpec((lc, v_width),
                            lambda dg, t: (tile_of(dg, t), vb + parts * (dg % groups) + part))

    return pl.pallas_call(
        functools.partial(_mlstm_scan_kernel, n_v=parts),
        grid=(2 * groups, nt),
        in_specs=[pl.BlockSpec((lc, hg * ML_QK_DIM), lambda dg, t: (tile_of(dg, t), qb + dg % groups)),
                  pl.BlockSpec((lc, hg * ML_QK_DIM), lambda dg, t: (tile_of(dg, t), kb + dg % groups))]
                 + [v_spec(part) for part in range(parts)]
                 + [pl.BlockSpec((hg, 8, lc), lambda dg, t: (dg, 0, tile_of(dg, t))),
                    pl.BlockSpec((1, lc, lc), lambda dg, t: (dg // groups, 0, 0))],
        out_specs=pl.BlockSpec((1, lc, hg * ML_V_DIM),
                               lambda dg, t: (dg // groups, tile_of(dg, t), dg % groups)),
        out_shape=jax.ShapeDtypeStruct((2, s_len, ML_V_W), BF16),
        scratch_shapes=[pltpu.VMEM((hg, ML_QK_DIM, ML_V_DIM + LANES), F32)],
        compiler_params=_cparams(("parallel", "arbitrary")),
        name="mlstm_scan",
    )(proj, proj, *([proj] * parts), gate_rows, mask)


def _mlstm_out_kernel(h_ref, mo_ref, g_ref, o_ref):
    for hd in range(ML_OUT_HEADS):
        cols = slice(hd * ML_V_DIM, (hd + 1) * ML_V_DIM)
        h = h_ref[0, :, cols].astype(F32) + h_ref[1, :, cols].astype(F32)
        y = h * lax.rsqrt(jnp.mean(h * h, axis=-1, keepdims=True) + EPS) * g_ref[0, :, cols]
        o_ref[:, cols] = (y * jax.nn.sigmoid(mo_ref[:, cols].astype(F32))).astype(o_ref.dtype)


def _mlstm_out(hs, proj, head_gain, tm=1024):
    s_len = proj.shape[0]
    width = ML_OUT_HEADS * ML_V_DIM
    assert OFF_MO % width == 0
    ob = OFF_MO // width
    return pl.pallas_call(
        _mlstm_out_kernel,
        grid=(s_len // tm, ML_V_W // width),
        in_specs=[pl.BlockSpec((2, tm, width), lambda i, h: (0, i, h)),
                  pl.BlockSpec((tm, width), lambda i, h: (i, ob + h)),
                  pl.BlockSpec((1, 1, width), lambda i, h: (h, 0, 0))],
        out_specs=pl.BlockSpec((tm, width), lambda i, h: (i, h)),
        out_shape=jax.ShapeDtypeStruct((s_len, ML_V_W), BF16),
        compiler_params=_cparams(("parallel", "parallel")),
        name="mlstm_out",
    )(hs, proj, head_gain.reshape(ML_V_W // width, 1, width))


def _branch_kernel(att_ref, mem_ref, wa_ref, wm_ref, ga_ref, gm_ref, ba_ref, bm_ref, o_ref,
                   wab_ref, wmb_ref):
    @pl.when(pl.program_id(1) == 0)
    def _():
        wab_ref[...] = wa_ref[0].astype(BF16)
        wmb_ref[...] = wm_ref[0].astype(BF16)

    ya = jnp.dot(att_ref[...], wab_ref[...], preferred_element_type=F32)
    ym = jnp.dot(mem_ref[...], wmb_ref[...], preferred_element_type=F32)
    g_a = jax.nn.sigmoid(ga_ref[...].astype(F32) + ba_ref[0])
    g_m = jax.nn.sigmoid(gm_ref[...].astype(F32) + bm_ref[0])
    o_ref[...] = (g_a * ya + g_m * ym).astype(o_ref.dtype)


def _branch_mix(att, mem, w_a, w_m, layer, proj, gate_bias, tm=512, tn=512):
    m = att.shape[0]
    ka, km, d = w_a.shape[1], w_m.shape[1], w_a.shape[2]
    bias = gate_bias.reshape(-1, 1, d)
    gab = OFF_GA // tn
    gmb = (OFF_GA + d) // tn
    return pl.pallas_call(
        _branch_kernel,
        grid=(d // tn, m // tm),
        in_specs=[pl.BlockSpec((tm, ka), lambda j, i: (i, 0)),
                  pl.BlockSpec((tm, km), lambda j, i: (i, 0)),
                  pl.BlockSpec((1, ka, tn), lambda j, i: (layer, 0, j)),
                  pl.BlockSpec((1, km, tn), lambda j, i: (layer, 0, j)),
                  pl.BlockSpec((tm, tn), lambda j, i: (i, gab + j)),
                  pl.BlockSpec((tm, tn), lambda j, i: (i, gmb + j)),
                  pl.BlockSpec((1, 1, tn), lambda j, i: (2 * layer, 0, j)),
                  pl.BlockSpec((1, 1, tn), lambda j, i: (2 * layer + 1, 0, j))],
        out_specs=pl.BlockSpec((tm, tn), lambda j, i: (i, j)),
        out_shape=jax.ShapeDtypeStruct((m, d), BF16),
        scratch_shapes=[pltpu.VMEM((ka, tn), BF16), pltpu.VMEM((km, tn), BF16)],
        compiler_params=_cparams(("parallel", "arbitrary")),
        name="branch_mix",
    )(att, mem, w_a, w_m, proj, proj, bias, bias)


def _pack_bf16_pair(lo, hi):
    lo_bits = lax.bitcast_convert_type(lo.astype(BF16).astype(F32), jnp.uint32)
    hi_bits = lax.bitcast_convert_type(hi.astype(BF16).astype(F32), jnp.uint32)
    return (lo_bits >> 16) | (hi_bits & jnp.uint32(0xFFFF0000))


def _unpack_bf16_pair(word):
    lo = lax.bitcast_convert_type(word << 16, F32)
    hi = lax.bitcast_convert_type(word & jnp.uint32(0xFFFF0000), F32)
    return lo, hi


def _norm_router_kernel(x_ref, g_ref, whi_ref, wlo_ref, br_ref, h_ref, lg_ref):
    x = x_ref[...]
    h = x * lax.rsqrt(jnp.mean(x * x, axis=-1, keepdims=True) + EPS) * g_ref[...]
    half = h.shape[1] // 2
    h_ref[...] = _pack_bf16_pair(h[:, :half], h[:, half:])
    h_hi = h.astype(BF16)
    h_lo = (h - h_hi.astype(F32)).astype(BF16)
    w_hi = whi_ref[...]
    logits = jnp.dot(h_hi, w_hi, preferred_element_type=F32)
    logits = logits + jnp.dot(h_lo, w_hi, preferred_element_type=F32)
    logits = logits + jnp.dot(h_hi, wlo_ref[...], preferred_element_type=F32)
    lg_ref[...] = logits + br_ref[...]


def _norm_router(x, gain, w_r, b_r, tm=256):
    m, d = x.shape
    n = w_r.shape[1]
    w_hi = w_r.astype(BF16)
    w_lo = (w_r - w_hi.astype(F32)).astype(BF16)
    return pl.pallas_call(
        _norm_router_kernel,
        grid=(m // tm,),
        in_specs=[pl.BlockSpec((tm, d), lambda i: (i, 0)),
                  pl.BlockSpec((1, d), lambda i: (0, 0)),
                  pl.BlockSpec((d, n), lambda i: (0, 0)),
                  pl.BlockSpec((d, n), lambda i: (0, 0)),
                  pl.BlockSpec((1, n), lambda i: (0, 0))],
        out_specs=[pl.BlockSpec((tm, d // 2), lambda i: (i, 0)),
                   pl.BlockSpec((tm, n), lambda i: (i, 0))],
        out_shape=[jax.ShapeDtypeStruct((m, d // 2), jnp.uint32),
                   jax.ShapeDtypeStruct((m, n), F32)],
        compiler_params=_cparams(("parallel",)),
        name="norm_router",
    )(x, gain.reshape(1, d), w_hi, w_lo, b_r)


def _moe_kernel(blk_e_ref, cnt_ref, first_ref, tok_ref, h_hbm, wg_ref, wu_ref,
                wd_lo_ref, wd_hi_ref, y_ref, xg_ref, xb_ref, act_ref, sem, *, n_ff_tiles, n_steps,
                n_blk):
    b = pl.program_id(0)
    j = pl.program_id(1)
    rows = MOE_ROWS
    tf = MOE_FF_TILE
    grp = DMA_ISSUE_UNROLL
    cnt = cnt_ref[b]
    used = cnt > 0
    up_phase = j < n_ff_tiles
    n_sub = (cnt + MOE_SUB_ROWS - 1) // MOE_SUB_ROWS

    def row_copy(i, tok):
        return pltpu.make_async_copy(h_hbm.at[pl.ds(tok, 1)], xg_ref.at[pl.ds(i, 1)], sem)

    def issue_rows(blk):
        first = first_ref[blk]
        last_entry = tok_ref.shape[0] - 1

        def body(gi, carry):
            for r in range(grp):
                i = gi * grp + r
                row_copy(i, tok_ref[jnp.minimum(first + i, last_entry)]).start()
            return carry
        lax.fori_loop(0, (cnt_ref[blk] + grp - 1) // grp, body, 0)

    def wait_rows(count):
        def body(gi, carry):
            for r in range(grp):
                row_copy(gi * grp + r, 0).wait()
            return carry
        lax.fori_loop(0, (count + grp - 1) // grp, body, 0)

    def land_rows(blk):
        wait_rows(cnt_ref[blk])
        half = xg_ref.shape[1]
        lo, hi = _unpack_bf16_pair(xg_ref[...])
        xb_ref[blk % 2, :, 0:half] = lo.astype(BF16)
        xb_ref[blk % 2, :, half:2 * half] = hi.astype(BF16)

    @pl.when(jnp.logical_and(b == 0, j == 0))
    def _():
        xg_ref[...] = jnp.zeros_like(xg_ref)
        issue_rows(0)
        land_rows(0)
        if n_blk > 1:
            issue_rows(1)

    @pl.when(jnp.logical_and(used, j == n_steps - 1))
    def _():
        @pl.when(b + 1 < n_blk)
        def _():
            @pl.when(cnt_ref[jnp.minimum(b + 1, n_blk - 1)] > 0)
            def _():
                land_rows(b + 1)

            @pl.when(b + 2 < n_blk)
            def _():
                issue_rows(jnp.minimum(b + 2, n_blk - 1))

    for m in range(MOE_SUB_ROWS, rows + 1, MOE_SUB_ROWS):
        live = jnp.logical_and(used, n_sub == m // MOE_SUB_ROWS)

        @pl.when(jnp.logical_and(live, up_phase))
        def _(m=m):
            x = xb_ref[b % 2, 0:m, :]
            g = jnp.dot(x, wg_ref[0, 0].astype(BF16), preferred_element_type=F32)
            u = jnp.dot(x, wu_ref[0, 0].astype(BF16), preferred_element_type=F32)
            act_ref[j, 0:m, :] = (g * jax.nn.sigmoid(g) * u).astype(BF16)

        @pl.when(jnp.logical_and(live, jnp.logical_not(up_phase)))
        def _(m=m):
            def down(wd_ref):
                acc = jnp.dot(act_ref[0, 0:m, :], wd_ref[0, 0, 0:tf, :].astype(BF16),
                              preferred_element_type=F32)
                for t in range(1, n_ff_tiles):
                    acc = acc + jnp.dot(act_ref[t, 0:m, :],
                                        wd_ref[0, 0, t * tf:(t + 1) * tf, :].astype(BF16),
                                        preferred_element_type=F32)
                return acc

            y_ref[0:m, :] = _pack_bf16_pair(down(wd_lo_ref), down(wd_hi_ref))
            if m < rows:
                y_ref[m:rows, :] = jnp.zeros((rows - m, y_ref.shape[1]), y_ref.dtype)

    @pl.when(jnp.logical_and(jnp.logical_not(used), jnp.logical_not(up_phase)))
    def _():
        y_ref[...] = jnp.zeros_like(y_ref)


def _moe_experts(h2p, blk_expert, blk_cnt, blk_first, tok_sorted, w_gate, w_up, w_down, layer):
    d = 2 * h2p.shape[1]
    d_ff = w_gate.shape[3]
    rows = MOE_ROWS
    n_blk = blk_expert.shape[0]
    tf = MOE_FF_TILE
    tn = MOE_OUT_TILE
    n_f = d_ff // tf
    n_o = d // 2 // tn

    def ff_tile(b, j, cnt):
        return jnp.where(cnt[b] > 0, jnp.minimum(j, n_f - 1), n_f - 1)

    def out_tile(b, j):
        return jnp.clip(j - n_f, 0, n_o - 1)

    def w_down_block(b, j, be, cnt, hi):
        stay = jnp.logical_or(j < n_f, cnt[b] == 0)
        expert = jnp.where(j < n_f, be[jnp.maximum(b - 1, 0)], be[b])
        return layer, expert, 0, hi * n_o + jnp.where(stay, n_o - 1, j - n_f)

    grid_spec = pltpu.PrefetchScalarGridSpec(
        num_scalar_prefetch=4,
        grid=(n_blk, n_f + n_o),
        in_specs=[pl.BlockSpec(memory_space=pl.ANY),
                  pl.BlockSpec((1, 1, d, tf),
                               lambda b, j, be, cnt, *_: (layer, be[b], 0, ff_tile(b, j, cnt))),
                  pl.BlockSpec((1, 1, d, tf),
                               lambda b, j, be, cnt, *_: (layer, be[b], 0, ff_tile(b, j, cnt))),
                  pl.BlockSpec((1, 1, d_ff, tn),
                               lambda b, j, be, cnt, *_: w_down_block(b, j, be, cnt, 0)),
                  pl.BlockSpec((1, 1, d_ff, tn),
                               lambda b, j, be, cnt, *_: w_down_block(b, j, be, cnt, 1))],
        out_specs=pl.BlockSpec((rows, tn), lambda b, j, *_: (b, out_tile(b, j))),
        scratch_shapes=[pltpu.VMEM((rows, d // 2), jnp.uint32),
                        pltpu.VMEM((2, rows, d), BF16),
                        pltpu.VMEM((n_f, rows, tf), BF16),
                        pltpu.SemaphoreType.DMA(())],
    )
    return pl.pallas_call(
        functools.partial(_moe_kernel, n_ff_tiles=n_f, n_steps=n_f + n_o, n_blk=n_blk),
        grid_spec=grid_spec,
        out_shape=jax.ShapeDtypeStruct((n_blk * rows, d // 2), jnp.uint32),
        compiler_params=_cparams(("arbitrary", "arbitrary")),
        name="moe_experts",
    )(blk_expert, blk_cnt, blk_first, tok_sorted, h2p, w_gate, w_up, w_down, w_down)


def _combine_kernel(pos_ref, pos_next_ref, x_ref, w_ref, y_hbm, o_ref, buf_ref, sem, *, n_steps):
    rows = COMBINE_ROWS
    n_copy = TOP_K * rows
    step = pl.program_id(0)
    slot = step % 2

    def row_copy(s, i, src):
        return pltpu.make_async_copy(y_hbm.at[pl.ds(src, 1)], buf_ref.at[s, pl.ds(i, 1)], sem.at[s])

    def issue(s, idx_ref):
        def body(i, carry):
            row_copy(s, i, idx_ref[0, 0, i]).start()
            return carry
        lax.fori_loop(0, n_copy, body, 0, unroll=DMA_ISSUE_UNROLL)

    @pl.when(step == 0)
    def _():
        issue(0, pos_ref)

    def drain(i, carry):
        row_copy(slot, i, 0).wait()
        return carry

    lax.fori_loop(0, n_copy, drain, 0, unroll=DMA_ISSUE_UNROLL)

    half = x_ref.shape[1] // 2
    chunk = COMBINE_CHUNK
    n_chunks = rows // chunk
    per_chunk = n_copy // n_chunks

    def chunk_body(prefetch):
        def body(c, carry):
            if prefetch:
                for r in range(per_chunk):
                    i = c * per_chunk + r
                    row_copy(1 - slot, i, pos_next_ref[0, 0, i]).start()
            r0 = pl.multiple_of(c * chunk, chunk)
            lo0, hi0 = _unpack_bf16_pair(buf_ref[slot, pl.ds(r0, chunk), :])
            lo1, hi1 = _unpack_bf16_pair(buf_ref[slot, pl.ds(rows + r0, chunk), :])
            w = w_ref[pl.ds(r0, chunk), :]
            w0 = w[:, 0:1]
            w1 = w[:, 1:2]
            o_ref[pl.ds(r0, chunk), 0:half] = x_ref[pl.ds(r0, chunk), 0:half] + (w0 * lo0 + w1 * lo1)
            o_ref[pl.ds(r0, chunk), half:2 * half] = (
                x_ref[pl.ds(r0, chunk), half:2 * half] + (w0 * hi0 + w1 * hi1))
            return carry
        return body

    @pl.when(step + 1 < n_steps)
    def _():
        lax.fori_loop(0, n_chunks, chunk_body(True), 0)

    @pl.when(step + 1 >= n_steps)
    def _():
        lax.fori_loop(0, n_chunks, chunk_body(False), 0)


def _moe_combine(x1, ys, pos, weights):
    n_tok, d = x1.shape
    rows = COMBINE_ROWS
    n_blk = n_tok // rows
    pos_b = pos.reshape(n_blk, rows, TOP_K).transpose(0, 2, 1).reshape(n_blk, 1, TOP_K * rows)
    idx_block = (1, 1, TOP_K * rows)
    return pl.pallas_call(
        functools.partial(_combine_kernel, n_steps=n_blk),
        grid=(n_blk,),
        in_specs=[pl.BlockSpec(idx_block, lambda i: (i, 0, 0), memory_space=pltpu.SMEM),
                  pl.BlockSpec(idx_block, lambda i: (jnp.minimum(i + 1, n_blk - 1), 0, 0),
                               memory_space=pltpu.SMEM),
                  pl.BlockSpec((rows, d), lambda i: (i, 0)),
                  pl.BlockSpec((rows, TOP_K), lambda i: (i, 0)),
                  pl.BlockSpec(memory_space=pl.ANY)],
        out_specs=pl.BlockSpec((rows, d), lambda i: (i, 0)),
        out_shape=jax.ShapeDtypeStruct((n_tok, d), F32),
        scratch_shapes=[pltpu.VMEM((2, TOP_K * rows, d // 2), jnp.uint32),
                        pltpu.SemaphoreType.DMA((2,))],
        compiler_params=_cparams(("arbitrary",)),
        name="moe_combine",
    )(pos_b, pos_b, x1, weights, ys)


def _route(logits, n_tok):
    glog = logits[:, :N_GROUPS]
    gprob = jax.nn.softmax(glog, axis=-1)
    g_sel = jnp.argmax(glog, axis=-1)
    p_g = jnp.max(gprob, axis=-1)
    elog = logits[:, N_GROUPS:N_GROUPS + N_EXPERTS]
    col_group = jnp.arange(N_EXPERTS, dtype=jnp.int32) // EXPERTS_PER_GROUP
    elog = jnp.where(col_group[None, :] == g_sel[:, None], elog, -jnp.inf)
    top_v, expert_ids = lax.top_k(elog, TOP_K)
    weights = p_g[:, None] * jax.nn.softmax(top_v, axis=-1)
    expert_ids = expert_ids.astype(jnp.int32)

    n_assign = n_tok * TOP_K
    rows = MOE_ROWS
    flat_e = expert_ids.reshape(n_assign)
    a_ids = jnp.arange(n_assign, dtype=jnp.int32)
    skey = jnp.sort(flat_e * n_assign + a_ids)
    order = skey % n_assign
    inv = jnp.argsort(order).astype(jnp.int32)
    experts = jnp.arange(N_EXPERTS, dtype=jnp.int32)
    counts = jnp.sum((flat_e[:, None] == experts[None, :]).astype(jnp.int32), axis=0)
    blocks = (counts + rows - 1) // rows
    pad_end = jnp.cumsum(blocks * rows)
    pad_start = pad_end - blocks * rows
    start = jnp.cumsum(counts) - counts
    n_slots = n_assign + N_EXPERTS * rows
    n_blk = n_slots // rows
    n_used = jnp.sum(blocks).astype(jnp.int32)
    blk_ids = jnp.arange(n_blk, dtype=jnp.int32)
    blk_expert = jnp.minimum(
        jnp.sum((pad_end[None, :] <= (blk_ids * rows)[:, None]).astype(jnp.int32), axis=1),
        N_EXPERTS - 1)
    blk_cnt = jnp.clip(counts[blk_expert] - (blk_ids * rows - pad_start[blk_expert]), 0, rows)
    blk_cnt = jnp.where(blk_ids < n_used, blk_cnt, 0).astype(jnp.int32)
    blk_first = (start[blk_expert] + blk_ids * rows - pad_start[blk_expert]).astype(jnp.int32)
    tok_sorted = (order // TOP_K).astype(jnp.int32)
    blk_expert = jnp.where(blk_ids < n_used, blk_expert,
                           blk_expert[jnp.maximum(n_used - 1, 0)]).astype(jnp.int32)
    shift = jnp.sum(jnp.where(flat_e[:, None] == experts[None, :], (pad_start - start)[None, :], 0),
                    axis=1)
    pos = (inv + shift).astype(jnp.int32).reshape(n_tok, TOP_K)
    return blk_expert, blk_cnt, blk_first, tok_sorted, pos, weights


def kernel(x, norm1_gain, w_in, attn_q_norm_gain, attn_k_norm_gain, mlstm_gate_bias,
           mlstm_head_norm_gain, branch_gate_bias, w_attn_branch, w_mlstm_branch, w_out,
           norm2_gain, w_router_group, b_router_group, w_router_expert, b_router_expert,
           w_expert_gate, w_expert_up, w_expert_down):
    bsz, s_len, d = x.shape
    assert bsz == 1
    depth = w_in.shape[0]
    xs = x.reshape(s_len, d)
    w_in_t = jnp.swapaxes(w_in, 1, 2)
    n_proj = OFF_GA + 2 * d
    tn = 512

    def proj_rows(j):
        return j * tn + jnp.where(j * tn >= OFF_GA, N_ML_GATES, 0)

    for l in range(depth):
        h = _rmsnorm(xs, norm1_gain[l])
        proj = _matmul_nt_paired(h, w_in_t, l, proj_rows, n_proj, BF16, 1024, tn, "in_proj")
        gates = _matmul_nt(h, w_in_t, l, lambda j: OFF_MG + j * LANES, LANES, F32, 1024, LANES,
                           "gate_proj")[:, :N_ML_GATES]

        att = _attention(proj, attn_q_norm_gain[l], attn_k_norm_gain[l])

        hs = _mlstm(proj, gates, mlstm_gate_bias[l])
        mem = _mlstm_out(hs, proj, mlstm_head_norm_gain[l])

        merged = _branch_mix(att, mem, w_attn_branch, w_mlstm_branch, l, proj, branch_gate_bias)
        x1 = _matmul_residual(merged, w_out, l, xs, 1024, 512)

        w_r = jnp.concatenate([w_router_group[l], w_router_expert[l]], axis=1)
        b_r = jnp.concatenate([b_router_group[l], b_router_expert[l]])
        n_r = w_r.shape[1]
        w_r = jnp.pad(w_r, ((0, 0), (0, LANES - n_r)))
        b_r = jnp.pad(b_r, (0, LANES - n_r)).reshape(1, LANES)
        h2, logits = _norm_router(x1, norm2_gain[l], w_r, b_r)

        blk_expert, blk_cnt, blk_first, tok_sorted, pos, weights = _route(logits, s_len)
        ys = _moe_experts(h2, blk_expert, blk_cnt, blk_first, tok_sorted,
                          w_expert_gate, w_expert_up, w_expert_down, l)
        xs = _moe_combine(x1, ys, pos, weights)
    return xs.reshape(bsz, s_len, d)
```

```python
import functools

import jax
import jax.numpy as jnp
from jax import lax
from jax.experimental import pallas as pl
from jax.experimental.pallas import tpu as pltpu

F32 = jnp.float32
BF16 = jnp.bfloat16

EPS = 1e-6
NEG = -1e30
LANES = 128

ATT_PATTERNS = ((128, 1), (512, 4), (2048, 16))
N_ATT_GROUPS = 3
ATT_HEADS = 8
ATT_HEAD_DIM = 128
ATT_GROUP_W = ATT_HEADS * ATT_HEAD_DIM
ATT_W = N_ATT_GROUPS * ATT_GROUP_W
ROPE_THETA = 10000.0
ATT_Q_TILE = 128
ATT_TILE_UNROLL = 16

ML_HEADS = 8
ML_QK_DIM = 256
ML_V_DIM = 512
ML_QK_W = ML_HEADS * ML_QK_DIM
ML_V_W = ML_HEADS * ML_V_DIM
N_ML_GATES = 4 * ML_HEADS
GATE_SOFTCAP = 15.0
ML_TILE = 256
ML_HEAD_GROUP = 4
ML_V_WINDOWS = 2
ML_OUT_HEADS = 2

N_GROUPS = 8
EXPERTS_PER_GROUP = 8
N_EXPERTS = N_GROUPS * EXPERTS_PER_GROUP
TOP_K = 2
MOE_ROWS = 512
MOE_SUB_ROWS = 128
MOE_FF_TILE = 256
MOE_OUT_TILE = 1024
COMBINE_ROWS = 256
COMBINE_CHUNK = 8
DMA_ISSUE_UNROLL = 8

OFF_AQ = 0
OFF_AK = OFF_AQ + ATT_W
OFF_AV = OFF_AK + ATT_W
OFF_MQ = OFF_AV + ATT_W
OFF_MK = OFF_MQ + ML_QK_W
OFF_MV = OFF_MK + ML_QK_W
OFF_MO = OFF_MV + ML_V_W
OFF_MG = OFF_MO + ML_V_W
OFF_GA = OFF_MG

VMEM_LIMIT = 56 * 1024 * 1024


def _cparams(sem, vmem=VMEM_LIMIT):
    return pltpu.CompilerParams(dimension_semantics=sem, vmem_limit_bytes=vmem)


def _rmsnorm_kernel(x_ref, g_ref, o_ref):
    x = x_ref[...]
    ms = jnp.mean(x * x, axis=-1, keepdims=True)
    o_ref[...] = (x * lax.rsqrt(ms + EPS) * g_ref[...]).astype(o_ref.dtype)


def _rmsnorm(x, gain, tm=256):
    m, d = x.shape
    return pl.pallas_call(
        _rmsnorm_kernel,
        grid=(m // tm,),
        in_specs=[pl.BlockSpec((tm, d), lambda i: (i, 0)),
                  pl.BlockSpec((1, d), lambda i: (0, 0))],
        out_specs=pl.BlockSpec((tm, d), lambda i: (i, 0)),
        out_shape=jax.ShapeDtypeStruct((m, d), BF16),
        compiler_params=_cparams(("parallel",)),
        name="rmsnorm",
    )(x, gain.reshape(1, d))


def _mm_nt_kernel(a_ref, wt_ref, o_ref, wb_ref):
    @pl.when(pl.program_id(1) == 0)
    def _():
        wb_ref[...] = wt_ref[0].astype(BF16)

    o_ref[...] = lax.dot_general(a_ref[...], wb_ref[...], (((1,), (1,)), ((), ())),
                                 preferred_element_type=F32).astype(o_ref.dtype)


def _matmul_nt(a, w_t, layer, row_of_tile, n, out_dtype, tm, tn, name):
    m, k = a.shape
    return pl.pallas_call(
        _mm_nt_kernel,
        grid=(n // tn, m // tm),
        in_specs=[pl.BlockSpec((tm, k), lambda j, i: (i, 0)),
                  pl.BlockSpec((pl.Element(1), pl.Element(tn), pl.Element(k)),
                               lambda j, i: (layer, pl.multiple_of(row_of_tile(j), 8), 0))],
        out_specs=pl.BlockSpec((tm, tn), lambda j, i: (i, j)),
        out_shape=jax.ShapeDtypeStruct((m, n), out_dtype),
        scratch_shapes=[pltpu.VMEM((tn, k), BF16)],
        compiler_params=_cparams(("parallel", "arbitrary")),
        name=name,
    )(a, w_t)


def _mm_nt_pair_kernel(a_ref, wt_ref, o_ref, wb_ref):
    c = pl.program_id(2)

    @pl.when(pl.program_id(1) == 0)
    def _():
        wb_ref[c] = wt_ref[0].astype(BF16)

    o_ref[...] = lax.dot_general(a_ref[...], wb_ref[c], (((1,), (1,)), ((), ())),
                                 preferred_element_type=F32).astype(o_ref.dtype)


def _matmul_nt_paired(a, w_t, layer, row_of_tile, n, out_dtype, tm, tn, name):
    m, k = a.shape
    assert n % (2 * tn) == 0

    def w_rows(jj, i, c):
        return pl.multiple_of(row_of_tile(2 * jj + jnp.where(i == 0, c, 1)), 8)

    return pl.pallas_call(
        _mm_nt_pair_kernel,
        grid=(n // (2 * tn), m // tm, 2),
        in_specs=[pl.BlockSpec((tm, k), lambda jj, i, c: (i, 0)),
                  pl.BlockSpec((pl.Element(1), pl.Element(tn), pl.Element(k)),
                               lambda jj, i, c: (layer, w_rows(jj, i, c), 0))],
        out_specs=pl.BlockSpec((tm, tn), lambda jj, i, c: (i, 2 * jj + c)),
        out_shape=jax.ShapeDtypeStruct((m, n), out_dtype),
        scratch_shapes=[pltpu.VMEM((2, tn, k), BF16)],
        compiler_params=_cparams(("arbitrary", "arbitrary", "arbitrary")),
        name=name,
    )(a, w_t)


def _mm_res_kernel(a_ref, w_ref, r_ref, o_ref, wb_ref):
    @pl.when(pl.program_id(1) == 0)
    def _():
        wb_ref[...] = w_ref[0].astype(BF16)

    o_ref[...] = r_ref[...] + jnp.dot(a_ref[...], wb_ref[...], preferred_element_type=F32)


def _matmul_residual(a, w, layer, res, tm, tn):
    m, k = a.shape
    n = w.shape[2]
    return pl.pallas_call(
        _mm_res_kernel,
        grid=(n // tn, m // tm),
        in_specs=[pl.BlockSpec((tm, k), lambda j, i: (i, 0)),
                  pl.BlockSpec((1, k, tn), lambda j, i: (layer, 0, j)),
                  pl.BlockSpec((tm, tn), lambda j, i: (i, j))],
        out_specs=pl.BlockSpec((tm, tn), lambda j, i: (i, j)),
        out_shape=jax.ShapeDtypeStruct((m, n), F32),
        scratch_shapes=[pltpu.VMEM((k, tn), BF16)],
        compiler_params=_cparams(("parallel", "arbitrary")),
        name="out_proj",
    )(a, w, res)


def _attn_kernel(q_ref, k_ref, v_ref, tab_ref, qg_ref, kg_ref, o_ref,
                 qn_ref, kn_ref, vf_ref, acc_ref, lse_ref, band_ref, *, s_len):
    grp = pl.program_id(1)
    tq = ATT_Q_TILE
    hd = ATT_HEAD_DIM
    half = hd // 2
    low = lax.broadcasted_iota(jnp.int32, (tq, hd), 1) < half
    row = lax.broadcasted_iota(jnp.int32, (hd, hd), 0)
    col = lax.broadcasted_iota(jnp.int32, (hd, hd), 1)
    mean_w = jnp.full((hd, hd), 1.0 / hd, BF16)
    rot = jnp.where(row == col + half, -1.0, jnp.where(col == row + half, 1.0, 0.0)).astype(BF16)

    def prep(i, carry):
        rows = pl.ds(pl.multiple_of(i * tq, tq), tq)
        tab = tab_ref[rows, :]
        swapped = pltpu.roll(tab, half, axis=1)
        c = jnp.where(low, tab, swapped)
        s = jnp.where(low, swapped, tab)
        for src, g_ref, dst in ((q_ref, qg_ref, qn_ref), (k_ref, kg_ref, kn_ref)):
            xf = src[rows, :].astype(F32)
            ms = jnp.dot((xf * xf).astype(BF16), mean_w, preferred_element_type=F32)
            y = xf * lax.rsqrt(ms + EPS) * g_ref[0]
            y_rot = jnp.dot(y.astype(BF16), rot, preferred_element_type=F32)
            dst[rows, :] = y * c + y_rot * s
        vf_ref[rows, :] = v_ref[rows, :].astype(F32)
        return carry

    lax.fori_loop(0, s_len // tq, prep, 0, unroll=4)

    def run_group(window, dilation, first):
        radius = window // (2 * dilation)
        sub_len = s_len // dilation
        win = tq + 2 * radius
        tiles_per_class = sub_len // tq
        rel = (lax.broadcasted_iota(jnp.int32, (tq, win), 1)
               - lax.broadcasted_iota(jnp.int32, (tq, win), 0))
        for case in range(3):
            band_ref[case] = jnp.where(jnp.abs(rel - case * radius) <= radius, 0.0, NEG)

        def rows_of(start, size):
            if dilation == 1:
                return pl.ds(start, size)
            return pl.ds(start, size, stride=dilation)

        def tile(idx, carry):
            r = idx // tiles_per_class
            q0 = (idx % tiles_per_class) * tq
            k0 = jnp.clip(q0 - radius, 0, sub_len - win)
            q_rows = rows_of(r + dilation * q0, tq)
            k_rows = rows_of(r + dilation * k0, win)
            q = qn_ref[q_rows, :].astype(BF16)
            k = kn_ref[k_rows, :].astype(BF16)
            v = vf_ref[k_rows, :].astype(BF16)
            s = lax.dot_general(q, k, (((1,), (1,)), ((), ())), preferred_element_type=F32)
            s = s + band_ref[(q0 - k0) // radius]
            m = jnp.max(s, axis=-1, keepdims=True)
            p = jnp.exp(s - m)
            den = jnp.sum(p, axis=-1, keepdims=True)
            o = jnp.dot(p.astype(BF16), v, preferred_element_type=F32) / den
            lse = jnp.broadcast_to(m + jnp.log(den), (tq, ATT_HEAD_DIM))
            if first:
                acc_ref[q_rows, :] = o
                lse_ref[q_rows, :] = lse
            else:
                old_l = lse_ref[q_rows, :]
                top = jnp.maximum(old_l, lse)
                a = jnp.exp(old_l - top)
                b = jnp.exp(lse - top)
                tot = a + b
                acc_ref[q_rows, :] = (a * acc_ref[q_rows, :] + b * o) / tot
                lse_ref[q_rows, :] = top + jnp.log(tot)
            return carry

        lax.fori_loop(0, s_len // tq, tile, 0, unroll=ATT_TILE_UNROLL)

    for gi, (window, dilation) in enumerate(ATT_PATTERNS):
        @pl.when(grp == gi)
        def _(window=window, dilation=dilation, gi=gi):
            run_group(window, dilation, gi == 0)

    @pl.when(grp == N_ATT_GROUPS - 1)
    def _():
        o_ref[...] = acc_ref[...].astype(o_ref.dtype)


def _attention(proj, q_gain, k_gain):
    s_len = proj.shape[0]
    half = ATT_HEAD_DIM // 2
    inv_freq = ROPE_THETA ** (-jnp.arange(half, dtype=F32) / half)
    ang = jnp.arange(s_len).astype(F32)[:, None] * inv_freq[None, :]
    table = jnp.concatenate([jnp.cos(ang), jnp.sin(ang)], axis=1)
    qb, kb, vb = OFF_AQ // LANES, OFF_AK // LANES, OFF_AV // LANES
    blk = (s_len, ATT_HEAD_DIM)
    radii = {window // (2 * dilation) for window, dilation in ATT_PATTERNS}
    assert len(radii) == 1
    win = ATT_Q_TILE + 2 * radii.pop()
    gain_spec = pl.BlockSpec((1, 1, ATT_HEAD_DIM), lambda h, g: (g, 0, 0))
    q_scaled = q_gain.reshape(N_ATT_GROUPS, 1, ATT_HEAD_DIM) * (ATT_HEAD_DIM ** -0.5)
    return pl.pallas_call(
        functools.partial(_attn_kernel, s_len=s_len),
        grid=(ATT_HEADS, N_ATT_GROUPS),
        in_specs=[pl.BlockSpec(blk, lambda h, g: (0, qb + g * ATT_HEADS + h)),
                  pl.BlockSpec(blk, lambda h, g: (0, kb + g * ATT_HEADS + h)),
                  pl.BlockSpec(blk, lambda h, g: (0, vb + g * ATT_HEADS + h)),
                  pl.BlockSpec(blk, lambda h, g: (0, 0)),
                  gain_spec, gain_spec],
        out_specs=pl.BlockSpec(blk, lambda h, g: (0, h)),
        out_shape=jax.ShapeDtypeStruct((s_len, ATT_GROUP_W), BF16),
        scratch_shapes=[pltpu.VMEM(blk, F32)] * 5 + [pltpu.VMEM((3, ATT_Q_TILE, win), F32)],
        compiler_params=_cparams(("parallel", "arbitrary")),
        name="dilated_attn",
    )(proj, proj, proj, table, q_scaled, k_gain.reshape(N_ATT_GROUPS, 1, ATT_HEAD_DIM))


def _tile_scan(x, pos, lc, combine, reverse, fill):
    n = x.shape[1]
    k = 1
    while k < lc:
        if reverse:
            shifted = pltpu.roll(x, n - k, axis=1)
            ok = pos < lc - k
        else:
            shifted = pltpu.roll(x, k, axis=1)
            ok = pos >= k
        x = combine(x, jnp.where(ok, shifted, fill))
        k *= 2
    return x


def _mlstm_gate_kernel(ig_ref, f_ref, bias_ref, o_ref, *, lc):
    n_rows, s_len = ig_ref.shape
    n_tiles = s_len // lc
    shape = (n_rows, s_len)
    bw = lax.broadcasted_iota(jnp.int32, shape, 0) >= ML_HEADS
    lane = lax.broadcasted_iota(jnp.int32, shape, 1)
    pos = lane % lc
    tile = lane // lc

    ig = GATE_SOFTCAP * jnp.tanh((ig_ref[...] + bias_ref[:, 0:1]) / GATE_SOFTCAP)
    f = GATE_SOFTCAP * jnp.tanh((f_ref[...] + bias_ref[:, 1:2]) / GATE_SOFTCAP)
    lf = jnp.minimum(f, 0.0) - jnp.log1p(jnp.exp(-jnp.abs(f)))

    pre = _tile_scan(lf, pos, lc, jnp.add, False, 0.0)
    suf = _tile_scan(lf, pos, lc, jnp.add, True, 0.0)
    b = jnp.where(bw, suf, pre)
    tot = pre + suf - lf
    a = ig - b
    pmax = _tile_scan(a, pos, lc, jnp.maximum, False, -jnp.inf)
    smax = _tile_scan(a, pos, lc, jnp.maximum, True, -jnp.inf)
    cm = jnp.where(bw, smax, pmax)
    cmax = jnp.maximum(pmax, smax)

    def tiles_scan(reverse):
        alpha, beta = tot, cmax + tot
        step = 1
        while step < n_tiles:
            sh = step * lc
            if reverse:
                pa, pb = pltpu.roll(alpha, s_len - sh, axis=1), pltpu.roll(beta, s_len - sh, axis=1)
                ok = tile < n_tiles - step
            else:
                pa, pb = pltpu.roll(alpha, sh, axis=1), pltpu.roll(beta, sh, axis=1)
                ok = tile >= step
            alpha, beta = (jnp.where(ok, pa + alpha, alpha),
                           jnp.where(ok, jnp.maximum(pb + alpha, beta), beta))
            step *= 2
        return jnp.maximum(alpha, beta)

    last_f = tiles_scan(False)
    last_r = tiles_scan(True)
    m_last = jnp.where(bw, last_r, last_f)
    prev_f = jnp.where(tile >= 1, pltpu.roll(last_f, lc, axis=1), 0.0)
    prev_r = jnp.where(tile < n_tiles - 1, pltpu.roll(last_r, s_len - lc, axis=1), 0.0)
    m_prev = jnp.where(bw, prev_r, prev_f)
    mm = jnp.maximum(m_prev, cm)

    o_ref[0] = a
    o_ref[1] = mm
    o_ref[2] = b + mm
    o_ref[3] = m_prev
    o_ref[4] = tot - m_last
    o_ref[5] = tot + m_prev - m_last
    o_ref[6] = jnp.zeros(shape, F32)
    o_ref[7] = jnp.zeros(shape, F32)


def _mlstm_scan_kernel(q_ref, k_ref, *rest, n_v):
    v_refs = rest[:n_v]
    g_ref, mask_ref, o_ref, c_ref = rest[n_v:]
    dk, dv = ML_QK_DIM, ML_V_DIM

    @pl.when(pl.program_id(1) == 0)
    def _():
        c_ref[...] = jnp.zeros_like(c_ref)

    lc = q_ref.shape[0]
    mask = mask_ref[0]
    ones = jnp.ones((lc, LANES), BF16)
    for hd in range(ML_HEAD_GROUP):
        rows = g_ref[hd]

        def column(r, rows=rows):
            return jnp.broadcast_to(rows[r:r + 1, :], (8, lc)).T[:, 0:1]

        a_row = rows[0:1, :]
        a_col, mm_col, m_col = column(0), column(1), column(2)
        m_prev = rows[3:4, 0:1]
        d_mat = jnp.exp(a_row - mm_col + mask)
        decay_q = jnp.exp(m_prev - mm_col)

        q = q_ref[:, hd * dk:(hd + 1) * dk]
        k = k_ref[:, hd * dk:(hd + 1) * dk] * (dk ** -0.5)
        per_window = ML_HEAD_GROUP // n_v
        v_cols = slice((hd % per_window) * dv, (hd % per_window + 1) * dv)
        v_ext = jnp.concatenate([v_refs[hd // per_window][:, v_cols], ones], axis=1)
        s = lax.dot_general(q, k, (((1,), (1,)), ((), ())), preferred_element_type=F32) * d_mat
        c_old = c_ref[hd]
        acc = jnp.dot(s.astype(BF16), v_ext, preferred_element_type=F32)
        acc = acc + decay_q * jnp.dot(q, c_old.astype(BF16), preferred_element_type=F32)
        den = acc[:, dv:dv + LANES]
        inv = 1.0 / jnp.maximum(jnp.abs(den), jnp.exp(-m_col))
        for part in range(dv // LANES):
            cols = slice(part * LANES, (part + 1) * LANES)
            o_ref[0, :, hd * dv + part * LANES:hd * dv + (part + 1) * LANES] = (
                acc[:, cols] * inv).astype(o_ref.dtype)

        w_col = jnp.exp(a_col + rows[4:5, 0:1])
        decay_c = jnp.exp(rows[5:6, 0:1])
        kw = k.astype(F32) * w_col
        c_ref[hd] = decay_c * c_old + lax.dot_general(
            kw.astype(BF16), v_ext, (((0,), (0,)), ((), ())), preferred_element_type=F32)


def _mlstm(proj, gates, gate_bias):
    s_len = proj.shape[0]
    lc = ML_TILE
    nt = s_len // lc
    n_rows = 2 * ML_HEADS
    gt = gates.T.reshape(2, 2, ML_HEADS, s_len)
    bias = gate_bias.reshape(2, 2, ML_HEADS)
    gate_rows = pl.pallas_call(
        functools.partial(_mlstm_gate_kernel, lc=lc),
        out_shape=jax.ShapeDtypeStruct((8, n_rows, s_len), F32),
        compiler_params=_cparams(None),
        name="mlstm_gates",
    )(gt[:, 0].reshape(n_rows, s_len), gt[:, 1].reshape(n_rows, s_len),
      jnp.stack([bias[:, 0].reshape(n_rows), bias[:, 1].reshape(n_rows)], axis=1))
    gate_rows = gate_rows.transpose(1, 0, 2)

    t_idx = jnp.arange(lc)[:, None]
    s_idx = jnp.arange(lc)[None, :]
    mask = jnp.stack([jnp.where(s_idx <= t_idx, 0.0, NEG), jnp.where(s_idx >= t_idx, 0.0, NEG)]).astype(F32)

    hg = ML_HEAD_GROUP
    groups = ML_HEADS // hg

    def tile_of(dg, t):
        return jnp.where(dg >= groups, nt - 1 - t, t)

    parts = ML_V_WINDOWS
    v_width = hg // parts * ML_V_DIM
    assert OFF_MQ % (hg * ML_QK_DIM) == 0 and OFF_MK % (hg * ML_QK_DIM) == 0
    assert OFF_MV % v_width == 0 and hg % parts == 0
    qb, kb = OFF_MQ // (hg * ML_QK_DIM), OFF_MK // (hg * ML_QK_DIM)
    vb = OFF_MV // v_width

    def v_spec(part):
        return pl.BlockSpec((lc, v_width),
                            lambda dg, t: (tile_of(dg, t), vb + parts * (dg % groups) + part))

    return pl.pallas_call(
        functools.partial(_mlstm_scan_kernel, n_v=parts),
        grid=(2 * groups, nt),
        in_specs=[pl.BlockSpec((lc, hg * ML_QK_DIM), lambda dg, t: (tile_of(dg, t), qb + dg % groups)),
                  pl.BlockSpec((lc, hg * ML_QK_DIM), lambda dg, t: (tile_of(dg, t), kb + dg % groups))]
                 + [v_spec(part) for part in range(parts)]
                 + [pl.BlockSpec((hg, 8, lc), lambda dg, t: (dg, 0, tile_of(dg, t))),
                    pl.BlockSpec((1, lc, lc), lambda dg, t: (dg // groups, 0, 0))],
        out_specs=pl.BlockSpec((1, lc, hg * ML_V_DIM),
                               lambda dg, t: (dg // groups, tile_of(dg, t), dg % groups)),
        out_shape=jax.ShapeDtypeStruct((2, s_len, ML_V_W), BF16),
        scratch_shapes=[pltpu.VMEM((hg, ML_QK_DIM, ML_V_DIM + LANES), F32)],
        compiler_params=_cparams(("parallel", "arbitrary")),
        name="mlstm_scan",
    )(proj, proj, *([proj] * parts), gate_rows, mask)


def _mlstm_out_kernel(h_ref, mo_ref, g_ref, o_ref):
    for hd in range(ML_OUT_HEADS):
        cols = slice(hd * ML_V_DIM, (hd + 1) * ML_V_DIM)
        h = h_ref[0, :, cols].astype(F32) + h_ref[1, :, cols].astype(F32)
        y = h * lax.rsqrt(jnp.mean(h * h, axis=-1, keepdims=True) + EPS) * g_ref[0, :, cols]
        o_ref[:, cols] = (y * jax.nn.sigmoid(mo_ref[:, cols].astype(F32))).astype(o_ref.dtype)


def _mlstm_out(hs, proj, head_gain, tm=1024):
    s_len = proj.shape[0]
    width = ML_OUT_HEADS * ML_V_DIM
    assert OFF_MO % width == 0
    ob = OFF_MO // width
    return pl.pallas_call(
        _mlstm_out_kernel,
        grid=(s_len // tm, ML_V_W // width),
        in_specs=[pl.BlockSpec((2, tm, width), lambda i, h: (0, i, h)),
                  pl.BlockSpec((tm, width), lambda i, h: (i, ob + h)),
                  pl.BlockSpec((1, 1, width), lambda i, h: (h, 0, 0))],
        out_specs=pl.BlockSpec((tm, width), lambda i, h: (i, h)),
        out_shape=jax.ShapeDtypeStruct((s_len, ML_V_W), BF16),
        compiler_params=_cparams(("parallel", "parallel")),
        name="mlstm_out",
    )(hs, proj, head_gain.reshape(ML_V_W // width, 1, width))


def _branch_kernel(att_ref, mem_ref, wa_ref, wm_ref, ga_ref, gm_ref, ba_ref, bm_ref, o_ref,
                   wab_ref, wmb_ref):
    @pl.when(pl.program_id(1) == 0)
    def _():
        wab_ref[...] = wa_ref[0].astype(BF16)
        wmb_ref[...] = wm_ref[0].astype(BF16)

    ya = jnp.dot(att_ref[...], wab_ref[...], preferred_element_type=F32)
    ym = jnp.dot(mem_ref[...], wmb_ref[...], preferred_element_type=F32)
    g_a = jax.nn.sigmoid(ga_ref[...].astype(F32) + ba_ref[0])
    g_m = jax.nn.sigmoid(gm_ref[...].astype(F32) + bm_ref[0])
    o_ref[...] = (g_a * ya + g_m * ym).astype(o_ref.dtype)


def _branch_mix(att, mem, w_a, w_m, layer, proj, gate_bias, tm=512, tn=512):
    m = att.shape[0]
    ka, km, d = w_a.shape[1], w_m.shape[1], w_a.shape[2]
    bias = gate_bias.reshape(-1, 1, d)
    gab = OFF_GA // tn
    gmb = (OFF_GA + d) // tn
    return pl.pallas_call(
        _branch_kernel,
        grid=(d // tn, m // tm),
        in_specs=[pl.BlockSpec((tm, ka), lambda j, i: (i, 0)),
                  pl.BlockSpec((tm, km), lambda j, i: (i, 0)),
                  pl.BlockSpec((1, ka, tn), lambda j, i: (layer, 0, j)),
                  pl.BlockSpec((1, km, tn), lambda j, i: (layer, 0, j)),
                  pl.BlockSpec((tm, tn), lambda j, i: (i, gab + j)),
                  pl.BlockSpec((tm, tn), lambda j, i: (i, gmb + j)),
                  pl.BlockSpec((1, 1, tn), lambda j, i: (2 * layer, 0, j)),
                  pl.BlockSpec((1, 1, tn), lambda j, i: (2 * layer + 1, 0, j))],
        out_specs=pl.BlockSpec((tm, tn), lambda j, i: (i, j)),
        out_shape=jax.ShapeDtypeStruct((m, d), BF16),
        scratch_shapes=[pltpu.VMEM((ka, tn), BF16), pltpu.VMEM((km, tn), BF16)],
        compiler_params=_cparams(("parallel", "arbitrary")),
        name="branch_mix",
    )(att, mem, w_a, w_m, proj, proj, bias, bias)


def _pack_bf16_pair(lo, hi):
    lo_bits = lax.bitcast_convert_type(lo.astype(BF16).astype(F32), jnp.uint32)
    hi_bits = lax.bitcast_convert_type(hi.astype(BF16).astype(F32), jnp.uint32)
    return (lo_bits >> 16) | (hi_bits & jnp.uint32(0xFFFF0000))


def _unpack_bf16_pair(word):
    lo = lax.bitcast_convert_type(word << 16, F32)
    hi = lax.bitcast_convert_type(word & jnp.uint32(0xFFFF0000), F32)
    return lo, hi


def _norm_router_kernel(x_ref, g_ref, whi_ref, wlo_ref, br_ref, h_ref, lg_ref):
    x = x_ref[...]
    h = x * lax.rsqrt(jnp.mean(x * x, axis=-1, keepdims=True) + EPS) * g_ref[...]
    half = h.shape[1] // 2
    h_ref[...] = _pack_bf16_pair(h[:, :half], h[:, half:])
    h_hi = h.astype(BF16)
    h_lo = (h - h_hi.astype(F32)).astype(BF16)
    w_hi = whi_ref[...]
    logits = jnp.dot(h_hi, w_hi, preferred_element_type=F32)
    logits = logits + jnp.dot(h_lo, w_hi, preferred_element_type=F32)
    logits = logits + jnp.dot(h_hi, wlo_ref[...], preferred_element_type=F32)
    lg_ref[...] = logits + br_ref[...]


def _norm_router(x, gain, w_r, b_r, tm=256):
    m, d = x.shape
    n = w_r.shape[1]
    w_hi = w_r.astype(BF16)
    w_lo = (w_r - w_hi.astype(F32)).astype(BF16)
    return pl.pallas_call(
        _norm_router_kernel,
        grid=(m // tm,),
        in_specs=[pl.BlockSpec((tm, d), lambda i: (i, 0)),
                  pl.BlockSpec((1, d), lambda i: (0, 0)),
                  pl.BlockSpec((d, n), lambda i: (0, 0)),
                  pl.BlockSpec((d, n), lambda i: (0, 0)),
                  pl.BlockSpec((1, n), lambda i: (0, 0))],
        out_specs=[pl.BlockSpec((tm, d // 2), lambda i: (i, 0)),
                   pl.BlockSpec((tm, n), lambda i: (i, 0))],
        out_shape=[jax.ShapeDtypeStruct((m, d // 2), jnp.uint32),
                   jax.ShapeDtypeStruct((m, n), F32)],
        compiler_params=_cparams(("parallel",)),
        name="norm_router",
    )(x, gain.reshape(1, d), w_hi, w_lo, b_r)


def _moe_kernel(blk_e_ref, cnt_ref, first_ref, tok_ref, h_hbm, wg_ref, wu_ref,
                wd_lo_ref, wd_hi_ref, y_ref, xg_ref, xb_ref, act_ref, sem, *, n_ff_tiles, n_steps,
                n_blk):
    b = pl.program_id(0)
    j = pl.program_id(1)
    rows = MOE_ROWS
    tf = MOE_FF_TILE
    grp = DMA_ISSUE_UNROLL
    cnt = cnt_ref[b]
    used = cnt > 0
    up_phase = j < n_ff_tiles
    n_sub = (cnt + MOE_SUB_ROWS - 1) // MOE_SUB_ROWS

    def row_copy(i, tok):
        return pltpu.make_async_copy(h_hbm.at[pl.ds(tok, 1)], xg_ref.at[pl.ds(i, 1)], sem)

    def issue_rows(blk):
        first = first_ref[blk]
        last_entry = tok_ref.shape[0] - 1

        def body(gi, carry):
            for r in range(grp):
                i = gi * grp + r
                row_copy(i, tok_ref[jnp.minimum(first + i, last_entry)]).start()
            return carry
        lax.fori_loop(0, (cnt_ref[blk] + grp - 1) // grp, body, 0)

    def wait_rows(count):
        def body(gi, carry):
            for r in range(grp):
                row_copy(gi * grp + r, 0).wait()
            return carry
        lax.fori_loop(0, (count + grp - 1) // grp, body, 0)

    def land_rows(blk):
        wait_rows(cnt_ref[blk])
        half = xg_ref.shape[1]
        lo, hi = _unpack_bf16_pair(xg_ref[...])
        xb_ref[blk % 2, :, 0:half] = lo.astype(BF16)
        xb_ref[blk % 2, :, half:2 * half] = hi.astype(BF16)

    @pl.when(jnp.logical_and(b == 0, j == 0))
    def _():
        xg_ref[...] = jnp.zeros_like(xg_ref)
        issue_rows(0)
        land_rows(0)
        if n_blk > 1:
            issue_rows(1)

    @pl.when(jnp.logical_and(used, j == n_steps - 1))
    def _():
        @pl.when(b + 1 < n_blk)
        def _():
            @pl.when(cnt_ref[jnp.minimum(b + 1, n_blk - 1)] > 0)
            def _():
                land_rows(b + 1)

            @pl.when(b + 2 < n_blk)
            def _():
                issue_rows(jnp.minimum(b + 2, n_blk - 1))

    for m in range(MOE_SUB_ROWS, rows + 1, MOE_SUB_ROWS):
        live = jnp.logical_and(used, n_sub == m // MOE_SUB_ROWS)

        @pl.when(jnp.logical_and(live, up_phase))
        def _(m=m):
            x = xb_ref[b % 2, 0:m, :]
            g = jnp.dot(x, wg_ref[0, 0].astype(BF16), preferred_element_type=F32)
            u = jnp.dot(x, wu_ref[0, 0].astype(BF16), preferred_element_type=F32)
            act_ref[j, 0:m, :] = (g * jax.nn.sigmoid(g) * u).astype(BF16)

        @pl.when(jnp.logical_and(live, jnp.logical_not(up_phase)))
        def _(m=m):
            def down(wd_ref):
                acc = jnp.dot(act_ref[0, 0:m, :], wd_ref[0, 0, 0:tf, :].astype(BF16),
                              preferred_element_type=F32)
                for t in range(1, n_ff_tiles):
                    acc = acc + jnp.dot(act_ref[t, 0:m, :],
                                        wd_ref[0, 0, t * tf:(t + 1) * tf, :].astype(BF16),
                                        preferred_element_type=F32)
                return acc

            y_ref[0:m, :] = _pack_bf16_pair(down(wd_lo_ref), down(wd_hi_ref))
            if m < rows:
                y_ref[m:rows, :] = jnp.zeros((rows - m, y_ref.shape[1]), y_ref.dtype)

    @pl.when(jnp.logical_and(jnp.logical_not(used), jnp.logical_not(up_phase)))
    def _():
        y_ref[...] = jnp.zeros_like(y_ref)


def _moe_experts(h2p, blk_expert, blk_cnt, blk_first, tok_sorted, w_gate, w_up, w_down, layer):
    d = 2 * h2p.shape[1]
    d_ff = w_gate.shape[3]
    rows = MOE_ROWS
    n_blk = blk_expert.shape[0]
    tf = MOE_FF_TILE
    tn = MOE_OUT_TILE
    n_f = d_ff // tf
    n_o = d // 2 // tn

    def ff_tile(b, j, cnt):
        return jnp.where(cnt[b] > 0, jnp.minimum(j, n_f - 1), n_f - 1)

    def out_tile(b, j):
        return jnp.clip(j - n_f, 0, n_o - 1)

    def w_down_block(b, j, be, cnt, hi):
        stay = jnp.logical_or(j < n_f, cnt[b] == 0)
        expert = jnp.where(j < n_f, be[jnp.maximum(b - 1, 0)], be[b])
        return layer, expert, 0, hi * n_o + jnp.where(stay, n_o - 1, j - n_f)

    grid_spec = pltpu.PrefetchScalarGridSpec(
        num_scalar_prefetch=4,
        grid=(n_blk, n_f + n_o),
        in_specs=[pl.BlockSpec(memory_space=pl.ANY),
                  pl.BlockSpec((1, 1, d, tf),
                               lambda b, j, be, cnt, *_: (layer, be[b], 0, ff_tile(b, j, cnt))),
                  pl.BlockSpec((1, 1, d, tf),
                               lambda b, j, be, cnt, *_: (layer, be[b], 0, ff_tile(b, j, cnt))),
                  pl.BlockSpec((1, 1, d_ff, tn),
                               lambda b, j, be, cnt, *_: w_down_block(b, j, be, cnt, 0)),
                  pl.BlockSpec((1, 1, d_ff, tn),
                               lambda b, j, be, cnt, *_: w_down_block(b, j, be, cnt, 1))],
        out_specs=pl.BlockSpec((rows, tn), lambda b, j, *_: (b, out_tile(b, j))),
        scratch_shapes=[pltpu.VMEM((rows, d // 2), jnp.uint32),
                        pltpu.VMEM((2, rows, d), BF16),
                        pltpu.VMEM((n_f, rows, tf), BF16),
                        pltpu.SemaphoreType.DMA(())],
    )
    return pl.pallas_call(
        functools.partial(_moe_kernel, n_ff_tiles=n_f, n_steps=n_f + n_o, n_blk=n_blk),
        grid_spec=grid_spec,
        out_shape=jax.ShapeDtypeStruct((n_blk * rows, d // 2), jnp.uint32),
        compiler_params=_cparams(("arbitrary", "arbitrary")),
        name="moe_experts",
    )(blk_expert, blk_cnt, blk_first, tok_sorted, h2p, w_gate, w_up, w_down, w_down)


def _combine_kernel(pos_ref, pos_next_ref, x_ref, w_ref, y_hbm, o_ref, buf_ref, sem, *, n_steps):
    rows = COMBINE_ROWS
    n_copy = TOP_K * rows
    step = pl.program_id(0)
    slot = step % 2

    def row_copy(s, i, src):
        return pltpu.make_async_copy(y_hbm.at[pl.ds(src, 1)], buf_ref.at[s, pl.ds(i, 1)], sem.at[s])

    def issue(s, idx_ref):
        def body(i, carry):
            row_copy(s, i, idx_ref[0, 0, i]).start()
            return carry
        lax.fori_loop(0, n_copy, body, 0, unroll=DMA_ISSUE_UNROLL)

    @pl.when(step == 0)
    def _():
        issue(0, pos_ref)

    def drain(i, carry):
        row_copy(slot, i, 0).wait()
        return carry

    lax.fori_loop(0, n_copy, drain, 0, unroll=DMA_ISSUE_UNROLL)

    half = x_ref.shape[1] // 2
    chunk = COMBINE_CHUNK
    n_chunks = rows // chunk
    per_chunk = n_copy // n_chunks

    def chunk_body(prefetch):
        def body(c, carry):
            if prefetch:
                for r in range(per_chunk):
                    i = c * per_chunk + r
                    row_copy(1 - slot, i, pos_next_ref[0, 0, i]).start()
            r0 = pl.multiple_of(c * chunk, chunk)
            lo0, hi0 = _unpack_bf16_pair(buf_ref[slot, pl.ds(r0, chunk), :])
            lo1, hi1 = _unpack_bf16_pair(buf_ref[slot, pl.ds(rows + r0, chunk), :])
            w = w_ref[pl.ds(r0, chunk), :]
            w0 = w[:, 0:1]
            w1 = w[:, 1:2]
            o_ref[pl.ds(r0, chunk), 0:half] = x_ref[pl.ds(r0, chunk), 0:half] + (w0 * lo0 + w1 * lo1)
            o_ref[pl.ds(r0, chunk), half:2 * half] = (
                x_ref[pl.ds(r0, chunk), half:2 * half] + (w0 * hi0 + w1 * hi1))
            return carry
        return body

    @pl.when(step + 1 < n_steps)
    def _():
        lax.fori_loop(0, n_chunks, chunk_body(True), 0)

    @pl.when(step + 1 >= n_steps)
    def _():
        lax.fori_loop(0, n_chunks, chunk_body(False), 0)


def _moe_combine(x1, ys, pos, weights):
    n_tok, d = x1.shape
    rows = COMBINE_ROWS
    n_blk = n_tok // rows
    pos_b = pos.reshape(n_blk, rows, TOP_K).transpose(0, 2, 1).reshape(n_blk, 1, TOP_K * rows)
    idx_block = (1, 1, TOP_K * rows)
    return pl.pallas_call(
        functools.partial(_combine_kernel, n_steps=n_blk),
        grid=(n_blk,),
        in_specs=[pl.BlockSpec(idx_block, lambda i: (i, 0, 0), memory_space=pltpu.SMEM),
                  pl.BlockSpec(idx_block, lambda i: (jnp.minimum(i + 1, n_blk - 1), 0, 0),
                               memory_space=pltpu.SMEM),
                  pl.BlockSpec((rows, d), lambda i: (i, 0)),
                  pl.BlockSpec((rows, TOP_K), lambda i: (i, 0)),
                  pl.BlockSpec(memory_space=pl.ANY)],
        out_specs=pl.BlockSpec((rows, d), lambda i: (i, 0)),
        out_shape=jax.ShapeDtypeStruct((n_tok, d), F32),
        scratch_shapes=[pltpu.VMEM((2, TOP_K * rows, d // 2), jnp.uint32),
                        pltpu.SemaphoreType.DMA((2,))],
        compiler_params=_cparams(("arbitrary",)),
        name="moe_combine",
    )(pos_b, pos_b, x1, weights, ys)


def _route(logits, n_tok):
    glog = logits[:, :N_GROUPS]
    gprob = jax.nn.softmax(glog, axis=-1)
    g_sel = jnp.argmax(glog, axis=-1)
    p_g = jnp.max(gprob, axis=-1)
    elog = logits[:, N_GROUPS:N_GROUPS + N_EXPERTS]
    col_group = jnp.arange(N_EXPERTS, dtype=jnp.int32) // EXPERTS_PER_GROUP
    elog = jnp.where(col_group[None, :] == g_sel[:, None], elog, -jnp.inf)
    cols = jnp.arange(N_EXPERTS, dtype=jnp.int32)[None, :]
    top_v, top_i = [], []
    for _ in range(TOP_K):
        idx = jnp.argmax(elog, axis=-1).astype(jnp.int32)
        top_v.append(jnp.max(elog, axis=-1))
        top_i.append(idx)
        elog = jnp.where(cols == idx[:, None], -jnp.inf, elog)
    top_v = jnp.stack(top_v, axis=1)
    expert_ids = jnp.stack(top_i, axis=1)
    weights = p_g[:, None] * jax.nn.softmax(top_v, axis=-1)

    n_assign = n_tok * TOP_K
    rows = MOE_ROWS
    flat_e = expert_ids.reshape(n_assign)
    a_ids = jnp.arange(n_assign, dtype=jnp.int32)
    skey = jnp.sort(flat_e * n_assign + a_ids)
    order = skey % n_assign
    inv = jnp.argsort(order).astype(jnp.int32)
    experts = jnp.arange(N_EXPERTS, dtype=jnp.int32)
    counts = jnp.sum((flat_e[:, None] == experts[None, :]).astype(jnp.int32), axis=0)
    blocks = (counts + rows - 1) // rows
    pad_end = jnp.cumsum(blocks * rows)
    pad_start = pad_end - blocks * rows
    start = jnp.cumsum(counts) - counts
    n_slots = n_assign + N_EXPERTS * rows
    n_blk = n_slots // rows
    n_used = jnp.sum(blocks).astype(jnp.int32)
    blk_ids = jnp.arange(n_blk, dtype=jnp.int32)
    blk_expert = jnp.minimum(
        jnp.sum((pad_end[None, :] <= (blk_ids * rows)[:, None]).astype(jnp.int32), axis=1),
        N_EXPERTS - 1)
    blk_cnt = jnp.clip(counts[blk_expert] - (blk_ids * rows - pad_start[blk_expert]), 0, rows)
    blk_cnt = jnp.where(blk_ids < n_used, blk_cnt, 0).astype(jnp.int32)
    blk_first = (start[blk_expert] + blk_ids * rows - pad_start[blk_expert]).astype(jnp.int32)
    tok_sorted = (order // TOP_K).astype(jnp.int32)
    blk_expert = jnp.where(blk_ids < n_used, blk_expert,
                           blk_expert[jnp.maximum(n_used - 1, 0)]).astype(jnp.int32)
    shift = jnp.sum(jnp.where(flat_e[:, None] == experts[None, :], (pad_start - start)[None, :], 0),
                    axis=1)
    pos = (inv + shift).astype(jnp.int32).reshape(n_tok, TOP_K)
    return blk_expert, blk_cnt, blk_first, tok_sorted, pos, weights


def kernel(x, norm1_gain, w_in, attn_q_norm_gain, attn_k_norm_gain, mlstm_gate_bias,
           mlstm_head_norm_gain, branch_gate_bias, w_attn_branch, w_mlstm_branch, w_out,
           norm2_gain, w_router_group, b_router_group, w_router_expert, b_router_expert,
           w_expert_gate, w_expert_up, w_expert_down):
    bsz, s_len, d = x.shape
    assert bsz == 1
    depth = w_in.shape[0]
    xs = x.reshape(s_len, d)
    w_in_t = jnp.swapaxes(w_in, 1, 2)
    n_proj = OFF_GA + 2 * d
    tn = 512

    def proj_rows(j):
        return j * tn + jnp.where(j * tn >= OFF_GA, N_ML_GATES, 0)

    for l in range(depth):
        h = _rmsnorm(xs, norm1_gain[l])
        proj = _matmul_nt_paired(h, w_in_t, l, proj_rows, n_proj, BF16, 1024, tn, "in_proj")
        gates = _matmul_nt(h, w_in_t, l, lambda j: OFF_MG + j * LANES, LANES, F32, 1024, LANES,
                           "gate_proj")[:, :N_ML_GATES]

        att = _attention(proj, attn_q_norm_gain[l], attn_k_norm_gain[l])

        hs = _mlstm(proj, gates, mlstm_gate_bias[l])
        mem = _mlstm_out(hs, proj, mlstm_head_norm_gain[l])

        merged = _branch_mix(att, mem, w_attn_branch, w_mlstm_branch, l, proj, branch_gate_bias)
        x1 = _matmul_residual(merged, w_out, l, xs, 1024, 512)

        w_r = jnp.concatenate([w_router_group[l], w_router_expert[l]], axis=1)
        b_r = jnp.concatenate([b_router_group[l], b_router_expert[l]])
        n_r = w_r.shape[1]
        w_r = jnp.pad(w_r, ((0, 0), (0, LANES - n_r)))
        b_r = jnp.pad(b_r, (0, LANES - n_r)).reshape(1, LANES)
        h2, logits = _norm_router(x1, norm2_gain[l], w_r, b_r)

        blk_expert, blk_cnt, blk_first, tok_sorted, pos, weights = _route(logits, s_len)
        ys = _moe_experts(h2, blk_expert, blk_cnt, blk_first, tok_sorted,
                          w_expert_gate, w_expert_up, w_expert_down, l)
        xs = _moe_combine(x1, ys, pos, weights)
    return xs.reshape(bsz, s_len, d)
```
